```python
import math
import jax, jax.numpy as jnp
from jax import lax
import numpy as np

D_MODEL = 4096
BATCH = 4
SEQ = 2048
DEPTH = 4
DEC_BATCH = 8
DEC_SEQ = 8
PAST_LEN = 8192
PAGE_SIZE = 128

N_A_LAYERS = DEPTH // 2
N_B_LAYERS = DEPTH - N_A_LAYERS
A_HEAD_K = 128
A_HEADS = D_MODEL // A_HEAD_K
A_HEAD_V = D_MODEL // A_HEADS
GLA_CHUNK = 16
HEAD_DIM = 128
B_HEADS = D_MODEL // HEAD_DIM
WINDOWS = (128, 512, 2048)
DILATIONS = (1, 4, 16)
N_BRANCH = len(WINDOWS)
SUB_WINDOW = 128
MAX_WINDOW = max(WINDOWS)
N_BUCKETS = 32
BUCKET_MAX_DIST = MAX_WINDOW
D_FF = 11008
CONV_W = 3
EPS = 1e-6

kernel_name = "yoco_hgrn2_dilated_convffn_step"

F32 = jnp.float32


def rmsnorm(x, g):
    xf = x.astype(F32)
    y = xf * lax.rsqrt(jnp.mean(xf * xf, axis=-1, keepdims=True) + EPS)
    return (y * g.astype(F32)).astype(x.dtype)


def _lower_bounds(lb_raw):
    p = jax.nn.softmax(lb_raw.astype(F32), axis=0)
    c = jnp.cumsum(p, axis=0)
    return c - c[0:1]


def gla_chunked(q, k, v, log_f, s0):
    n, t, h, dk = q.shape
    dv = v.shape[-1]
    c = math.gcd(t, GLA_CHUNK)
    nc = t // c

    def to_chunks(a):
        return a.reshape(n, nc, c, h, a.shape[-1]).transpose(1, 0, 3, 2, 4)

    causal = jnp.asarray(np.tril(np.ones((c, c), dtype=bool)))

    def step(s, inp):
        qc, kc, vc, gc = inp
        b = jnp.cumsum(gc, axis=2)
        o_inter = jnp.einsum('nhtk,nhkv->nhtv', qc * jnp.exp(b), s)
        diff = b[:, :, :, None, :] - b[:, :, None, :, :]
        decay = jnp.exp(jnp.where(causal[:, :, None], diff, -jnp.inf))
        attn = jnp.einsum('nhtk,nhsk,nhtsk->nhts', qc, kc, decay)
        o = o_inter + jnp.einsum('nhts,nhsv->nhtv', attn, vc)
        b_last = b[:, :, -1:, :]
        s_new = jnp.exp(b_last[:, :, 0])[..., None] * s + jnp.einsum(
            'nhsk,nhsv->nhkv', kc * jnp.exp(b_last - b), vc)
        return s_new, o

    s_fin, o = lax.scan(step, s0, (to_chunks(q), to_chunks(k), to_chunks(v), to_chunks(log_f)))
    o = o.transpose(1, 0, 3, 2, 4).reshape(n, t, h, dv)
    return o, s_fin


def hgrn2_mixer(h, w_in, lb, out_gain, w_out, s0):
    n, t, _ = h.shape
    q, fz, i, g = jnp.split(h @ w_in, 4, axis=-1)
    fz = fz.astype(F32)
    lb = lb.astype(F32)
    log_f = jnp.logaddexp(jnp.log(lb), jnp.log1p(-lb) + jax.nn.log_sigmoid(fz))
    k = (1.0 - lb) * jax.nn.sigmoid(-fz)
    q = jax.nn.silu(q.astype(F32)) * (A_HEAD_K ** -0.5)
    shp_k = (n, t, A_HEADS, A_HEAD_K)
    shp_v = (n, t, A_HEADS, A_HEAD_V)
    o, s_new = gla_chunked(q.reshape(shp_k), k.reshape(shp_k), i.astype(F32).reshape(shp_v),
                           log_f.reshape(shp_k), s0.astype(F32))
    o = o * lax.rsqrt(jnp.mean(o * o, axis=-1, keepdims=True) + EPS)
    o = o * out_gain.astype(F32).reshape(A_HEADS, A_HEAD_V)
    o = o.reshape(n, t, D_MODEL) * jax.nn.silu(g.astype(F32))
    return o.astype(h.dtype) @ w_out, s_new


def _t5_buckets(dist):
    max_exact = N_BUCKETS // 2
    n = np.asarray(dist, dtype=np.int64)
    large = max_exact + (np.log(np.maximum(n, 1) / max_exact)
                         / np.log(BUCKET_MAX_DIST / max_exact)
                         * (N_BUCKETS - max_exact)).astype(np.int64)
    large = np.minimum(large, N_BUCKETS - 1)
    return np.where(n < max_exact, n, large).astype(np.int32)


def _branch_bias(rel_bias, g):
    dist = np.arange(SUB_WINDOW + 1) * DILATIONS[g]
    tab = rel_bias[:, g * B_HEADS:(g + 1) * B_HEADS].astype(F32)
    return tab[_t5_buckets(dist)].T


def dilated_branch_prompt(q, k, v, d, bias):
    n, t, h, dh = q.shape
    m = t // d
    bl = SUB_WINDOW
    nb = -(-m // bl)
    mp = nb * bl

    def blocks(a):
        a = a.astype(F32).reshape(n, m, d, h, dh)
        a = jnp.pad(a, ((0, 0), (0, mp - m), (0, 0), (0, 0), (0, 0)))
        return a.reshape(n, nb, bl, d, h, dh)

    def with_prev(a):
        prev = jnp.pad(a, ((0, 0), (1, 0), (0, 0), (0, 0), (0, 0), (0, 0)))[:, :-1]
        return jnp.concatenate([prev, a], axis=2)

    qb = blocks(q)
    kb = with_prev(blocks(k))
    vb = with_prev(blocks(v))
    s = jnp.einsum('nbqrhd,nbkrhd->nbrhqk', qb, kb) * (dh ** -0.5)
    iq = np.arange(bl)[:, None]
    ik = np.arange(2 * bl)[None, :]
    off = iq + bl - ik
    blk = np.arange(nb)[:, None, None]
    valid = (off >= 0) & (off <= SUB_WINDOW) & (blk * bl + ik - bl >= 0)
    bias_blk = bias[:, np.clip(off, 0, SUB_WINDOW)]
    s = jnp.where(valid[None, :, None, None], s + bias_blk[None, None, None], -jnp.inf)
    mx = jnp.max(s, axis=-1, keepdims=True)
    p = jnp.exp(s - mx)
    l = jnp.sum(p, axis=-1, keepdims=True)
    o = jnp.einsum('nbrhqk,nbkrhd->nbqrhd', p / l, vb)
    o = o.reshape(n, mp, d, h, dh)[:, :m].reshape(n, t, h, dh)
    lse = (mx + jnp.log(l))[..., 0].transpose(0, 1, 4, 2, 3)
    lse = lse.reshape(n, mp, d, h)[:, :m].reshape(n, t, h)
    return o, lse


def dilated_branch_sample(q, k_all, v_all, p_len, d, bias):
    n, t, h, dh = q.shape
    j = np.arange(SUB_WINDOW + 1)
    idx = p_len + np.arange(t)[:, None] - j[None, :] * d
    valid = idx >= 0
    flat = np.maximum(idx, 0).reshape(-1)
    kg = jnp.take(k_all, flat, axis=1).reshape(n, t, j.size, h, dh).astype(F32)
    vg = jnp.take(v_all, flat, axis=1).reshape(n, t, j.size, h, dh).astype(F32)
    s = jnp.einsum('nthd,ntjhd->nthj', q.astype(F32), kg) * (dh ** -0.5) + bias[None, None]
    s = jnp.where(valid[None, :, None, :], s, -jnp.inf)
    mx = jnp.max(s, axis=-1, keepdims=True)
    p = jnp.exp(s - mx)
    l = jnp.sum(p, axis=-1, keepdims=True)
    o = jnp.einsum('nthj,ntjhd->nthd', p / l, vg)
    return o, (mx + jnp.log(l))[..., 0]


def dilated_mixer(h, w_q, w_out, k, v, rel_bias, k_prev, v_prev):
    n, t, _ = h.shape
    q = (h @ w_q).reshape(n, t, N_BRANCH, B_HEADS, HEAD_DIM)
    outs, lses = [], []
    if k_prev is not None:
        p_len = k_prev.shape[1]
        k_all = jnp.concatenate([k_prev.astype(k.dtype), k], axis=1)
        v_all = jnp.concatenate([v_prev.astype(v.dtype), v], axis=1)
    for g in range(N_BRANCH):
        bias = _branch_bias(rel_bias, g)
        if k_prev is None:
            o, lse = dilated_branch_prompt(q[:, :, g], k, v, DILATIONS[g], bias)
        else:
            o, lse = dilated_branch_sample(q[:, :, g], k_all, v_all, p_len, DILATIONS[g], bias)
        outs.append(o)
        lses.append(lse)
    wts = jax.nn.softmax(jnp.stack(lses), axis=0)
    o = jnp.sum(wts[..., None] * jnp.stack(outs), axis=0).reshape(n, t, B_HEADS * HEAD_DIM)
    return o.astype(h.dtype) @ w_out


def conv_ffn(h, w_up, conv_w, conv_b, w_down, prev):
    t = h.shape[1]
    a, b = jnp.split(h @ w_up, 2, axis=-1)
    a_ext = jnp.concatenate([prev.astype(a.dtype), a], axis=1)
    conv = conv_b + conv_w[0] * a_ext[:, 0:t]
    for i in range(1, CONV_W):
        conv = conv + conv_w[i] * a_ext[:, i:i + t]
    out = (jax.nn.silu(conv) * b) @ w_down
    return out, a_ext[:, t:]


def _trunk(x, hgrn_s0, conv_s0, k_prev, v_prev, a_w_in, a_lb_raw, a_out_gain, a_w_out,
           kv_norm, w_kv, b_w_q, b_w_out, rel_bias, ffn_w_up, ffn_conv_w, ffn_conv_b,
           ffn_w_down, norm_mix, norm_ffn, norm_final):
    n, t, _ = x.shape
    lbs = _lower_bounds(a_lb_raw)
    hgrn_new, conv_new = [], []
    k = v = None
    for l in range(DEPTH):
        if l < N_A_LAYERS:
            o, s = hgrn2_mixer(rmsnorm(x, norm_mix[l]), a_w_in[l], lbs[l], a_out_gain[l],
                               a_w_out[l], hgrn_s0[l])
            hgrn_new.append(s.astype(x.dtype))
        else:
            if l == N_A_LAYERS:
                kk, vv = jnp.split(rmsnorm(x, kv_norm) @ w_kv, 2, axis=-1)
                k = kk.reshape(n, t, B_HEADS, HEAD_DIM)
                v = vv.reshape(n, t, B_HEADS, HEAD_DIM)
            jb = l - N_A_LAYERS
            o = dilated_mixer(rmsnorm(x, norm_mix[l]), b_w_q[jb], b_w_out[jb], k, v,
                              rel_bias, k_prev, v_prev)
        x = x + o
        o, c = conv_ffn(rmsnorm(x, norm_ffn[l]), ffn_w_up[l], ffn_conv_w[l], ffn_conv_b[l],
                        ffn_w_down[l], conv_s0[l])
        x = x + o
        conv_new.append(c)
    return rmsnorm(x, norm_final), jnp.stack(hgrn_new), jnp.stack(conv_new), k, v


def setup_inputs(seed: int = 0) -> dict:
    key = jax.random.key(seed)
    ks = jax.random.split(key, 24)
    D = D_MODEL
    buf = min(MAX_WINDOW, PAST_LEN)

    def nrm(k, shape, scale):
        return jax.random.normal(k, shape, F32) * scale

    return {
        "x_prompt": nrm(ks[0], (BATCH, SEQ, D), 1.0),
        "x_sample": nrm(ks[1], (DEC_BATCH, DEC_SEQ, D), 1.0),
        "state_hgrn": nrm(ks[2], (N_A_LAYERS, DEC_BATCH, A_HEADS, A_HEAD_K, A_HEAD_V), 0.5),
        "state_ffn_conv": nrm(ks[3], (DEPTH, DEC_BATCH, CONV_W - 1, D_FF), 1.0),
        "cache_k_win": nrm(ks[4], (DEC_BATCH, buf, B_HEADS, HEAD_DIM), 1.0),
        "cache_v_win": nrm(ks[5], (DEC_BATCH, buf, B_HEADS, HEAD_DIM), 1.0),
        "a_w_in": nrm(ks[6], (N_A_LAYERS, D, 4 * D), D ** -0.5),
        "a_lb_raw": nrm(ks[7], (N_A_LAYERS, D), 1.0),
        "a_out_gain": 1.0 + nrm(ks[8], (N_A_LAYERS, D), 0.02),
        "a_w_out": nrm(ks[9], (N_A_LAYERS, D, D), D ** -0.5),
        "kv_norm": 1.0 + nrm(ks[10], (D,), 0.02),
        "w_kv": nrm(ks[11], (D, 2 * B_HEADS * HEAD_DIM), D ** -0.5),
        "b_w_q": nrm(ks[12], (N_B_LAYERS, D, N_BRANCH * B_HEADS * HEAD_DIM), D ** -0.5),
        "b_w_out": nrm(ks[13], (N_B_LAYERS, B_HEADS * HEAD_DIM, D), (B_HEADS * HEAD_DIM) ** -0.5),
        "rel_bias": nrm(ks[14], (N_BUCKETS, N_BRANCH * B_HEADS), 0.5),
        "ffn_w_up": nrm(ks[15], (DEPTH, D, 2 * D_FF), D ** -0.5),
        "ffn_conv_w": nrm(ks[16], (DEPTH, CONV_W, D_FF), CONV_W ** -0.5),
        "ffn_conv_b": nrm(ks[17], (DEPTH, D_FF), 0.02),
        "ffn_w_down": nrm(ks[18], (DEPTH, D_FF, D), D_FF ** -0.5),
        "norm_mix": 1.0 + nrm(ks[19], (DEPTH, D), 0.02),
        "norm_ffn": 1.0 + nrm(ks[20], (DEPTH, D), 0.02),
        "norm_final": 1.0 + nrm(ks[21], (D,), 0.02),
    }


def reference(x_prompt, x_sample, state_hgrn, state_ffn_conv, cache_k_win, cache_v_win,
              a_w_in, a_lb_raw, a_out_gain, a_w_out, kv_norm, w_kv, b_w_q, b_w_out, rel_bias,
              ffn_w_up, ffn_conv_w, ffn_conv_b, ffn_w_down, norm_mix, norm_ffn, norm_final):
    nb_p, t_p, _ = x_prompt.shape
    h0 = jnp.zeros((N_A_LAYERS, nb_p, A_HEADS, A_HEAD_K, A_HEAD_V), F32)
    c0 = jnp.zeros((DEPTH, nb_p, CONV_W - 1, D_FF), x_prompt.dtype)
    y_p, h_p, c_p, k_p, v_p = _trunk(
        x_prompt, h0, c0, None, None, a_w_in, a_lb_raw, a_out_gain, a_w_out, kv_norm, w_kv,
        b_w_q, b_w_out, rel_bias, ffn_w_up, ffn_conv_w, ffn_conv_b, ffn_w_down, norm_mix,
        norm_ffn, norm_final)
    y_s, h_s, c_s, k_s, v_s = _trunk(
        x_sample, state_hgrn, state_ffn_conv, cache_k_win, cache_v_win, a_w_in, a_lb_raw,
        a_out_gain, a_w_out, kv_norm, w_kv, b_w_q, b_w_out, rel_bias, ffn_w_up, ffn_conv_w,
        ffn_conv_b, ffn_w_down, norm_mix, norm_ffn, norm_final)
    keep = min(MAX_WINDOW, t_p)
    return (y_p, y_s, h_p, h_s, c_p, c_s, k_p[:, -keep:], v_p[:, -keep:], k_s, v_s)
```

```python
import functools
import math

import numpy as np
import jax
import jax.numpy as jnp
from jax import lax
from jax.experimental import pallas as pl
from jax.experimental.pallas import tpu as pltpu

F32 = jnp.float32
BF16 = jnp.bfloat16

V7X_VMEM_BYTES = 64 * 1024 * 1024
V7X_LANES = 128
V7X_SUBLANES = 8
COMPILER_SCRATCH_BYTES = 12 * 1024 * 1024

HEAD = 128
GLA_CHUNK = 128
SUB_WINDOW = 128
DILATIONS = (1, 4, 16)
N_BUCKETS = 32
BUCKET_MAX_DIST = 2048
CONV_W = 3
EPS = 1e-6
NEG = -1e30


def _vmem_limit(block_bytes):
    return int(min(block_bytes + COMPILER_SCRATCH_BYTES, V7X_VMEM_BYTES - (2 << 20)))


def _nbytes(shape, dtype):
    return int(np.prod(shape)) * jnp.dtype(dtype).itemsize


def _rmsnorm_body(x_ref, g_ref, o_ref):
    x = x_ref[...]
    ms = jnp.mean(x * x, axis=-1, keepdims=True)
    o_ref[...] = (x * lax.rsqrt(ms + EPS) * g_ref[...]).astype(o_ref.dtype)


def _rmsnorm(x, g, out_dtype):
    m, d = x.shape
    tm = min(m, 256)
    assert m % tm == 0
    blocks = 2 * (_nbytes((tm, d), F32) + _nbytes((tm, d), out_dtype))
    return pl.pallas_call(
        _rmsnorm_body,
        grid=(m // tm,),
        in_specs=[pl.BlockSpec((tm, d), lambda i: (i, 0)),
                  pl.BlockSpec((1, d), lambda i: (0, 0))],
        out_specs=pl.BlockSpec((tm, d), lambda i: (i, 0)),
        out_shape=jax.ShapeDtypeStruct((m, d), out_dtype),
        compiler_params=pltpu.CompilerParams(vmem_limit_bytes=_vmem_limit(blocks)),
        name="rmsnorm",
    )(x, g.reshape(1, d))


W_CAST_ROWS = 256


def _cast_weight(w_ref, wb_ref):
    k = w_ref.shape[0]
    assert k % W_CAST_ROWS == 0

    def step(c, carry):
        rows = pl.ds(pl.multiple_of(c * W_CAST_ROWS, W_CAST_ROWS), W_CAST_ROWS)
        wb_ref[rows, :] = w_ref[rows, :].astype(BF16)
        return carry

    lax.fori_loop(0, k // W_CAST_ROWS, step, 0)


def _linear_body(*refs, has_res):
    if has_res:
        xp_ref, xs_ref, w_ref, rp_ref, rs_ref, yp_ref, ys_ref, wb_ref = refs
    else:
        xp_ref, xs_ref, w_ref, yp_ref, ys_ref, wb_ref = refs

    @pl.when(pl.program_id(1) == 0)
    def _new_column_tile():
        _cast_weight(w_ref, wb_ref)
        ys = jnp.dot(xs_ref[...].astype(BF16), wb_ref[...], preferred_element_type=F32)
        if has_res:
            ys = ys + rs_ref[...]
        ys_ref[...] = ys

    yp = jnp.dot(xp_ref[...], wb_ref[...], preferred_element_type=F32)
    if has_res:
        yp = yp + rp_ref[...]
    yp_ref[...] = yp


def _linear_tiles(mp, k, n):
    if k <= 4096:
        tm, tn = 512, 512
    else:
        tm, tn = 256, 256
    return min(tm, mp), min(tn, n)


def _linear(xp, xs, w, *, col0=0, n=None, res=None, name):
    mp, k = xp.shape
    ms = xs.shape[0]
    n = w.shape[1] if n is None else n
    tm, tn = _linear_tiles(mp, k, n)
    assert mp % tm == 0 and n % tn == 0 and col0 % tn == 0
    jb0 = col0 // tn
    in_specs = [pl.BlockSpec((tm, k), lambda j, i: (i, 0)),
                pl.BlockSpec((ms, k), lambda j, i: (0, 0)),
                pl.BlockSpec((k, tn), lambda j, i: (0, j + jb0))]
    args = [xp, xs, w]
    blocks = 2 * (_nbytes((tm, k), BF16) + _nbytes((ms, k), xs.dtype) + _nbytes((k, tn), F32)
                  + _nbytes((tm, tn), F32) + _nbytes((ms, tn), F32)) + _nbytes((k, tn), BF16)
    if res is not None:
        in_specs += [pl.BlockSpec((tm, tn), lambda j, i: (i, j)),
                     pl.BlockSpec((ms, tn), lambda j, i: (0, j))]
        args += list(res)
        blocks += 2 * (_nbytes((tm, tn), F32) + _nbytes((ms, tn), F32))
    return pl.pallas_call(
        functools.partial(_linear_body, has_res=res is not None),
        grid=(n // tn, mp // tm),
        in_specs=in_specs,
        out_specs=[pl.BlockSpec((tm, tn), lambda j, i: (i, j)),
                   pl.BlockSpec((ms, tn), lambda j, i: (0, j))],
        out_shape=[jax.ShapeDtypeStruct((mp, n), F32), jax.ShapeDtypeStruct((ms, n), F32)],
        scratch_shapes=[pltpu.VMEM((k, tn), BF16)],
        compiler_params=pltpu.CompilerParams(
            dimension_semantics=("arbitrary", "arbitrary"),
            vmem_limit_bytes=_vmem_limit(blocks)),
        name=name,
    )(*args)


def _ffn_up_body(hp_ref, hs_ref, wa_ref, wg_ref, cw_ref, cb_ref, exts_ref,
                 actp_ref, acts_ref, tailp_ref, as_ref,
                 wab_ref, wgb_ref, ext_ref, *, tm, tlen, ts):
    i = pl.program_id(1)
    cw = cw_ref[...]
    cb = cb_ref[...]

    def act_fn(a2, a1, a0, b):
        conv = cb + cw[0:1, :] * a2 + cw[1:2, :] * a1 + cw[2:3, :] * a0
        return jax.nn.silu(conv) * b

    @pl.when(i == 0)
    def _new_column_tile():
        _cast_weight(wa_ref, wab_ref)
        _cast_weight(wg_ref, wgb_ref)
        hs = hs_ref[...].astype(BF16)
        a = jnp.dot(hs, wab_ref[...], preferred_element_type=F32)
        b = jnp.dot(hs, wgb_ref[...], preferred_element_type=F32)
        ms = a.shape[0]
        t = lax.broadcasted_iota(jnp.int32, a.shape, 0) % ts
        ext = exts_ref[...]
        a1 = jnp.where(t == 0, pltpu.roll(ext, ms - 1, axis=0), pltpu.roll(a, 1, axis=0))
        a2 = jnp.where(t < 2, ext, pltpu.roll(a, 2, axis=0))
        acts_ref[...] = act_fn(a2, a1, a, b).astype(acts_ref.dtype)
        as_ref[...] = a

    @pl.when((i * tm) % tlen == 0)
    def _sequence_start():
        ext_ref[0:8, :] = jnp.zeros((8, ext_ref.shape[1]), F32)

    hp = hp_ref[...]
    a = jnp.dot(hp, wab_ref[...], preferred_element_type=F32)
    b = jnp.dot(hp, wgb_ref[...], preferred_element_type=F32)
    ext_ref[8:8 + tm, :] = a
    a1 = ext_ref[7:7 + tm, :]
    a2 = ext_ref[6:6 + tm, :]
    actp_ref[...] = act_fn(a2, a1, a, b).astype(actp_ref.dtype)
    tail = a[tm - 8:tm, :]
    ext_ref[0:8, :] = tail

    @pl.when(((i + 1) * tm) % tlen == 0)
    def _sequence_end():
        n = (i * tm) // tlen
        tailp_ref[pl.ds(pl.multiple_of(n * 8, 8), 8), :] = tail


def _ffn_up(hp, hs, w_up, conv_w, conv_b, ext_s, *, tlen, ts):
    mp, d = hp.shape
    ms = hs.shape[0]
    f = w_up.shape[1] // 2
    nseq = mp // tlen
    tm = min(512, tlen)
    tf = 256
    assert mp % tm == 0 and tlen % tm == 0 and f % tf == 0
    nf = f // tf
    blocks = (2 * (_nbytes((tm, d), BF16) + _nbytes((ms, d), hs.dtype) + 2 * _nbytes((d, tf), F32)
                   + _nbytes((tm, tf), BF16) + 3 * _nbytes((ms, tf), F32) + _nbytes((nseq * 8, tf), F32))
              + 2 * _nbytes((d, tf), BF16) + _nbytes((tm + 8, tf), F32) + 4 * _nbytes((tm, tf), F32))
    return pl.pallas_call(
        functools.partial(_ffn_up_body, tm=tm, tlen=tlen, ts=ts),
        grid=(nf, mp // tm),
        in_specs=[pl.BlockSpec((tm, d), lambda j, i: (i, 0)),
                  pl.BlockSpec((ms, d), lambda j, i: (0, 0)),
                  pl.BlockSpec((d, tf), lambda j, i: (0, j)),
                  pl.BlockSpec((d, tf), lambda j, i: (0, j + nf)),
                  pl.BlockSpec((CONV_W, tf), lambda j, i: (0, j)),
                  pl.BlockSpec((1, tf), lambda j, i: (0, j)),
                  pl.BlockSpec((ms, tf), lambda j, i: (0, j))],
        out_specs=[pl.BlockSpec((tm, tf), lambda j, i: (i, j)),
                   pl.BlockSpec((ms, tf), lambda j, i: (0, j)),
                   pl.BlockSpec((nseq * 8, tf), lambda j, i: (0, j)),
                   pl.BlockSpec((ms, tf), lambda j, i: (0, j))],
        out_shape=[jax.ShapeDtypeStruct((mp, f), BF16),
                   jax.ShapeDtypeStruct((ms, f), BF16),
                   jax.ShapeDtypeStruct((nseq * 8, f), F32),
                   jax.ShapeDtypeStruct((ms, f), F32)],
        scratch_shapes=[pltpu.VMEM((d, tf), BF16), pltpu.VMEM((d, tf), BF16),
                        pltpu.VMEM((tm + 8, tf), F32)],
        compiler_params=pltpu.CompilerParams(
            dimension_semantics=("arbitrary", "arbitrary"),
            vmem_limit_bytes=_vmem_limit(blocks)),
        name="ffn_up",
    )(hp, hs, w_up, w_up, conv_w, conv_b.reshape(1, f), ext_s)


def _level_table(c):
    t = np.arange(c)[:, None]
    s = np.arange(c)[None, :]
    x = np.maximum(t ^ s, 1)
    lev = np.floor(np.log2(x)).astype(np.int32)
    return np.where(t > s, lev, np.where(t == s, -1, -2)).astype(np.int32)


def _cumsum_table(c):
    return np.tril(np.ones((c, c), np.float32))


def _split3(x):
    x0 = x.astype(BF16)
    r1 = x - x0.astype(F32)
    x1 = r1.astype(BF16)
    x2 = (r1 - x1.astype(F32)).astype(BF16)
    return x0, x1, x2


def _block_boundary_rows(b, blk):
    c, w = b.shape
    half = blk // 2
    if blk >= 2 * V7X_SUBLANES:
        x = b.reshape(c // blk, blk, w)
        r = jnp.broadcast_to(x[:, half - 1:half, :], x.shape)
        return r.reshape(c, w)
    x = b.reshape(c // V7X_SUBLANES, V7X_SUBLANES, w)
    sub = lax.broadcasted_iota(jnp.int32, x.shape, 1)

    def row(i):
        return jnp.broadcast_to(x[:, i:i + 1, :], x.shape)

    if blk == 8:
        r = row(3)
    elif blk == 4:
        r = jnp.where(sub < 4, row(1), row(5))
    else:
        r = jnp.where(sub < 2, row(0), jnp.where(sub < 4, row(2), jnp.where(sub < 6, row(4), row(6))))
    return r.reshape(c, w)


_NT = (((1,), (1,)), ((), ()))
_TN = (((0,), (0,)), ((), ()))


def _gla_body(*refs, nseq, tlen, valid, has_s0):
    if has_s0:
        (q_ref, f_ref, v_ref, g_ref, lb_ref, gain_ref, lev_ref, tri_ref, s0_ref,
         o_ref, so_ref) = refs
    else:
        q_ref, f_ref, v_ref, g_ref, lb_ref, gain_ref, lev_ref, tri_ref, o_ref, so_ref = refs
    c = GLA_CHUNK
    rows_in = min(c, valid)
    nchunk = max(tlen // c, 1)
    lb = lb_ref[...]
    log_lb = jnp.log(lb)
    log1m_lb = jnp.log1p(-lb)
    one_m_lb = 1.0 - lb
    gain = gain_ref[...]
    row_id = lax.broadcasted_iota(jnp.int32, (c, HEAD), 0)

    def padded(ref, rows):
        x = ref[rows, :]
        if rows_in < c:
            x = jnp.concatenate([x, jnp.zeros((c - rows_in, HEAD), F32)], axis=0)
        return x

    def chunk(n, ci, st):
        row0 = pl.multiple_of(n * tlen + ci * rows_in, rows_in)
        rows = pl.ds(row0, rows_in)
        qr = padded(q_ref, rows)
        fz = padded(f_ref, rows)
        v = padded(v_ref, rows)
        log_sig = jnp.minimum(fz, 0.0) - jnp.log1p(jnp.exp(-jnp.abs(fz)))
        u = log1m_lb + log_sig
        log_f = jnp.maximum(log_lb, u) + jnp.log1p(jnp.exp(-jnp.abs(log_lb - u)))
        k = one_m_lb * jax.nn.sigmoid(-fz)
        if rows_in < c:
            log_f = jnp.where(row_id < rows_in, log_f, 0.0)
            k = jnp.where(row_id < rows_in, k, 0.0)
        q = jax.nn.silu(qr) * (HEAD ** -0.5)
        vb = v.astype(BF16)

        tri = tri_ref[...]
        f0, f1, f2 = _split3(log_f)
        b = (jnp.dot(tri, f0, preferred_element_type=F32)
             + jnp.dot(tri, f1, preferred_element_type=F32)
             + jnp.dot(tri, f2, preferred_element_type=F32))
        b_last = b[c - 1:c, :]

        lev = lev_ref[...]
        diag = jnp.sum(q * k, axis=-1, keepdims=True)
        att = jnp.where(lev == -1, diag, 0.0)
        for l in range(int(math.log2(c))):
            r = _block_boundary_rows(b, 2 << l)
            delta = b - r
            qt = (q * jnp.exp(jnp.minimum(delta, 0.0))).astype(BF16)
            kt = (k * jnp.exp(jnp.minimum(-delta, 0.0))).astype(BF16)
            s_l = lax.dot_general(qt, kt, _NT, preferred_element_type=F32)
            att = jnp.where(lev == l, s_l, att)
        o = jnp.dot(att.astype(BF16), vb, preferred_element_type=F32)
        o = o + lax.dot_general((q * jnp.exp(b)).astype(BF16), st.astype(BF16), _NT,
                                preferred_element_type=F32)
        kd = (k * jnp.exp(b_last - b)).astype(BF16)
        st_new = st * jnp.exp(b_last) + lax.dot_general(vb, kd, _TN, preferred_element_type=F32)

        o = o * lax.rsqrt(jnp.mean(o * o, axis=-1, keepdims=True) + EPS) * gain
        g = padded(g_ref, rows)
        o = o * jax.nn.silu(g)
        o_ref[rows, :] = o[0:rows_in, :].astype(o_ref.dtype)
        return st_new

    def sequence(n, carry):
        if has_s0:
            st0 = s0_ref[n, 0].T
        else:
            st0 = jnp.zeros((HEAD, HEAD), F32)
        st = lax.fori_loop(0, nchunk, lambda ci, st: chunk(n, ci, st), st0)
        so_ref[n, 0] = st.T
        return carry

    lax.fori_loop(0, nseq, sequence, 0)


def _gla(proj, lb, gain, s0, *, nseq_total, tlen, out_dtype):
    m, d4 = proj.shape
    d = d4 // 4
    nh = d // HEAD
    nseq = 1 if tlen >= GLA_CHUNK else nseq_total
    assert tlen % GLA_CHUNK == 0 or tlen < GLA_CHUNK
    rows = nseq * tlen
    col = lambda off: pl.BlockSpec((rows, HEAD), lambda n, h: (n, h + off))
    vec = pl.BlockSpec((1, HEAD), lambda n, h: (0, h))
    const = pl.BlockSpec((GLA_CHUNK, GLA_CHUNK), lambda n, h: (0, 0))
    in_specs = [col(0), col(nh), col(2 * nh), col(3 * nh), vec, vec, const, const]
    args = [proj, proj, proj, proj, lb.reshape(1, d), gain.reshape(1, d),
            jnp.asarray(_level_table(GLA_CHUNK)), jnp.asarray(_cumsum_table(GLA_CHUNK), BF16)]
    state_spec = pl.BlockSpec((nseq, 1, HEAD, HEAD), lambda n, h: (n, h, 0, 0))
    if s0 is not None:
        in_specs.append(state_spec)
        args.append(s0)
    blocks = 2 * (4 * _nbytes((rows, HEAD), F32) + _nbytes((rows, HEAD), out_dtype)
                  + 2 * _nbytes((nseq, 1, HEAD, HEAD), F32)) + (4 << 20)
    return pl.pallas_call(
        functools.partial(_gla_body, nseq=nseq, tlen=tlen, valid=tlen, has_s0=s0 is not None),
        grid=(nseq_total // nseq, nh),
        in_specs=in_specs,
        out_specs=[pl.BlockSpec((rows, HEAD), lambda n, h: (n, h)), state_spec],
        out_shape=[jax.ShapeDtypeStruct((m, d), out_dtype),
                   jax.ShapeDtypeStruct((nseq_total, nh, HEAD, HEAD), F32)],
        compiler_params=pltpu.CompilerParams(
            dimension_semantics=("arbitrary", "arbitrary"),
            vmem_limit_bytes=_vmem_limit(blocks)),
        name="hgrn2_recurrence",
    )(*args)


def _t5_buckets(dist):
    max_exact = N_BUCKETS // 2
    n = np.asarray(dist, dtype=np.int64)
    large = max_exact + (np.log(np.maximum(n, 1) / max_exact)
                         / np.log(BUCKET_MAX_DIST / max_exact)
                         * (N_BUCKETS - max_exact)).astype(np.int64)
    large = np.minimum(large, N_BUCKETS - 1)
    return np.where(n < max_exact, n, large).astype(np.int32)


def _prompt_bias_tiles(rel_bias, nh):
    bl = SUB_WINDOW
    off = np.arange(bl)[:, None] + bl - np.arange(2 * bl)[None, :]
    band = (off >= 0) & (off <= SUB_WINDOW)
    tiles = []
    for g, dil in enumerate(DILATIONS):
        buckets = _t5_buckets(np.clip(off, 0, SUB_WINDOW) * dil)
        tab = rel_bias[:, g * nh:(g + 1) * nh].astype(F32)
        t = jnp.transpose(tab[buckets.reshape(-1)], (1, 0)).reshape(nh, bl, 2 * bl)
        tiles.append(jnp.where(band[None], t, NEG))
    return jnp.stack(tiles)


def _attn_prompt_body(q1_ref, q2_ref, q3_ref, k_ref, v_ref, bias_ref, o_ref,
                      kpad_ref, vpad_ref, acc_ref, mx_ref, sm_ref, *, tlen):
    bl = SUB_WINDOW
    pad = bl * max(DILATIONS)
    scale = HEAD ** -0.5
    kpad_ref[0:pad, :] = jnp.zeros((pad, HEAD), F32)
    vpad_ref[0:pad, :] = jnp.zeros((pad, HEAD), F32)
    kpad_ref[pad:pad + tlen, :] = k_ref[...]
    vpad_ref[pad:pad + tlen, :] = v_ref[...]
    key_is_prev = lax.broadcasted_iota(jnp.int32, (bl, 2 * bl), 1) < bl

    for g, (dil, q_ref) in enumerate(zip(DILATIONS, (q1_ref, q2_ref, q3_ref))):
        nblk = tlen // (dil * bl)
        bias = bias_ref[g, 0]

        def block(it, carry, dil=dil, q_ref=q_ref, nblk=nblk, bias=bias, g=g):
            r = it // nblk
            bi = it % nblk
            start = r + dil * bl * bi
            qrows = pl.ds(start, bl, stride=dil) if dil > 1 else pl.ds(start, bl)
            kstart = pad + start - dil * bl
            krows = pl.ds(kstart, 2 * bl, stride=dil) if dil > 1 else pl.ds(kstart, 2 * bl)
            qb = q_ref[qrows, :].astype(BF16)
            k2 = kpad_ref[krows, :].astype(BF16)
            v2 = vpad_ref[krows, :].astype(BF16)
            s = lax.dot_general(qb, k2, _NT, preferred_element_type=F32) * scale + bias
            s = jnp.where(jnp.logical_and(bi == 0, key_is_prev), NEG, s)
            mx = jnp.max(s, axis=-1, keepdims=True)
            p = jnp.exp(s - mx)
            sm = jnp.sum(p, axis=-1, keepdims=True)
            acc = jnp.dot(p.astype(BF16), v2, preferred_element_type=F32)
            acc_ref[g, qrows, :] = acc
            mx_ref[g, qrows, :] = jnp.broadcast_to(mx, (bl, HEAD))
            sm_ref[g, qrows, :] = jnp.broadcast_to(sm, (bl, HEAD))
            return carry

        lax.fori_loop(0, dil * nblk, block, 0)

    def merge(ci, carry):
        rows = pl.ds(pl.multiple_of(ci * bl, bl), bl)
        m1, m2, m3 = mx_ref[0, rows, :], mx_ref[1, rows, :], mx_ref[2, rows, :]
        mm = jnp.maximum(jnp.maximum(m1, m2), m3)
        w1, w2, w3 = jnp.exp(m1 - mm), jnp.exp(m2 - mm), jnp.exp(m3 - mm)
        num = w1 * acc_ref[0, rows, :] + w2 * acc_ref[1, rows, :] + w3 * acc_ref[2, rows, :]
        den = w1 * sm_ref[0, rows, :] + w2 * sm_ref[1, rows, :] + w3 * sm_ref[2, rows, :]
        o_ref[rows, :] = (num / den).astype(o_ref.dtype)
        return carry

    lax.fori_loop(0, tlen // bl, merge, 0)


def _attn_prompt(qproj, k, v, bias_tiles, *, nseq, tlen):
    m, d3 = qproj.shape
    d = d3 // 3
    nh = d // HEAD
    qspec = lambda g: pl.BlockSpec((tlen, HEAD), lambda n, h: (n, g * nh + h))
    kvspec = pl.BlockSpec((tlen, HEAD), lambda n, h: (n, h))
    pad = SUB_WINDOW * max(DILATIONS)
    blocks = (2 * (5 * _nbytes((tlen, HEAD), F32) + _nbytes((3, 1, SUB_WINDOW, 2 * SUB_WINDOW), F32)
                   + _nbytes((tlen, HEAD), BF16))
              + 2 * _nbytes((pad + tlen, HEAD), F32) + 9 * _nbytes((tlen, HEAD), F32))
    return pl.pallas_call(
        functools.partial(_attn_prompt_body, tlen=tlen),
        grid=(nseq, nh),
        in_specs=[qspec(0), qspec(1), qspec(2), kvspec, kvspec,
                  pl.BlockSpec((3, 1, SUB_WINDOW, 2 * SUB_WINDOW), lambda n, h: (0, h, 0, 0))],
        out_specs=pl.BlockSpec((tlen, HEAD), lambda n, h: (n, h)),
        out_shape=jax.ShapeDtypeStruct((m, d), BF16),
        scratch_shapes=[pltpu.VMEM((pad + tlen, HEAD), F32), pltpu.VMEM((pad + tlen, HEAD), F32),
                        pltpu.VMEM((3, tlen, HEAD), F32), pltpu.VMEM((3, tlen, HEAD), F32),
                        pltpu.VMEM((3, tlen, HEAD), F32)],
        compiler_params=pltpu.CompilerParams(
            dimension_semantics=("arbitrary", "arbitrary"),
            vmem_limit_bytes=_vmem_limit(blocks)),
        name="dilated_attention_prompt",
    )(qproj, qproj, qproj, k, v, bias_tiles)


def _sample_bias_table(rel_bias, nh, p_len, ts, nkeys):
    t = np.arange(ts)[:, None]
    row = np.arange(nkeys)[None, :]
    dist = p_len + t - row
    out = []
    for g, dil in enumerate(DILATIONS):
        j = dist // dil
        ok = (dist >= 0) & (dist % dil == 0) & (j <= SUB_WINDOW) & (row < p_len + ts)
        buckets = _t5_buckets(np.clip(j, 0, SUB_WINDOW) * dil)
        tab = rel_bias[:, g * nh:(g + 1) * nh].astype(F32)
        b = jnp.transpose(tab[buckets.reshape(-1)], (1, 0)).reshape(nh, ts, nkeys)
        out.append(jnp.where(ok[None], b, NEG))
    return jnp.concatenate(out, axis=1)


def _attn_sample_body(q1_ref, q2_ref, q3_ref, kc_ref, vc_ref, kn_ref, vn_ref, bias_ref, o_ref,
                      kall_ref, vall_ref, *, p_len, ts):
    nkeys = kall_ref.shape[0]
    scale = HEAD ** -0.5
    kall_ref[0:p_len, :] = kc_ref[0]
    vall_ref[0:p_len, :] = vc_ref[0]
    kall_ref[p_len:p_len + ts, :] = kn_ref[...]
    vall_ref[p_len:p_len + ts, :] = vn_ref[...]
    tail = nkeys - p_len - ts
    kall_ref[p_len + ts:nkeys, :] = jnp.zeros((tail, HEAD), F32)
    vall_ref[p_len + ts:nkeys, :] = jnp.zeros((tail, HEAD), F32)
    kb = kall_ref[...].astype(BF16)
    vb = vall_ref[...].astype(BF16)
    bias = bias_ref[0]
    s = []
    for g, q_ref in enumerate((q1_ref, q2_ref, q3_ref)):
        sg = lax.dot_general(q_ref[...].astype(BF16), kb, _NT, preferred_element_type=F32)
        s.append(sg * scale + bias[g * ts:(g + 1) * ts, :])
    mx = [jnp.max(x, axis=-1, keepdims=True) for x in s]
    mm = jnp.maximum(jnp.maximum(mx[0], mx[1]), mx[2])
    p = jnp.exp(s[0] - mm) + jnp.exp(s[1] - mm) + jnp.exp(s[2] - mm)
    den = jnp.sum(p, axis=-1, keepdims=True)
    o = jnp.dot(p.astype(BF16), vb, preferred_element_type=F32)
    o_ref[...] = (o / den).astype(o_ref.dtype)


def _attn_sample(qproj, k_new, v_new, cache_k, cache_v, bias_table, *, nseq, ts):
    m, d3 = qproj.shape
    d = d3 // 3
    nh = d // HEAD
    p_len = cache_k.shape[1]
    nkeys = bias_table.shape[-1]
    qspec = lambda g: pl.BlockSpec((ts, HEAD), lambda n, h: (n, g * nh + h))
    newspec = pl.BlockSpec((ts, HEAD), lambda n, h: (n, h))
    cspec = pl.BlockSpec((1, p_len, HEAD), lambda n, h: (n, 0, h))
    blocks = (2 * (2 * _nbytes((p_len, HEAD), F32) + _nbytes((3 * ts, nkeys), F32))
              + 2 * _nbytes((nkeys, HEAD), F32) + 2 * _nbytes((nkeys, HEAD), BF16)
              + 8 * _nbytes((3 * ts, nkeys), F32))
    return pl.pallas_call(
        functools.partial(_attn_sample_body, p_len=p_len, ts=ts),
        grid=(nseq, nh),
        in_specs=[qspec(0), qspec(1), qspec(2), cspec, cspec, newspec, newspec,
                  pl.BlockSpec((1, 3 * ts, nkeys), lambda n, h: (h, 0, 0))],
        out_specs=pl.BlockSpec((ts, HEAD), lambda n, h: (n, h)),
        out_shape=jax.ShapeDtypeStruct((m, d), F32),
        scratch_shapes=[pltpu.VMEM((nkeys, HEAD), F32), pltpu.VMEM((nkeys, HEAD), F32)],
        compiler_params=pltpu.CompilerParams(
            dimension_semantics=("arbitrary", "arbitrary"),
            vmem_limit_bytes=_vmem_limit(blocks)),
        name="dilated_attention_sample",
    )(qproj, qproj, qproj, cache_k.reshape(nseq, p_len, d), cache_v.reshape(nseq, p_len, d),
      k_new, v_new, bias_table)


def _lower_bounds(lb_raw):
    p = jax.nn.softmax(lb_raw.astype(F32), axis=0)
    c = jnp.cumsum(p, axis=0)
    return c - c[0:1]


def kernel(x_prompt, x_sample, state_hgrn, state_ffn_conv, cache_k_win, cache_v_win, a_w_in, a_lb_raw, a_out_gain, a_w_out, kv_norm, w_kv, b_w_q, b_w_out, rel_bias, ffn_w_up, ffn_conv_w, ffn_conv_b, ffn_w_down, norm_mix, norm_ffn, norm_final):
    nb_p, t_p, d = x_prompt.shape
    nb_s, t_s, _ = x_sample.shape
    depth = ffn_w_up.shape[0]
    n_a = a_w_in.shape[0]
    nh = d // HEAD
    f = ffn_w_down.shape[1]
    p_len = cache_k_win.shape[1]
    assert t_p % (SUB_WINDOW * max(DILATIONS)) == 0 and t_s == V7X_SUBLANES

    xp = x_prompt.reshape(nb_p * t_p, d)
    xs = x_sample.reshape(nb_s * t_s, d)
    lbs = _lower_bounds(a_lb_raw)
    nkeys = -(-(p_len + t_s) // V7X_LANES) * V7X_LANES
    bias_p = _prompt_bias_tiles(rel_bias, nh)
    bias_s = _sample_bias_table(rel_bias, nh, p_len, t_s, nkeys)

    hgrn_p, hgrn_s, conv_p, conv_s = [], [], [], []
    kp = vp = ks = vs = None
    for l in range(depth):
        hp = _rmsnorm(xp, norm_mix[l], BF16)
        hs = _rmsnorm(xs, norm_mix[l], BF16)
        if l < n_a:
            pp, ps = _linear(hp, hs, a_w_in[l], name="hgrn2_in_proj")
            op, sp = _gla(pp, lbs[l], a_out_gain[l], None, nseq_total=nb_p, tlen=t_p, out_dtype=BF16)
            os_, ss = _gla(ps, lbs[l], a_out_gain[l], state_hgrn[l], nseq_total=nb_s, tlen=t_s,
                           out_dtype=F32)
            hgrn_p.append(sp)
            hgrn_s.append(ss)
            w_o = a_w_out[l]
        else:
            if l == n_a:
                kvp = _rmsnorm(xp, kv_norm, BF16)
                kvs = _rmsnorm(xs, kv_norm, BF16)
                kp, ks = _linear(kvp, kvs, w_kv, col0=0, n=d, name="k_proj")
                vp, vs = _linear(kvp, kvs, w_kv, col0=d, n=d, name="v_proj")
            jb = l - n_a
            qp, qs = _linear(hp, hs, b_w_q[jb], name="q_proj")
            op = _attn_prompt(qp, kp, vp, bias_p, nseq=nb_p, tlen=t_p)
            os_ = _attn_sample(qs, ks, vs, cache_k_win, cache_v_win, bias_s, nseq=nb_s, ts=t_s)
            w_o = b_w_out[jb]
        xp, xs = _linear(op, os_, w_o, res=(xp, xs), name="mixer_out_proj")

        hp = _rmsnorm(xp, norm_ffn[l], BF16)
        hs = _rmsnorm(xs, norm_ffn[l], BF16)
        prev = state_ffn_conv[l]
        ext_s = jnp.pad(prev, ((0, 0), (0, t_s - (CONV_W - 1)), (0, 0))).reshape(nb_s * t_s, f)
        actp, acts, tailp, a_s = _ffn_up(hp, hs, ffn_w_up[l], ffn_conv_w[l], ffn_conv_b[l], ext_s,
                                         tlen=t_p, ts=t_s)
        conv_p.append(tailp.reshape(nb_p, 8, f)[:, 8 - (CONV_W - 1):, :])
        conv_s.append(a_s.reshape(nb_s, t_s, f)[:, t_s - (CONV_W - 1):, :])
        xp, xs = _linear(actp, acts, ffn_w_down[l], res=(xp, xs), name="ffn_down_proj")

    yp = _rmsnorm(xp, norm_final, F32).reshape(nb_p, t_p, d)
    ys = _rmsnorm(xs, norm_final, F32).reshape(nb_s, t_s, d)
    keep = min(BUCKET_MAX_DIST, t_p)
    k_win = kp.reshape(nb_p, t_p, nh, HEAD)[:, t_p - keep:]
    v_win = vp.reshape(nb_p, t_p, nh, HEAD)[:, t_p - keep:]
    return (yp, ys, jnp.stack(hgrn_p), jnp.stack(hgrn_s), jnp.stack(conv_p), jnp.stack(conv_s),
            k_win, v_win, ks.reshape(nb_s, t_s, nh, HEAD), vs.reshape(nb_s, t_s, nh, HEAD))
```

```python
import functools
import math

import numpy as np
import jax
import jax.numpy as jnp
from jax import lax
from jax.experimental import pallas as pl
from jax.experimental.pallas import tpu as pltpu

F32 = jnp.float32
BF16 = jnp.bfloat16

V7X_VMEM_BYTES = 64 * 1024 * 1024
V7X_LANES = 128
V7X_SUBLANES = 8
COMPILER_SCRATCH_BYTES = 12 * 1024 * 1024

HEAD = 128
GLA_CHUNK = 128
GLA_CHUNK_UNROLL = 4
SUB_WINDOW = 128
DILATIONS = (1, 4, 16)
N_BUCKETS = 32
BUCKET_MAX_DIST = 2048
CONV_W = 3
EPS = 1e-6
NEG = -1e30


def _vmem_limit(block_bytes):
    return int(min(block_bytes + COMPILER_SCRATCH_BYTES, V7X_VMEM_BYTES - (2 << 20)))


def _nbytes(shape, dtype):
    return int(np.prod(shape)) * jnp.dtype(dtype).itemsize


def _rmsnorm_body(x_ref, g_ref, o_ref):
    x = x_ref[...]
    ms = jnp.mean(x * x, axis=-1, keepdims=True)
    o_ref[...] = (x * lax.rsqrt(ms + EPS) * g_ref[...]).astype(o_ref.dtype)


def _rmsnorm(x, g, out_dtype):
    m, d = x.shape
    tm = min(m, 256)
    assert m % tm == 0
    blocks = 2 * (_nbytes((tm, d), F32) + _nbytes((tm, d), out_dtype))
    return pl.pallas_call(
        _rmsnorm_body,
        grid=(m // tm,),
        in_specs=[pl.BlockSpec((tm, d), lambda i: (i, 0)),
                  pl.BlockSpec((1, d), lambda i: (0, 0))],
        out_specs=pl.BlockSpec((tm, d), lambda i: (i, 0)),
        out_shape=jax.ShapeDtypeStruct((m, d), out_dtype),
        compiler_params=pltpu.CompilerParams(vmem_limit_bytes=_vmem_limit(blocks)),
        name="rmsnorm",
    )(x, g.reshape(1, d))


W_CAST_ROWS = 256


def _cast_weight(w_ref, wb_ref):
    k = w_ref.shape[0]
    assert k % W_CAST_ROWS == 0

    def step(c, carry):
        rows = pl.ds(pl.multiple_of(c * W_CAST_ROWS, W_CAST_ROWS), W_CAST_ROWS)
        wb_ref[rows, :] = w_ref[rows, :].astype(BF16)
        return carry

    lax.fori_loop(0, k // W_CAST_ROWS, step, 0)


def _linear_body(*refs, has_res):
    if has_res:
        xp_ref, xs_ref, w_ref, rp_ref, rs_ref, yp_ref, ys_ref, wb_ref = refs
    else:
        xp_ref, xs_ref, w_ref, yp_ref, ys_ref, wb_ref = refs

    @pl.when(pl.program_id(1) == 0)
    def _new_column_tile():
        _cast_weight(w_ref, wb_ref)
        ys = jnp.dot(xs_ref[...].astype(BF16), wb_ref[...], preferred_element_type=F32)
        if has_res:
            ys = ys + rs_ref[...]
        ys_ref[...] = ys

    yp = jnp.dot(xp_ref[...], wb_ref[...], preferred_element_type=F32)
    if has_res:
        yp = yp + rp_ref[...]
    yp_ref[...] = yp


def _linear_tiles(mp, k, n):
    if k <= 4096:
        tm, tn, w_buffers = 512, 512, 2
    else:
        tm, tn, w_buffers = 256, 512, 1
    return min(tm, mp), min(tn, n), w_buffers


def _linear(xp, xs, w, layer, *, col0=0, n=None, res=None, name):
    mp, k = xp.shape
    ms = xs.shape[0]
    n = w.shape[2] if n is None else n
    tm, tn, w_buffers = _linear_tiles(mp, k, n)
    assert mp % tm == 0 and n % tn == 0 and col0 % tn == 0
    jb0 = col0 // tn
    w_mode = {} if w_buffers == 2 else {"pipeline_mode": pl.Buffered(w_buffers)}
    in_specs = [pl.BlockSpec((tm, k), lambda j, i: (i, 0)),
                pl.BlockSpec((ms, k), lambda j, i: (0, 0)),
                pl.BlockSpec((None, k, tn), lambda j, i: (layer, 0, j + jb0), **w_mode)]
    args = [xp, xs, w]
    blocks = (2 * (_nbytes((tm, k), BF16) + _nbytes((ms, k), xs.dtype)
                   + _nbytes((tm, tn), F32) + _nbytes((ms, tn), F32))
              + w_buffers * _nbytes((k, tn), F32) + _nbytes((k, tn), BF16))
    if res is not None:
        in_specs += [pl.BlockSpec((tm, tn), lambda j, i: (i, j)),
                     pl.BlockSpec((ms, tn), lambda j, i: (0, j))]
        args += list(res)
        blocks += 2 * (_nbytes((tm, tn), F32) + _nbytes((ms, tn), F32))
    return pl.pallas_call(
        functools.partial(_linear_body, has_res=res is not None),
        grid=(n // tn, mp // tm),
        in_specs=in_specs,
        out_specs=[pl.BlockSpec((tm, tn), lambda j, i: (i, j)),
                   pl.BlockSpec((ms, tn), lambda j, i: (0, j))],
        out_shape=[jax.ShapeDtypeStruct((mp, n), F32), jax.ShapeDtypeStruct((ms, n), F32)],
        scratch_shapes=[pltpu.VMEM((k, tn), BF16)],
        compiler_params=pltpu.CompilerParams(
            dimension_semantics=("arbitrary", "arbitrary"),
            vmem_limit_bytes=_vmem_limit(blocks)),
        name=name,
    )(*args)


def _ffn_up_body(hp_ref, hs_ref, wa_ref, wg_ref, cw_ref, cb_ref, exts_ref,
                 actp_ref, acts_ref, tailp_ref, as_ref,
                 wab_ref, wgb_ref, ext_ref, *, tm, tlen, ts):
    i = pl.program_id(1)
    cw = cw_ref[...]
    cb = cb_ref[...]

    def act_fn(a2, a1, a0, b):
        conv = cb + cw[0:1, :] * a2 + cw[1:2, :] * a1 + cw[2:3, :] * a0
        return jax.nn.silu(conv) * b

    @pl.when(i == 0)
    def _new_column_tile():
        _cast_weight(wa_ref, wab_ref)
        _cast_weight(wg_ref, wgb_ref)
        hs = hs_ref[...].astype(BF16)
        a = jnp.dot(hs, wab_ref[...], preferred_element_type=F32)
        b = jnp.dot(hs, wgb_ref[...], preferred_element_type=F32)
        ms = a.shape[0]
        t = lax.broadcasted_iota(jnp.int32, a.shape, 0) % ts
        ext = exts_ref[...]
        a1 = jnp.where(t == 0, pltpu.roll(ext, ms - 1, axis=0), pltpu.roll(a, 1, axis=0))
        a2 = jnp.where(t < 2, ext, pltpu.roll(a, 2, axis=0))
        acts_ref[...] = act_fn(a2, a1, a, b).astype(acts_ref.dtype)
        as_ref[...] = a

    @pl.when((i * tm) % tlen == 0)
    def _sequence_start():
        ext_ref[0:8, :] = jnp.zeros((8, ext_ref.shape[1]), F32)

    hp = hp_ref[...]
    a = jnp.dot(hp, wab_ref[...], preferred_element_type=F32)
    b = jnp.dot(hp, wgb_ref[...], preferred_element_type=F32)
    ext_ref[8:8 + tm, :] = a
    a1 = ext_ref[7:7 + tm, :]
    a2 = ext_ref[6:6 + tm, :]
    actp_ref[...] = act_fn(a2, a1, a, b).astype(actp_ref.dtype)
    tail = a[tm - 8:tm, :]
    ext_ref[0:8, :] = tail

    @pl.when(((i + 1) * tm) % tlen == 0)
    def _sequence_end():
        n = (i * tm) // tlen
        tailp_ref[pl.ds(pl.multiple_of(n * 8, 8), 8), :] = tail


def _ffn_up(hp, hs, w_up, conv_w, conv_b, ext_s, layer, *, tlen, ts):
    mp, d = hp.shape
    ms = hs.shape[0]
    f = w_up.shape[2] // 2
    nseq = mp // tlen
    tm = min(512, tlen)
    tf = 256
    assert mp % tm == 0 and tlen % tm == 0 and f % tf == 0
    nf = f // tf
    blocks = (2 * (_nbytes((tm, d), BF16) + _nbytes((ms, d), hs.dtype) + 2 * _nbytes((d, tf), F32)
                   + _nbytes((tm, tf), BF16) + 3 * _nbytes((ms, tf), F32) + _nbytes((nseq * 8, tf), F32))
              + 2 * _nbytes((d, tf), BF16) + _nbytes((tm + 8, tf), F32) + 4 * _nbytes((tm, tf), F32))
    return pl.pallas_call(
        functools.partial(_ffn_up_body, tm=tm, tlen=tlen, ts=ts),
        grid=(nf, mp // tm),
        in_specs=[pl.BlockSpec((tm, d), lambda j, i: (i, 0)),
                  pl.BlockSpec((ms, d), lambda j, i: (0, 0)),
                  pl.BlockSpec((None, d, tf), lambda j, i: (layer, 0, j)),
                  pl.BlockSpec((None, d, tf), lambda j, i: (layer, 0, j + nf)),
                  pl.BlockSpec((None, CONV_W, tf), lambda j, i: (layer, 0, j)),
                  pl.BlockSpec((None, 1, tf), lambda j, i: (layer, 0, j)),
                  pl.BlockSpec((ms, tf), lambda j, i: (0, j))],
        out_specs=[pl.BlockSpec((tm, tf), lambda j, i: (i, j)),
                   pl.BlockSpec((ms, tf), lambda j, i: (0, j)),
                   pl.BlockSpec((nseq * 8, tf), lambda j, i: (0, j)),
                   pl.BlockSpec((ms, tf), lambda j, i: (0, j))],
        out_shape=[jax.ShapeDtypeStruct((mp, f), BF16),
                   jax.ShapeDtypeStruct((ms, f), BF16),
                   jax.ShapeDtypeStruct((nseq * 8, f), F32),
                   jax.ShapeDtypeStruct((ms, f), F32)],
        scratch_shapes=[pltpu.VMEM((d, tf), BF16), pltpu.VMEM((d, tf), BF16),
                        pltpu.VMEM((tm + 8, tf), F32)],
        compiler_params=pltpu.CompilerParams(
            dimension_semantics=("arbitrary", "arbitrary"),
            vmem_limit_bytes=_vmem_limit(blocks)),
        name="ffn_up",
    )(hp, hs, w_up, w_up, conv_w, conv_b, ext_s)


def _level_table(c):
    t = np.arange(c)[:, None]
    s = np.arange(c)[None, :]
    x = np.maximum(t ^ s, 1)
    lev = np.floor(np.log2(x)).astype(np.int32)
    return np.where(t > s, lev, np.where(t == s, -1, -2)).astype(np.int32)


def _cumsum_table(c):
    return np.tril(np.ones((c, c), np.float32))


def _split3(x):
    x0 = x.astype(BF16)
    r1 = x - x0.astype(F32)
    x1 = r1.astype(BF16)
    x2 = (r1 - x1.astype(F32)).astype(BF16)
    return x0, x1, x2


def _block_boundary_rows(b, blk):
    c, w = b.shape
    half = blk // 2
    if blk >= 2 * V7X_SUBLANES:
        x = b.reshape(c // blk, blk, w)
        r = jnp.broadcast_to(x[:, half - 1:half, :], x.shape)
        return r.reshape(c, w)
    x = b.reshape(c // V7X_SUBLANES, V7X_SUBLANES, w)
    sub = lax.broadcasted_iota(jnp.int32, x.shape, 1)

    def row(i):
        return jnp.broadcast_to(x[:, i:i + 1, :], x.shape)

    if blk == 8:
        r = row(3)
    elif blk == 4:
        r = jnp.where(sub < 4, row(1), row(5))
    else:
        r = jnp.where(sub < 2, row(0), jnp.where(sub < 4, row(2), jnp.where(sub < 6, row(4), row(6))))
    return r.reshape(c, w)


_NT = (((1,), (1,)), ((), ()))
_TN = (((0,), (0,)), ((), ()))


def _gla_body(*refs, nseq, tlen, valid, has_s0):
    if has_s0:
        (q_ref, f_ref, v_ref, g_ref, lb_ref, gain_ref, lev_ref, tri_ref, s0_ref,
         o_ref, so_ref) = refs
    else:
        q_ref, f_ref, v_ref, g_ref, lb_ref, gain_ref, lev_ref, tri_ref, o_ref, so_ref = refs
    c = GLA_CHUNK
    rows_in = min(c, valid)
    nchunk = max(tlen // c, 1)
    lb = lb_ref[...]
    log_lb = jnp.log(lb)
    log1m_lb = jnp.log1p(-lb)
    one_m_lb = 1.0 - lb
    gain = gain_ref[...]
    row_id = lax.broadcasted_iota(jnp.int32, (c, HEAD), 0)

    def padded(ref, rows):
        x = ref[rows, :]
        if rows_in < c:
            x = jnp.concatenate([x, jnp.zeros((c - rows_in, HEAD), F32)], axis=0)
        return x

    def chunk(n, ci, st):
        row0 = pl.multiple_of(n * tlen + ci * rows_in, rows_in)
        rows = pl.ds(row0, rows_in)
        qr = padded(q_ref, rows)
        fz = padded(f_ref, rows)
        v = padded(v_ref, rows)
        log_sig = jnp.minimum(fz, 0.0) - jnp.log1p(jnp.exp(-jnp.abs(fz)))
        u = log1m_lb + log_sig
        log_f = jnp.maximum(log_lb, u) + jnp.log1p(jnp.exp(-jnp.abs(log_lb - u)))
        k = one_m_lb * jax.nn.sigmoid(-fz)
        if rows_in < c:
            log_f = jnp.where(row_id < rows_in, log_f, 0.0)
            k = jnp.where(row_id < rows_in, k, 0.0)
        q = jax.nn.silu(qr) * (HEAD ** -0.5)
        vb = v.astype(BF16)

        tri = tri_ref[...]
        f0, f1, f2 = _split3(log_f)
        b = (jnp.dot(tri, f0, preferred_element_type=F32)
             + jnp.dot(tri, f1, preferred_element_type=F32)
             + jnp.dot(tri, f2, preferred_element_type=F32))
        b_last = b[c - 1:c, :]

        lev = lev_ref[...]
        diag = jnp.sum(q * k, axis=-1, keepdims=True)
        att = jnp.where(lev == -1, diag, 0.0)
        for l in range(int(math.log2(c))):
            r = _block_boundary_rows(b, 2 << l)
            delta = b - r
            qt = (q * jnp.exp(jnp.minimum(delta, 0.0))).astype(BF16)
            kt = (k * jnp.exp(jnp.minimum(-delta, 0.0))).astype(BF16)
            s_l = lax.dot_general(qt, kt, _NT, preferred_element_type=F32)
            att = jnp.where(lev == l, s_l, att)
        o = jnp.dot(att.astype(BF16), vb, preferred_element_type=F32)
        o = o + lax.dot_general((q * jnp.exp(b)).astype(BF16), st.astype(BF16), _NT,
                                preferred_element_type=F32)
        kd = (k * jnp.exp(b_last - b)).astype(BF16)
        st_new = st * jnp.exp(b_last) + lax.dot_general(vb, kd, _TN, preferred_element_type=F32)

        o = o * lax.rsqrt(jnp.mean(o * o, axis=-1, keepdims=True) + EPS) * gain
        g = padded(g_ref, rows)
        o = o * jax.nn.silu(g)
        o_ref[rows, :] = o[0:rows_in, :].astype(o_ref.dtype)
        return st_new

    def sequence(n, carry):
        if has_s0:
            st0 = s0_ref[n, 0].T
        else:
            st0 = jnp.zeros((HEAD, HEAD), F32)
        st = lax.fori_loop(0, nchunk, lambda ci, st: chunk(n, ci, st), st0,
                           unroll=min(GLA_CHUNK_UNROLL, nchunk))
        so_ref[n, 0] = st.T
        return carry

    lax.fori_loop(0, nseq, sequence, 0)


def _gla(proj, lb, gain, s0, *, nseq_total, tlen, out_dtype):
    m, d4 = proj.shape
    d = d4 // 4
    nh = d // HEAD
    nseq = 1 if tlen >= GLA_CHUNK else nseq_total
    assert tlen % GLA_CHUNK == 0 or tlen < GLA_CHUNK
    rows = nseq * tlen
    col = lambda off: pl.BlockSpec((rows, HEAD), lambda n, h: (n, h + off))
    vec = pl.BlockSpec((1, HEAD), lambda n, h: (0, h))
    const = pl.BlockSpec((GLA_CHUNK, GLA_CHUNK), lambda n, h: (0, 0))
    in_specs = [col(0), col(nh), col(2 * nh), col(3 * nh), vec, vec, const, const]
    args = [proj, proj, proj, proj, lb.reshape(1, d), gain.reshape(1, d),
            jnp.asarray(_level_table(GLA_CHUNK)), jnp.asarray(_cumsum_table(GLA_CHUNK), BF16)]
    state_spec = pl.BlockSpec((nseq, 1, HEAD, HEAD), lambda n, h: (n, h, 0, 0))
    if s0 is not None:
        in_specs.append(state_spec)
        args.append(s0)
    blocks = 2 * (4 * _nbytes((rows, HEAD), F32) + _nbytes((rows, HEAD), out_dtype)
                  + 2 * _nbytes((nseq, 1, HEAD, HEAD), F32)) + (4 << 20)
    return pl.pallas_call(
        functools.partial(_gla_body, nseq=nseq, tlen=tlen, valid=tlen, has_s0=s0 is not None),
        grid=(nseq_total // nseq, nh),
        in_specs=in_specs,
        out_specs=[pl.BlockSpec((rows, HEAD), lambda n, h: (n, h)), state_spec],
        out_shape=[jax.ShapeDtypeStruct((m, d), out_dtype),
                   jax.ShapeDtypeStruct((nseq_total, nh, HEAD, HEAD), F32)],
        compiler_params=pltpu.CompilerParams(
            dimension_semantics=("arbitrary", "arbitrary"),
            vmem_limit_bytes=_vmem_limit(blocks)),
        name="hgrn2_recurrence",
    )(*args)


def _t5_buckets(dist):
    max_exact = N_BUCKETS // 2
    n = np.asarray(dist, dtype=np.int64)
    large = max_exact + (np.log(np.maximum(n, 1) / max_exact)
                         / np.log(BUCKET_MAX_DIST / max_exact)
                         * (N_BUCKETS - max_exact)).astype(np.int64)
    large = np.minimum(large, N_BUCKETS - 1)
    return np.where(n < max_exact, n, large).astype(np.int32)


def _bucket_lookup(tab, buckets):
    flat = jnp.asarray(buckets.reshape(1, -1))
    onehot = (flat == lax.broadcasted_iota(jnp.int32, (N_BUCKETS, flat.shape[1]), 0)).astype(F32)
    out = jnp.einsum("bh,bx->hx", tab, onehot, precision=lax.Precision.HIGHEST)
    return out.reshape((tab.shape[1],) + buckets.shape)


def _prompt_bias_tiles(rel_bias, nh):
    bl = SUB_WINDOW
    off = np.arange(bl)[:, None] + bl - np.arange(2 * bl)[None, :]
    band = (off >= 0) & (off <= SUB_WINDOW)
    tiles = []
    for g, dil in enumerate(DILATIONS):
        buckets = _t5_buckets(np.clip(off, 0, SUB_WINDOW) * dil)
        tab = rel_bias[:, g * nh:(g + 1) * nh].astype(F32)
        tiles.append(jnp.where(band[None], _bucket_lookup(tab, buckets), NEG))
    return jnp.stack(tiles)


ATTN_BLOCK_UNROLL = 8


def _attn_prompt_body(q1_ref, q2_ref, q3_ref, k_ref, v_ref, bias_ref, o_ref,
                      acc_ref, lse_ref, *, tlen):
    bl = SUB_WINDOW
    scale = HEAD ** -0.5

    for g, (dil, q_ref) in enumerate(zip(DILATIONS, (q1_ref, q2_ref, q3_ref))):
        nblk = tlen // (dil * bl)
        with_prev = nblk > 1

        def block(it, carry, dil=dil, q_ref=q_ref, nblk=nblk, g=g, with_prev=with_prev):
            r = it // nblk
            bi = it % nblk
            start = r + dil * bl * bi
            qrows = pl.ds(start, bl, stride=dil) if dil > 1 else pl.ds(start, bl)
            if with_prev:
                kstart = start - jnp.where(bi > 0, dil * bl, 0)
                nkey = 2 * bl
                bias = bias_ref[g, 0]
                bias_first = jnp.concatenate([bias[:, bl:2 * bl], jnp.full((bl, bl), NEG, F32)], axis=1)
                bias = jnp.where(bi == 0, bias_first, bias)
            else:
                kstart = start
                nkey = bl
                bias = bias_ref[g, 0, :, bl:2 * bl]
            krows = pl.ds(kstart, nkey, stride=dil) if dil > 1 else pl.ds(kstart, nkey)
            qb = q_ref[qrows, :].astype(BF16)
            k2 = k_ref[krows, :].astype(BF16)
            v2 = v_ref[krows, :].astype(BF16)
            s = lax.dot_general(qb, k2, _NT, preferred_element_type=F32) * scale + bias
            mx = jnp.max(s, axis=-1, keepdims=True)
            p = jnp.exp(s - mx)
            sm = jnp.sum(p, axis=-1, keepdims=True)
            acc = jnp.dot(p.astype(BF16), v2, preferred_element_type=F32)
            acc_ref[g, qrows, :] = acc / sm
            lse_ref[g, qrows, :] = jnp.broadcast_to(mx + jnp.log(sm), (bl, HEAD))
            return carry

        if with_prev:
            lax.fori_loop(0, dil * nblk, block, 0, unroll=ATTN_BLOCK_UNROLL)
        else:
            for it in range(dil * nblk):
                block(it, 0)

    def merge(ci, carry):
        rows = pl.ds(pl.multiple_of(ci * bl, bl), bl)
        l1, l2, l3 = lse_ref[0, rows, :], lse_ref[1, rows, :], lse_ref[2, rows, :]
        mm = jnp.maximum(jnp.maximum(l1, l2), l3)
        w1, w2, w3 = jnp.exp(l1 - mm), jnp.exp(l2 - mm), jnp.exp(l3 - mm)
        num = w1 * acc_ref[0, rows, :] + w2 * acc_ref[1, rows, :] + w3 * acc_ref[2, rows, :]
        o_ref[rows, :] = (num / (w1 + w2 + w3)).astype(o_ref.dtype)
        return carry

    lax.fori_loop(0, tlen // bl, merge, 0, unroll=2)


def _attn_prompt(qproj, k, v, bias_tiles, *, nseq, tlen):
    m, d3 = qproj.shape
    d = d3 // 3
    nh = d // HEAD
    qspec = lambda g: pl.BlockSpec((tlen, HEAD), lambda n, h: (n, g * nh + h))
    kvspec = pl.BlockSpec((tlen, HEAD), lambda n, h: (n, h))
    blocks = (2 * (5 * _nbytes((tlen, HEAD), F32) + _nbytes((3, 1, SUB_WINDOW, 2 * SUB_WINDOW), F32)
                   + _nbytes((tlen, HEAD), BF16))
              + 6 * _nbytes((tlen, HEAD), F32))
    return pl.pallas_call(
        functools.partial(_attn_prompt_body, tlen=tlen),
        grid=(nseq, nh),
        in_specs=[qspec(0), qspec(1), qspec(2), kvspec, kvspec,
                  pl.BlockSpec((3, 1, SUB_WINDOW, 2 * SUB_WINDOW), lambda n, h: (0, h, 0, 0))],
        out_specs=pl.BlockSpec((tlen, HEAD), lambda n, h: (n, h)),
        out_shape=jax.ShapeDtypeStruct((m, d), BF16),
        scratch_shapes=[pltpu.VMEM((3, tlen, HEAD), F32), pltpu.VMEM((3, tlen, HEAD), F32)],
        compiler_params=pltpu.CompilerParams(
            dimension_semantics=("arbitrary", "arbitrary"),
            vmem_limit_bytes=_vmem_limit(blocks)),
        name="dilated_attention_prompt",
    )(qproj, qproj, qproj, k, v, bias_tiles)


def _sample_bias_table(rel_bias, nh, p_len, ts, nkeys):
    t = np.arange(ts)[:, None]
    row = np.arange(nkeys)[None, :]
    dist = p_len + t - row
    out = []
    for g, dil in enumerate(DILATIONS):
        j = dist // dil
        ok = (dist >= 0) & (dist % dil == 0) & (j <= SUB_WINDOW) & (row < p_len + ts)
        buckets = _t5_buckets(np.clip(j, 0, SUB_WINDOW) * dil)
        tab = rel_bias[:, g * nh:(g + 1) * nh].astype(F32)
        out.append(jnp.where(ok[None], _bucket_lookup(tab, buckets), NEG))
    return jnp.concatenate(out, axis=1)


def _attn_sample_body(q1_ref, q2_ref, q3_ref, kc_ref, vc_ref, kn_ref, vn_ref, bias_ref, o_ref,
                      kall_ref, vall_ref, *, p_len, ts):
    nkeys = kall_ref.shape[0]
    scale = HEAD ** -0.5
    kall_ref[0:p_len, :] = kc_ref[0]
    vall_ref[0:p_len, :] = vc_ref[0]
    kall_ref[p_len:p_len + ts, :] = kn_ref[...]
    vall_ref[p_len:p_len + ts, :] = vn_ref[...]
    tail = nkeys - p_len - ts
    kall_ref[p_len + ts:nkeys, :] = jnp.zeros((tail, HEAD), F32)
    vall_ref[p_len + ts:nkeys, :] = jnp.zeros((tail, HEAD), F32)
    kb = kall_ref[...].astype(BF16)
    vb = vall_ref[...].astype(BF16)
    bias = bias_ref[0]
    s = []
    for g, q_ref in enumerate((q1_ref, q2_ref, q3_ref)):
        sg = lax.dot_general(q_ref[...].astype(BF16), kb, _NT, preferred_element_type=F32)
        s.append(sg * scale + bias[g * ts:(g + 1) * ts, :])
    mx = [jnp.max(x, axis=-1, keepdims=True) for x in s]
    mm = jnp.maximum(jnp.maximum(mx[0], mx[1]), mx[2])
    p = jnp.exp(s[0] - mm) + jnp.exp(s[1] - mm) + jnp.exp(s[2] - mm)
    den = jnp.sum(p, axis=-1, keepdims=True)
    o = jnp.dot(p.astype(BF16), vb, preferred_element_type=F32)
    o_ref[...] = (o / den).astype(o_ref.dtype)


def _attn_sample(qproj, k_new, v_new, cache_k, cache_v, bias_table, *, nseq, ts):
    m, d3 = qproj.shape
    d = d3 // 3
    nh = d // HEAD
    p_len = cache_k.shape[1]
    nkeys = bias_table.shape[-1]
    qspec = lambda g: pl.BlockSpec((ts, HEAD), lambda n, h: (n, g * nh + h))
    newspec = pl.BlockSpec((ts, HEAD), lambda n, h: (n, h))
    cspec = pl.BlockSpec((1, p_len, HEAD), lambda n, h: (n, 0, h))
    blocks = (2 * (2 * _nbytes((p_len, HEAD), F32) + _nbytes((3 * ts, nkeys), F32))
              + 2 * _nbytes((nkeys, HEAD), F32) + 2 * _nbytes((nkeys, HEAD), BF16)
              + 8 * _nbytes((3 * ts, nkeys), F32))
    return pl.pallas_call(
        functools.partial(_attn_sample_body, p_len=p_len, ts=ts),
        grid=(nseq, nh),
        in_specs=[qspec(0), qspec(1), qspec(2), cspec, cspec, newspec, newspec,
                  pl.BlockSpec((1, 3 * ts, nkeys), lambda n, h: (h, 0, 0))],
        out_specs=pl.BlockSpec((ts, HEAD), lambda n, h: (n, h)),
        out_shape=jax.ShapeDtypeStruct((m, d), F32),
        scratch_shapes=[pltpu.VMEM((nkeys, HEAD), F32), pltpu.VMEM((nkeys, HEAD), F32)],
        compiler_params=pltpu.CompilerParams(
            dimension_semantics=("arbitrary", "arbitrary"),
            vmem_limit_bytes=_vmem_limit(blocks)),
        name="dilated_attention_sample",
    )(qproj, qproj, qproj, cache_k.reshape(nseq, p_len, d), cache_v.reshape(nseq, p_len, d),
      k_new, v_new, bias_table)


def _lower_bounds(lb_raw):
    p = jax.nn.softmax(lb_raw.astype(F32), axis=0)
    c = jnp.cumsum(p, axis=0)
    return c - c[0:1]


def kernel(x_prompt, x_sample, state_hgrn, state_ffn_conv, cache_k_win, cache_v_win, a_w_in, a_lb_raw, a_out_gain, a_w_out, kv_norm, w_kv, b_w_q, b_w_out, rel_bias, ffn_w_up, ffn_conv_w, ffn_conv_b, ffn_w_down, norm_mix, norm_ffn, norm_final):
    nb_p, t_p, d = x_prompt.shape
    nb_s, t_s, _ = x_sample.shape
    depth = ffn_w_up.shape[0]
    n_a = a_w_in.shape[0]
    nh = d // HEAD
    f = ffn_w_down.shape[1]
    p_len = cache_k_win.shape[1]
    assert t_p % (SUB_WINDOW * max(DILATIONS)) == 0 and t_s == V7X_SUBLANES

    xp = x_prompt.reshape(nb_p * t_p, d)
    xs = x_sample.reshape(nb_s * t_s, d)
    lbs = _lower_bounds(a_lb_raw)
    nkeys = -(-(p_len + t_s) // V7X_LANES) * V7X_LANES
    bias_p = _prompt_bias_tiles(rel_bias, nh)
    bias_s = _sample_bias_table(rel_bias, nh, p_len, t_s, nkeys)

    hgrn_p, hgrn_s, conv_p, conv_s = [], [], [], []
    kp = vp = ks = vs = None
    w_kv3 = w_kv.reshape(1, d, 2 * d)
    conv_b3 = ffn_conv_b.reshape(depth, 1, f)
    for l in range(depth):
        hp = _rmsnorm(xp, norm_mix[l], BF16)
        hs = _rmsnorm(xs, norm_mix[l], BF16)
        if l < n_a:
            pp, ps = _linear(hp, hs, a_w_in, l, name="hgrn2_in_proj")
            op, sp = _gla(pp, lbs[l], a_out_gain[l], None, nseq_total=nb_p, tlen=t_p, out_dtype=BF16)
            os_, ss = _gla(ps, lbs[l], a_out_gain[l], state_hgrn[l], nseq_total=nb_s, tlen=t_s,
                           out_dtype=F32)
            hgrn_p.append(sp)
            hgrn_s.append(ss)
            w_o, l_o = a_w_out, l
        else:
            if l == n_a:
                kvp = _rmsnorm(xp, kv_norm, BF16)
                kvs = _rmsnorm(xs, kv_norm, BF16)
                kp, ks = _linear(kvp, kvs, w_kv3, 0, col0=0, n=d, name="k_proj")
                vp, vs = _linear(kvp, kvs, w_kv3, 0, col0=d, n=d, name="v_proj")
            jb = l - n_a
            qp, qs = _linear(hp, hs, b_w_q, jb, name="q_proj")
            op = _attn_prompt(qp, kp, vp, bias_p, nseq=nb_p, tlen=t_p)
            os_ = _attn_sample(qs, ks, vs, cache_k_win, cache_v_win, bias_s, nseq=nb_s, ts=t_s)
            w_o, l_o = b_w_out, jb
        xp, xs = _linear(op, os_, w_o, l_o, res=(xp, xs), name="mixer_out_proj")

        hp = _rmsnorm(xp, norm_ffn[l], BF16)
        hs = _rmsnorm(xs, norm_ffn[l], BF16)
        prev = state_ffn_conv[l]
        ext_s = jnp.pad(prev, ((0, 0), (0, t_s - (CONV_W - 1)), (0, 0))).reshape(nb_s * t_s, f)
        actp, acts, tailp, a_s = _ffn_up(hp, hs, ffn_w_up, ffn_conv_w, conv_b3, ext_s, l,
                                         tlen=t_p, ts=t_s)
        conv_p.append(tailp.reshape(nb_p, 8, f)[:, 8 - (CONV_W - 1):, :])
        conv_s.append(a_s.reshape(nb_s, t_s, f)[:, t_s - (CONV_W - 1):, :])
        xp, xs = _linear(actp, acts, ffn_w_down, l, res=(xp, xs), name="ffn_down_proj")

    yp = _rmsnorm(xp, norm_final, F32).reshape(nb_p, t_p, d)
    ys = _rmsnorm(xs, norm_final, F32).reshape(nb_s, t_s, d)
    keep = min(BUCKET_MAX_DIST, t_p)
    k_win = kp.reshape(nb_p, t_p, nh, HEAD)[:, t_p - keep:]
    v_win = vp.reshape(nb_p, t_p, nh, HEAD)[:, t_p - keep:]
    return (yp, ys, jnp.stack(hgrn_p), jnp.stack(hgrn_s), jnp.stack(conv_p), jnp.stack(conv_s),
            k_win, v_win, ks.reshape(nb_s, t_s, nh, HEAD), vs.reshape(nb_s, t_s, nh, HEAD))
```

```python
import functools
import math

import numpy as np
import jax
import jax.numpy as jnp
from jax import lax
from jax.experimental import pallas as pl
from jax.experimental.pallas import tpu as pltpu

F32 = jnp.float32
BF16 = jnp.bfloat16

V7X_VMEM_BYTES = 64 * 1024 * 1024
V7X_LANES = 128
V7X_SUBLANES = 8
COMPILER_SCRATCH_BYTES = 12 * 1024 * 1024

HEAD = 128
GLA_CHUNK = 128
GLA_CHUNK_UNROLL = 8
SUB_WINDOW = 128
DILATIONS = (1, 4, 16)
N_BUCKETS = 32
BUCKET_MAX_DIST = 2048
CONV_W = 3
EPS = 1e-6
NEG = -1e30
LOG2_E = 1.4426950408889634


def _vmem_limit(block_bytes):
    return int(min(block_bytes + COMPILER_SCRATCH_BYTES, V7X_VMEM_BYTES - (2 << 20)))


def _nbytes(shape, dtype):
    return int(np.prod(shape)) * jnp.dtype(dtype).itemsize


def _rmsnorm_body(x_ref, g_ref, o_ref):
    x = x_ref[...]
    ms = jnp.mean(x * x, axis=-1, keepdims=True)
    o_ref[...] = (x * lax.rsqrt(ms + EPS) * g_ref[...]).astype(o_ref.dtype)


def _rmsnorm(x, g, out_dtype):
    m, d = x.shape
    tm = min(m, 256)
    assert m % tm == 0
    blocks = 2 * (_nbytes((tm, d), F32) + _nbytes((tm, d), out_dtype))
    return pl.pallas_call(
        _rmsnorm_body,
        grid=(m // tm,),
        in_specs=[pl.BlockSpec((tm, d), lambda i: (i, 0)),
                  pl.BlockSpec((1, d), lambda i: (0, 0))],
        out_specs=pl.BlockSpec((tm, d), lambda i: (i, 0)),
        out_shape=jax.ShapeDtypeStruct((m, d), out_dtype),
        compiler_params=pltpu.CompilerParams(vmem_limit_bytes=_vmem_limit(blocks)),
        name="rmsnorm",
    )(x, g.reshape(1, d))


W_CAST_ROWS = 256


def _cast_weight(w_ref, wb_ref):
    k = w_ref.shape[0]
    assert k % W_CAST_ROWS == 0

    def step(c, carry):
        rows = pl.ds(pl.multiple_of(c * W_CAST_ROWS, W_CAST_ROWS), W_CAST_ROWS)
        wb_ref[rows, :] = w_ref[rows, :].astype(BF16)
        return carry

    lax.fori_loop(0, k // W_CAST_ROWS, step, 0)


def _linear_body(*refs, layer, col0, kc, nchunks, spc, has_res):
    if has_res:
        xp_ref, xs_ref, w_hbm, rp_ref, rs_ref, yp_ref, ys_ref, wb_ref, stg_ref, sem = refs
    else:
        xp_ref, xs_ref, w_hbm, yp_ref, ys_ref, wb_ref, stg_ref, sem = refs
    j = pl.program_id(0)
    i = pl.program_id(1)
    nj = pl.num_programs(0)
    tn = wb_ref.shape[2]
    cur = j % 2

    def chunk_copy(jt, c):
        rows = pl.ds(pl.multiple_of(c * kc, kc), kc)
        cols = pl.ds(pl.multiple_of(col0 + jt * tn, tn), tn)
        return pltpu.make_async_copy(w_hbm.at[layer, rows, cols], stg_ref.at[c % 2], sem.at[c % 2])

    def land(jt, c, wslot):
        chunk_copy(jt, c).wait()
        rows = pl.ds(pl.multiple_of(c * kc, kc), kc)
        wb_ref[wslot, rows, :] = stg_ref[c % 2].astype(BF16)

    @pl.when(jnp.logical_and(j == 0, i == 0))
    def _first_tile():
        chunk_copy(0, 0).start()

        def step(c, carry):
            @pl.when(c + 1 < nchunks)
            def _():
                chunk_copy(0, c + 1).start()
            land(0, c, 0)
            return carry

        lax.fori_loop(0, nchunks, step, 0)

    @pl.when(jnp.logical_and(j > 0, i == 0))
    def _finish_prefetched_tile():
        land(j, nchunks - 1, cur)

    @pl.when(jnp.logical_and(j + 1 < nj, i % spc == 0))
    def _prefetch_next_tile():
        c = i // spc
        chunk_copy(j + 1, c).start()

        @pl.when(c > 0)
        def _():
            land(j + 1, c - 1, 1 - cur)

    @pl.when(i == 0)
    def _sample_rows():
        ys = jnp.dot(xs_ref[...].astype(BF16), wb_ref[cur], preferred_element_type=F32)
        if has_res:
            ys = ys + rs_ref[...]
        ys_ref[...] = ys

    yp = jnp.dot(xp_ref[...], wb_ref[cur], preferred_element_type=F32)
    if has_res:
        yp = yp + rp_ref[...]
    yp_ref[...] = yp


BF16_SUBLANES = 16
MAX_W_CHUNKS = 16


def _linear_tiles(mp, k, n, col0):
    tm, tn_max = (512, 1024) if k <= 4096 else (256, 512)
    tm = min(tm, mp)
    tn = next(t for t in (tn_max, tn_max // 2, tn_max // 4, V7X_LANES)
              if n % t == 0 and col0 % t == 0)
    ni = mp // tm
    nchunks = next(c for c in range(min(MAX_W_CHUNKS, ni), 0, -1)
                   if ni % c == 0 and k % (c * BF16_SUBLANES) == 0)
    return tm, tn, k // nchunks, nchunks, ni // nchunks


def _linear(xp, xs, w, layer, *, col0=0, n=None, res=None, name):
    mp, k = xp.shape
    ms = xs.shape[0]
    n = w.shape[2] if n is None else n
    tm, tn, kc, nchunks, spc = _linear_tiles(mp, k, n, col0)
    assert mp % tm == 0 and n % tn == 0
    in_specs = [pl.BlockSpec((tm, k), lambda j, i: (i, 0)),
                pl.BlockSpec((ms, k), lambda j, i: (0, 0)),
                pl.BlockSpec(memory_space=pl.ANY)]
    args = [xp, xs, w]
    blocks = (2 * (_nbytes((tm, k), BF16) + _nbytes((ms, k), xs.dtype)
                   + _nbytes((tm, tn), F32) + _nbytes((ms, tn), F32))
              + 2 * _nbytes((k, tn), BF16) + 2 * _nbytes((kc, tn), F32))
    if res is not None:
        in_specs += [pl.BlockSpec((tm, tn), lambda j, i: (i, j)),
                     pl.BlockSpec((ms, tn), lambda j, i: (0, j))]
        args += list(res)
        blocks += 2 * (_nbytes((tm, tn), F32) + _nbytes((ms, tn), F32))
    return pl.pallas_call(
        functools.partial(_linear_body, layer=layer, col0=col0, kc=kc, nchunks=nchunks, spc=spc,
                          has_res=res is not None),
        grid=(n // tn, mp // tm),
        in_specs=in_specs,
        out_specs=[pl.BlockSpec((tm, tn), lambda j, i: (i, j)),
                   pl.BlockSpec((ms, tn), lambda j, i: (0, j))],
        out_shape=[jax.ShapeDtypeStruct((mp, n), F32), jax.ShapeDtypeStruct((ms, n), F32)],
        scratch_shapes=[pltpu.VMEM((2, k, tn), BF16), pltpu.VMEM((2, kc, tn), F32),
                        pltpu.SemaphoreType.DMA((2,))],
        compiler_params=pltpu.CompilerParams(
            dimension_semantics=("arbitrary", "arbitrary"),
            vmem_limit_bytes=_vmem_limit(blocks)),
        name=name,
    )(*args)


FFN_STEP_ROWS = 1024
FFN_SUB_ROWS = 256


def _ffn_up_body(hp_ref, hs_ref, wa_ref, wg_ref, cw_ref, cb_ref, exts_ref,
                 actp_ref, acts_ref, tailp_ref, as_ref,
                 wab_ref, wgb_ref, ext_ref, *, tm, tlen, ts):
    i = pl.program_id(1)
    cw = cw_ref[...]
    cb = cb_ref[...]

    def act_fn(a2, a1, a0, b):
        conv = cb + cw[0:1, :] * a2 + cw[1:2, :] * a1 + cw[2:3, :] * a0
        return jax.nn.silu(conv) * b

    @pl.when(i == 0)
    def _new_column_tile():
        _cast_weight(wa_ref, wab_ref)
        _cast_weight(wg_ref, wgb_ref)
        hs = hs_ref[...].astype(BF16)
        a = jnp.dot(hs, wab_ref[...], preferred_element_type=F32)
        b = jnp.dot(hs, wgb_ref[...], preferred_element_type=F32)
        ms = a.shape[0]
        t = lax.broadcasted_iota(jnp.int32, a.shape, 0) % ts
        ext = exts_ref[...]
        a1 = jnp.where(t == 0, pltpu.roll(ext, ms - 1, axis=0), pltpu.roll(a, 1, axis=0))
        a2 = jnp.where(t < 2, ext, pltpu.roll(a, 2, axis=0))
        acts_ref[...] = act_fn(a2, a1, a, b).astype(acts_ref.dtype)
        as_ref[...] = a

    @pl.when((i * tm) % tlen == 0)
    def _sequence_start():
        ext_ref[...] = jnp.zeros(ext_ref.shape, F32)

    tail = ext_ref[...]
    for s in range(tm // FFN_SUB_ROWS):
        rows = slice(s * FFN_SUB_ROWS, (s + 1) * FFN_SUB_ROWS)
        hp = hp_ref[rows, :]
        a = jnp.dot(hp, wab_ref[...], preferred_element_type=F32)
        b = jnp.dot(hp, wgb_ref[...], preferred_element_type=F32)
        ext = jnp.concatenate([tail, a], axis=0)
        a1 = ext[7:7 + FFN_SUB_ROWS, :]
        a2 = ext[6:6 + FFN_SUB_ROWS, :]
        actp_ref[rows, :] = act_fn(a2, a1, a, b).astype(actp_ref.dtype)
        tail = a[FFN_SUB_ROWS - 8:FFN_SUB_ROWS, :]
    ext_ref[...] = tail

    @pl.when(((i + 1) * tm) % tlen == 0)
    def _sequence_end():
        n = (i * tm) // tlen
        tailp_ref[pl.ds(pl.multiple_of(n * 8, 8), 8), :] = tail


def _ffn_up(hp, hs, w_up, conv_w, conv_b, ext_s, layer, *, tlen, ts):
    mp, d = hp.shape
    ms = hs.shape[0]
    f = w_up.shape[2] // 2
    nseq = mp // tlen
    tm = min(FFN_STEP_ROWS, tlen)
    tf = 256
    assert mp % tm == 0 and tlen % tm == 0 and f % tf == 0 and tm % FFN_SUB_ROWS == 0
    nf = f // tf
    blocks = (2 * (_nbytes((tm, d), BF16) + _nbytes((ms, d), hs.dtype) + 2 * _nbytes((d, tf), F32)
                   + _nbytes((tm, tf), BF16) + 3 * _nbytes((ms, tf), F32) + _nbytes((nseq * 8, tf), F32))
              + 2 * _nbytes((d, tf), BF16) + _nbytes((8, tf), F32) + 5 * _nbytes((tm, tf), F32))
    return pl.pallas_call(
        functools.partial(_ffn_up_body, tm=tm, tlen=tlen, ts=ts),
        grid=(nf, mp // tm),
        in_specs=[pl.BlockSpec((tm, d), lambda j, i: (i, 0)),
                  pl.BlockSpec((ms, d), lambda j, i: (0, 0)),
                  pl.BlockSpec((None, d, tf), lambda j, i: (layer, 0, j)),
                  pl.BlockSpec((None, d, tf), lambda j, i: (layer, 0, j + nf)),
                  pl.BlockSpec((None, CONV_W, tf), lambda j, i: (layer, 0, j)),
                  pl.BlockSpec((None, 1, tf), lambda j, i: (layer, 0, j)),
                  pl.BlockSpec((ms, tf), lambda j, i: (0, j))],
        out_specs=[pl.BlockSpec((tm, tf), lambda j, i: (i, j)),
                   pl.BlockSpec((ms, tf), lambda j, i: (0, j)),
                   pl.BlockSpec((nseq * 8, tf), lambda j, i: (0, j)),
                   pl.BlockSpec((ms, tf), lambda j, i: (0, j))],
        out_shape=[jax.ShapeDtypeStruct((mp, f), BF16),
                   jax.ShapeDtypeStruct((ms, f), BF16),
                   jax.ShapeDtypeStruct((nseq * 8, f), F32),
                   jax.ShapeDtypeStruct((ms, f), F32)],
        scratch_shapes=[pltpu.VMEM((d, tf), BF16), pltpu.VMEM((d, tf), BF16),
                        pltpu.VMEM((8, tf), F32)],
        compiler_params=pltpu.CompilerParams(
            dimension_semantics=("arbitrary", "arbitrary"),
            vmem_limit_bytes=_vmem_limit(blocks)),
        name="ffn_up",
    )(hp, hs, w_up, w_up, conv_w, conv_b, ext_s)


def _level_table(c):
    t = np.arange(c)[:, None]
    s = np.arange(c)[None, :]
    x = np.maximum(t ^ s, 1)
    lev = np.floor(np.log2(x)).astype(np.int32)
    return np.where(t > s, lev, np.where(t == s, -1, -2)).astype(np.int32)


def _cumsum_table(c):
    return np.tril(np.ones((c, c), np.float32))


def _split3(x):
    x0 = x.astype(BF16)
    r1 = x - x0.astype(F32)
    x1 = r1.astype(BF16)
    x2 = (r1 - x1.astype(F32)).astype(BF16)
    return x0, x1, x2


def _block_boundary_rows(b, blk):
    c, w = b.shape
    half = blk // 2
    if blk >= 2 * V7X_SUBLANES:
        x = b.reshape(c // blk, blk, w)
        r = jnp.broadcast_to(x[:, half - 1:half, :], x.shape)
        return r.reshape(c, w)
    x = b.reshape(c // V7X_SUBLANES, V7X_SUBLANES, w)
    sub = lax.broadcasted_iota(jnp.int32, x.shape, 1)

    def row(i):
        return jnp.broadcast_to(x[:, i:i + 1, :], x.shape)

    if blk == 8:
        r = row(3)
    elif blk == 4:
        r = jnp.where(sub < 4, row(1), row(5))
    else:
        r = jnp.where(sub < 2, row(0), jnp.where(sub < 4, row(2), jnp.where(sub < 6, row(4), row(6))))
    return r.reshape(c, w)


_NT = (((1,), (1,)), ((), ()))
_TN = (((0,), (0,)), ((), ()))


def _gla_body(*refs, nseq, tlen, valid, has_s0):
    if has_s0:
        (q_ref, f_ref, v_ref, g_ref, lb_ref, gain_ref, lev_ref, tri_ref, s0_ref,
         o_ref, so_ref) = refs
    else:
        q_ref, f_ref, v_ref, g_ref, lb_ref, gain_ref, lev_ref, tri_ref, o_ref, so_ref = refs
    c = GLA_CHUNK
    rows_in = min(c, valid)
    nchunk = max(tlen // c, 1)
    lb = lb_ref[...]
    log_lb = jnp.log(lb)
    log1m_lb = jnp.log1p(-lb)
    one_m_lb = 1.0 - lb
    gain = gain_ref[...]
    row_id = lax.broadcasted_iota(jnp.int32, (c, HEAD), 0)

    def padded(ref, rows):
        x = ref[rows, :]
        if rows_in < c:
            x = jnp.concatenate([x, jnp.zeros((c - rows_in, HEAD), F32)], axis=0)
        return x

    def chunk(n, ci, st):
        row0 = pl.multiple_of(n * tlen + ci * rows_in, rows_in)
        rows = pl.ds(row0, rows_in)
        qr = padded(q_ref, rows)
        fz = padded(f_ref, rows)
        v = padded(v_ref, rows)
        e = jnp.exp(-jnp.abs(fz))
        one_p_e = 1.0 + e
        log_sig = jnp.minimum(fz, 0.0) - jnp.log(one_p_e)
        u = log1m_lb + log_sig
        log_f = jnp.maximum(log_lb, u) + jnp.log(1.0 + jnp.exp(-jnp.abs(log_lb - u)))
        k = one_m_lb * (jnp.where(fz > 0.0, e, 1.0) / one_p_e)
        if rows_in < c:
            log_f = jnp.where(row_id < rows_in, log_f, 0.0)
            k = jnp.where(row_id < rows_in, k, 0.0)
        q = jax.nn.silu(qr) * (HEAD ** -0.5)
        vb = v.astype(BF16)

        tri = tri_ref[...]
        f0, f1, f2 = _split3(log_f)
        b = (jnp.dot(tri, f0, preferred_element_type=F32)
             + jnp.dot(tri, f1, preferred_element_type=F32)
             + jnp.dot(tri, f2, preferred_element_type=F32))
        b = b * LOG2_E
        b_last = b[c - 1:c, :]

        lev = lev_ref[...]
        diag = jnp.sum(q * k, axis=-1, keepdims=True)
        att = jnp.where(lev == -1, diag, 0.0)
        for l in range(int(math.log2(c))):
            r = _block_boundary_rows(b, 2 << l)
            decay = jnp.exp2(-jnp.abs(b - r))
            qt = (q * decay).astype(BF16)
            kt = (k * decay).astype(BF16)
            s_l = lax.dot_general(qt, kt, _NT, preferred_element_type=F32)
            att = jnp.where(lev == l, s_l, att)
        o = jnp.dot(att.astype(BF16), vb, preferred_element_type=F32)
        o = o + lax.dot_general((q * jnp.exp2(b)).astype(BF16), st.astype(BF16), _NT,
                                preferred_element_type=F32)
        kd = (k * jnp.exp2(b_last - b)).astype(BF16)
        st_new = st * jnp.exp2(b_last) + lax.dot_general(vb, kd, _TN, preferred_element_type=F32)

        o = o * lax.rsqrt(jnp.mean(o * o, axis=-1, keepdims=True) + EPS) * gain
        g = padded(g_ref, rows)
        o = o * jax.nn.silu(g)
        o_ref[rows, :] = o[0:rows_in, :].astype(o_ref.dtype)
        return st_new

    def sequence(n, carry):
        if has_s0:
            st0 = s0_ref[n, 0].T
        else:
            st0 = jnp.zeros((HEAD, HEAD), F32)
        st = lax.fori_loop(0, nchunk, lambda ci, st: chunk(n, ci, st), st0,
                           unroll=min(GLA_CHUNK_UNROLL, nchunk))
        so_ref[n, 0] = st.T
        return carry

    lax.fori_loop(0, nseq, sequence, 0)


def _gla(proj, lb, gain, s0, *, nseq_total, tlen, out_dtype):
    m, d4 = proj.shape
    d = d4 // 4
    nh = d // HEAD
    nseq = 1 if tlen >= GLA_CHUNK else nseq_total
    assert tlen % GLA_CHUNK == 0 or tlen < GLA_CHUNK
    rows = nseq * tlen
    col = lambda off: pl.BlockSpec((rows, HEAD), lambda n, h: (n, h + off))
    vec = pl.BlockSpec((1, HEAD), lambda n, h: (0, h))
    const = pl.BlockSpec((GLA_CHUNK, GLA_CHUNK), lambda n, h: (0, 0))
    in_specs = [col(0), col(nh), col(2 * nh), col(3 * nh), vec, vec, const, const]
    args = [proj, proj, proj, proj, lb.reshape(1, d), gain.reshape(1, d),
            jnp.asarray(_level_table(GLA_CHUNK)), jnp.asarray(_cumsum_table(GLA_CHUNK), BF16)]
    state_spec = pl.BlockSpec((nseq, 1, HEAD, HEAD), lambda n, h: (n, h, 0, 0))
    if s0 is not None:
        in_specs.append(state_spec)
        args.append(s0)
    blocks = 2 * (4 * _nbytes((rows, HEAD), F32) + _nbytes((rows, HEAD), out_dtype)
                  + 2 * _nbytes((nseq, 1, HEAD, HEAD), F32)) + (4 << 20)
    return pl.pallas_call(
        functools.partial(_gla_body, nseq=nseq, tlen=tlen, valid=tlen, has_s0=s0 is not None),
        grid=(nseq_total // nseq, nh),
        in_specs=in_specs,
        out_specs=[pl.BlockSpec((rows, HEAD), lambda n, h: (n, h)), state_spec],
        out_shape=[jax.ShapeDtypeStruct((m, d), out_dtype),
                   jax.ShapeDtypeStruct((nseq_total, nh, HEAD, HEAD), F32)],
        compiler_params=pltpu.CompilerParams(
            dimension_semantics=("arbitrary", "arbitrary"),
            vmem_limit_bytes=_vmem_limit(blocks)),
        name="hgrn2_recurrence",
    )(*args)


def _t5_buckets(dist):
    max_exact = N_BUCKETS // 2
    n = np.asarray(dist, dtype=np.int64)
    large = max_exact + (np.log(np.maximum(n, 1) / max_exact)
                         / np.log(BUCKET_MAX_DIST / max_exact)
                         * (N_BUCKETS - max_exact)).astype(np.int64)
    large = np.minimum(large, N_BUCKETS - 1)
    return np.where(n < max_exact, n, large).astype(np.int32)


def _bucket_lookup(tab, buckets):
    flat = jnp.asarray(buckets.reshape(1, -1))
    onehot = (flat == lax.broadcasted_iota(jnp.int32, (N_BUCKETS, flat.shape[1]), 0)).astype(F32)
    out = jnp.einsum("bh,bx->hx", tab, onehot, precision=lax.Precision.HIGHEST)
    return out.reshape((tab.shape[1],) + buckets.shape)


def _prompt_bias_tiles(rel_bias, nh):
    bl = SUB_WINDOW
    off = np.arange(bl)[:, None] + bl - np.arange(2 * bl)[None, :]
    band = (off >= 0) & (off <= SUB_WINDOW)
    tiles = []
    for g, dil in enumerate(DILATIONS):
        buckets = _t5_buckets(np.clip(off, 0, SUB_WINDOW) * dil)
        tab = rel_bias[:, g * nh:(g + 1) * nh].astype(F32)
        tiles.append(jnp.where(band[None], _bucket_lookup(tab, buckets), NEG))
    return jnp.stack(tiles)


ATTN_BLOCK_UNROLL = 8


def _attn_prompt_body(q1_ref, q2_ref, q3_ref, k_ref, v_ref, bias_ref, o_ref,
                      acc_ref, lse_ref, *, tlen):
    bl = SUB_WINDOW
    scale = HEAD ** -0.5

    for g, (dil, q_ref) in enumerate(zip(DILATIONS, (q1_ref, q2_ref, q3_ref))):
        nblk = tlen // (dil * bl)
        with_prev = nblk > 1

        def block(it, carry, dil=dil, q_ref=q_ref, nblk=nblk, g=g, with_prev=with_prev):
            r = it // nblk
            bi = it % nblk
            start = r + dil * bl * bi
            qrows = pl.ds(start, bl, stride=dil) if dil > 1 else pl.ds(start, bl)
            if with_prev:
                kstart = start - jnp.where(bi > 0, dil * bl, 0)
                nkey = 2 * bl
                bias = bias_ref[g, 0]
                bias_first = jnp.concatenate([bias[:, bl:2 * bl], jnp.full((bl, bl), NEG, F32)], axis=1)
                bias = jnp.where(bi == 0, bias_first, bias)
            else:
                kstart = start
                nkey = bl
                bias = bias_ref[g, 0, :, bl:2 * bl]
            krows = pl.ds(kstart, nkey, stride=dil) if dil > 1 else pl.ds(kstart, nkey)
            qb = q_ref[qrows, :].astype(BF16)
            k2 = k_ref[krows, :].astype(BF16)
            v2 = v_ref[krows, :].astype(BF16)
            s = lax.dot_general(qb, k2, _NT, preferred_element_type=F32) * scale + bias
            mx = jnp.max(s, axis=-1, keepdims=True)
            p = jnp.exp(s - mx)
            sm = jnp.sum(p, axis=-1, keepdims=True)
            acc = jnp.dot(p.astype(BF16), v2, preferred_element_type=F32)
            acc_ref[g, qrows, :] = acc / sm
            lse_ref[g, qrows, :] = jnp.broadcast_to(mx + jnp.log(sm), (bl, HEAD))
            return carry

        if with_prev:
            lax.fori_loop(0, dil * nblk, block, 0, unroll=ATTN_BLOCK_UNROLL)
        else:
            for it in range(dil * nblk):
                block(it, 0)

    def merge(ci, carry):
        rows = pl.ds(pl.multiple_of(ci * bl, bl), bl)
        l1, l2, l3 = lse_ref[0, rows, :], lse_ref[1, rows, :], lse_ref[2, rows, :]
        mm = jnp.maximum(jnp.maximum(l1, l2), l3)
        w1, w2, w3 = jnp.exp(l1 - mm), jnp.exp(l2 - mm), jnp.exp(l3 - mm)
        num = w1 * acc_ref[0, rows, :] + w2 * acc_ref[1, rows, :] + w3 * acc_ref[2, rows, :]
        o_ref[rows, :] = (num / (w1 + w2 + w3)).astype(o_ref.dtype)
        return carry

    lax.fori_loop(0, tlen // bl, merge, 0, unroll=2)


def _attn_prompt(qproj, k, v, bias_tiles, *, nseq, tlen):
    m, d3 = qproj.shape
    d = d3 // 3
    nh = d // HEAD
    qspec = lambda g: pl.BlockSpec((tlen, HEAD), lambda n, h: (n, g * nh + h))
    kvspec = pl.BlockSpec((tlen, HEAD), lambda n, h: (n, h))
    blocks = (2 * (5 * _nbytes((tlen, HEAD), F32) + _nbytes((3, 1, SUB_WINDOW, 2 * SUB_WINDOW), F32)
                   + _nbytes((tlen, HEAD), BF16))
              + 6 * _nbytes((tlen, HEAD), F32))
    return pl.pallas_call(
        functools.partial(_attn_prompt_body, tlen=tlen),
        grid=(nseq, nh),
        in_specs=[qspec(0), qspec(1), qspec(2), kvspec, kvspec,
                  pl.BlockSpec((3, 1, SUB_WINDOW, 2 * SUB_WINDOW), lambda n, h: (0, h, 0, 0))],
        out_specs=pl.BlockSpec((tlen, HEAD), lambda n, h: (n, h)),
        out_shape=jax.ShapeDtypeStruct((m, d), BF16),
        scratch_shapes=[pltpu.VMEM((3, tlen, HEAD), F32), pltpu.VMEM((3, tlen, HEAD), F32)],
        compiler_params=pltpu.CompilerParams(
            dimension_semantics=("arbitrary", "arbitrary"),
            vmem_limit_bytes=_vmem_limit(blocks)),
        name="dilated_attention_prompt",
    )(qproj, qproj, qproj, k, v, bias_tiles)


def _sample_bias_table(rel_bias, nh, p_len, ts, nkeys):
    t = np.arange(ts)[:, None]
    row = np.arange(nkeys)[None, :]
    dist = p_len + t - row
    out = []
    for g, dil in enumerate(DILATIONS):
        j = dist // dil
        ok = (dist >= 0) & (dist % dil == 0) & (j <= SUB_WINDOW) & (row < p_len + ts)
        buckets = _t5_buckets(np.clip(j, 0, SUB_WINDOW) * dil)
        tab = rel_bias[:, g * nh:(g + 1) * nh].astype(F32)
        out.append(jnp.where(ok[None], _bucket_lookup(tab, buckets), NEG))
    return jnp.concatenate(out, axis=1)


def _attn_sample_body(q1_ref, q2_ref, q3_ref, kc_ref, vc_ref, kn_ref, vn_ref, bias_ref, o_ref,
                      kall_ref, vall_ref, *, p_len, ts):
    nkeys = kall_ref.shape[0]
    scale = HEAD ** -0.5
    kall_ref[0:p_len, :] = kc_ref[0]
    vall_ref[0:p_len, :] = vc_ref[0]
    kall_ref[p_len:p_len + ts, :] = kn_ref[...]
    vall_ref[p_len:p_len + ts, :] = vn_ref[...]
    tail = nkeys - p_len - ts
    kall_ref[p_len + ts:nkeys, :] = jnp.zeros((tail, HEAD), F32)
    vall_ref[p_len + ts:nkeys, :] = jnp.zeros((tail, HEAD), F32)
    kb = kall_ref[...].astype(BF16)
    vb = vall_ref[...].astype(BF16)
    bias = bias_ref[0]
    s = []
    for g, q_ref in enumerate((q1_ref, q2_ref, q3_ref)):
        sg = lax.dot_general(q_ref[...].astype(BF16), kb, _NT, preferred_element_type=F32)
        s.append(sg * scale + bias[g * ts:(g + 1) * ts, :])
    mx = [jnp.max(x, axis=-1, keepdims=True) for x in s]
    mm = jnp.maximum(jnp.maximum(mx[0], mx[1]), mx[2])
    p = jnp.exp(s[0] - mm) + jnp.exp(s[1] - mm) + jnp.exp(s[2] - mm)
    den = jnp.sum(p, axis=-1, keepdims=True)
    o = jnp.dot(p.astype(BF16), vb, preferred_element_type=F32)
    o_ref[...] = (o / den).astype(o_ref.dtype)


def _attn_sample(qproj, k_new, v_new, cache_k, cache_v, bias_table, *, nseq, ts):
    m, d3 = qproj.shape
    d = d3 // 3
    nh = d // HEAD
    p_len = cache_k.shape[1]
    nkeys = bias_table.shape[-1]
    qspec = lambda g: pl.BlockSpec((ts, HEAD), lambda n, h: (n, g * nh + h))
    newspec = pl.BlockSpec((ts, HEAD), lambda n, h: (n, h))
    cspec = pl.BlockSpec((1, p_len, HEAD), lambda n, h: (n, 0, h))
    blocks = (2 * (2 * _nbytes((p_len, HEAD), F32) + _nbytes((3 * ts, nkeys), F32))
              + 2 * _nbytes((nkeys, HEAD), F32) + 2 * _nbytes((nkeys, HEAD), BF16)
              + 8 * _nbytes((3 * ts, nkeys), F32))
    return pl.pallas_call(
        functools.partial(_attn_sample_body, p_len=p_len, ts=ts),
        grid=(nseq, nh),
        in_specs=[qspec(0), qspec(1), qspec(2), cspec, cspec, newspec, newspec,
                  pl.BlockSpec((1, 3 * ts, nkeys), lambda n, h: (h, 0, 0))],
        out_specs=pl.BlockSpec((ts, HEAD), lambda n, h: (n, h)),
        out_shape=jax.ShapeDtypeStruct((m, d), F32),
        scratch_shapes=[pltpu.VMEM((nkeys, HEAD), F32), pltpu.VMEM((nkeys, HEAD), F32)],
        compiler_params=pltpu.CompilerParams(
            dimension_semantics=("arbitrary", "arbitrary"),
            vmem_limit_bytes=_vmem_limit(blocks)),
        name="dilated_attention_sample",
    )(qproj, qproj, qproj, cache_k.reshape(nseq, p_len, d), cache_v.reshape(nseq, p_len, d),
      k_new, v_new, bias_table)


def _lower_bounds(lb_raw):
    p = jax.nn.softmax(lb_raw.astype(F32), axis=0)
    c = jnp.cumsum(p, axis=0)
    return c - c[0:1]


def kernel(x_prompt, x_sample, state_hgrn, state_ffn_conv, cache_k_win, cache_v_win, a_w_in, a_lb_raw, a_out_gain, a_w_out, kv_norm, w_kv, b_w_q, b_w_out, rel_bias, ffn_w_up, ffn_conv_w, ffn_conv_b, ffn_w_down, norm_mix, norm_ffn, norm_final):
    nb_p, t_p, d = x_prompt.shape
    nb_s, t_s, _ = x_sample.shape
    depth = ffn_w_up.shape[0]
    n_a = a_w_in.shape[0]
    nh = d // HEAD
    f = ffn_w_down.shape[1]
    p_len = cache_k_win.shape[1]
    assert t_p % (SUB_WINDOW * max(DILATIONS)) == 0 and t_s == V7X_SUBLANES

    xp = x_prompt.reshape(nb_p * t_p, d)
    xs = x_sample.reshape(nb_s * t_s, d)
    lbs = _lower_bounds(a_lb_raw)
    nkeys = -(-(p_len + t_s) // V7X_LANES) * V7X_LANES
    bias_p = _prompt_bias_tiles(rel_bias, nh)
    bias_s = _sample_bias_table(rel_bias, nh, p_len, t_s, nkeys)

    hgrn_p, hgrn_s, conv_p, conv_s = [], [], [], []
    kp = vp = ks = vs = None
    w_kv3 = w_kv.reshape(1, d, 2 * d)
    conv_b3 = ffn_conv_b.reshape(depth, 1, f)
    for l in range(depth):
        hp = _rmsnorm(xp, norm_mix[l], BF16)
        hs = _rmsnorm(xs, norm_mix[l], BF16)
        if l < n_a:
            pp, ps = _linear(hp, hs, a_w_in, l, name="hgrn2_in_proj")
            op, sp = _gla(pp, lbs[l], a_out_gain[l], None, nseq_total=nb_p, tlen=t_p, out_dtype=BF16)
            os_, ss = _gla(ps, lbs[l], a_out_gain[l], state_hgrn[l], nseq_total=nb_s, tlen=t_s,
                           out_dtype=F32)
            hgrn_p.append(sp)
            hgrn_s.append(ss)
            w_o, l_o = a_w_out, l
        else:
            if l == n_a:
                kvp = _rmsnorm(xp, kv_norm, BF16)
                kvs = _rmsnorm(xs, kv_norm, BF16)
                kp, ks = _linear(kvp, kvs, w_kv3, 0, col0=0, n=d, name="k_proj")
                vp, vs = _linear(kvp, kvs, w_kv3, 0, col0=d, n=d, name="v_proj")
            jb = l - n_a
            qp, qs = _linear(hp, hs, b_w_q, jb, name="q_proj")
            op = _attn_prompt(qp, kp, vp, bias_p, nseq=nb_p, tlen=t_p)
            os_ = _attn_sample(qs, ks, vs, cache_k_win, cache_v_win, bias_s, nseq=nb_s, ts=t_s)
            w_o, l_o = b_w_out, jb
        xp, xs = _linear(op, os_, w_o, l_o, res=(xp, xs), name="mixer_out_proj")

        hp = _rmsnorm(xp, norm_ffn[l], BF16)
        hs = _rmsnorm(xs, norm_ffn[l], BF16)
        prev = state_ffn_conv[l]
        ext_s = jnp.pad(prev, ((0, 0), (0, t_s - (CONV_W - 1)), (0, 0))).reshape(nb_s * t_s, f)
        actp, acts, tailp, a_s = _ffn_up(hp, hs, ffn_w_up, ffn_conv_w, conv_b3, ext_s, l,
                                         tlen=t_p, ts=t_s)
        conv_p.append(tailp.reshape(nb_p, 8, f)[:, 8 - (CONV_W - 1):, :])
        conv_s.append(a_s.reshape(nb_s, t_s, f)[:, t_s - (CONV_W - 1):, :])
        xp, xs = _linear(actp, acts, ffn_w_down, l, res=(xp, xs), name="ffn_down_proj")

    yp = _rmsnorm(xp, norm_final, F32).reshape(nb_p, t_p, d)
    ys = _rmsnorm(xs, norm_final, F32).reshape(nb_s, t_s, d)
    keep = min(BUCKET_MAX_DIST, t_p)
    k_win = kp.reshape(nb_p, t_p, nh, HEAD)[:, t_p - keep:]
    v_win = vp.reshape(nb_p, t_p, nh, HEAD)[:, t_p - keep:]
    return (yp, ys, jnp.stack(hgrn_p), jnp.stack(hgrn_s), jnp.stack(conv_p), jnp.stack(conv_s),
            k_win, v_win, ks.reshape(nb_s, t_s, nh, HEAD), vs.reshape(nb_s, t_s, nh, HEAD))
```

```python
import functools
import math

import numpy as np
import jax
import jax.numpy as jnp
from jax import lax
from jax.experimental import pallas as pl
from jax.experimental.pallas import tpu as pltpu

F32 = jnp.float32
BF16 = jnp.bfloat16

V7X_VMEM_BYTES = 64 * 1024 * 1024
V7X_LANES = 128
V7X_SUBLANES = 8
COMPILER_SCRATCH_BYTES = 12 * 1024 * 1024

HEAD = 128
GLA_CHUNK = 128
GLA_CHUNK_UNROLL = 8
SUB_WINDOW = 128
DILATIONS = (1, 4, 16)
N_BUCKETS = 32
BUCKET_MAX_DIST = 2048
CONV_W = 3
EPS = 1e-6
NEG = -1e30
LOG2_E = 1.4426950408889634


def _vmem_limit(block_bytes):
    return int(min(block_bytes + COMPILER_SCRATCH_BYTES, V7X_VMEM_BYTES - (2 << 20)))


def _nbytes(shape, dtype):
    return int(np.prod(shape)) * jnp.dtype(dtype).itemsize


def _rmsnorm_body(x_ref, g_ref, o_ref):
    x = x_ref[...]
    ms = jnp.mean(x * x, axis=-1, keepdims=True)
    o_ref[...] = (x * lax.rsqrt(ms + EPS) * g_ref[...]).astype(o_ref.dtype)


def _rmsnorm(x, g, out_dtype):
    m, d = x.shape
    tm = min(m, 256)
    assert m % tm == 0
    blocks = 2 * (_nbytes((tm, d), F32) + _nbytes((tm, d), out_dtype))
    return pl.pallas_call(
        _rmsnorm_body,
        grid=(m // tm,),
        in_specs=[pl.BlockSpec((tm, d), lambda i: (i, 0)),
                  pl.BlockSpec((1, d), lambda i: (0, 0))],
        out_specs=pl.BlockSpec((tm, d), lambda i: (i, 0)),
        out_shape=jax.ShapeDtypeStruct((m, d), out_dtype),
        compiler_params=pltpu.CompilerParams(vmem_limit_bytes=_vmem_limit(blocks)),
        name="rmsnorm",
    )(x, g.reshape(1, d))


W_CAST_ROWS = 256


def _cast_weight(w_ref, wb_ref):
    k = w_ref.shape[0]
    assert k % W_CAST_ROWS == 0

    def step(c, carry):
        rows = pl.ds(pl.multiple_of(c * W_CAST_ROWS, W_CAST_ROWS), W_CAST_ROWS)
        wb_ref[rows, :] = w_ref[rows, :].astype(BF16)
        return carry

    lax.fori_loop(0, k // W_CAST_ROWS, step, 0)


def _linear_body(*refs, layer, col0, kc, nchunks, spc, has_res):
    if has_res:
        xp_ref, xs_ref, w_hbm, rp_ref, rs_ref, yp_ref, ys_ref, wb_ref, stg_ref, sem = refs
    else:
        xp_ref, xs_ref, w_hbm, yp_ref, ys_ref, wb_ref, stg_ref, sem = refs
    j = pl.program_id(0)
    i = pl.program_id(1)
    nj = pl.num_programs(0)
    tn = wb_ref.shape[2]
    cur = j % 2

    def chunk_copy(jt, c):
        rows = pl.ds(pl.multiple_of(c * kc, kc), kc)
        cols = pl.ds(pl.multiple_of(col0 + jt * tn, tn), tn)
        return pltpu.make_async_copy(w_hbm.at[layer, rows, cols], stg_ref.at[c % 2], sem.at[c % 2])

    def land(jt, c, wslot):
        chunk_copy(jt, c).wait()
        rows = pl.ds(pl.multiple_of(c * kc, kc), kc)
        wb_ref[wslot, rows, :] = stg_ref[c % 2].astype(BF16)

    @pl.when(jnp.logical_and(j == 0, i == 0))
    def _first_tile():
        chunk_copy(0, 0).start()

        def step(c, carry):
            @pl.when(c + 1 < nchunks)
            def _():
                chunk_copy(0, c + 1).start()
            land(0, c, 0)
            return carry

        lax.fori_loop(0, nchunks, step, 0)

    @pl.when(jnp.logical_and(j > 0, i == 0))
    def _finish_prefetched_tile():
        land(j, nchunks - 1, cur)

    @pl.when(jnp.logical_and(j + 1 < nj, i % spc == 0))
    def _prefetch_next_tile():
        c = i // spc
        chunk_copy(j + 1, c).start()

        @pl.when(c > 0)
        def _():
            land(j + 1, c - 1, 1 - cur)

    @pl.when(i == 0)
    def _sample_rows():
        ys = jnp.dot(xs_ref[...].astype(BF16), wb_ref[cur], preferred_element_type=F32)
        if has_res:
            ys = ys + rs_ref[...]
        ys_ref[...] = ys

    yp = jnp.dot(xp_ref[...], wb_ref[cur], preferred_element_type=F32)
    if has_res:
        yp = yp + rp_ref[...]
    yp_ref[...] = yp


BF16_SUBLANES = 16
MAX_W_CHUNKS = 16


def _linear_tiles(mp, k, n, col0):
    tm, tn_max = (512, 1024) if k <= 4096 else (256, 512)
    tm = min(tm, mp)
    tn = next(t for t in (tn_max, tn_max // 2, tn_max // 4, V7X_LANES)
              if n % t == 0 and col0 % t == 0)
    ni = mp // tm
    nchunks = next(c for c in range(min(MAX_W_CHUNKS, ni), 0, -1)
                   if ni % c == 0 and k % (c * BF16_SUBLANES) == 0)
    return tm, tn, k // nchunks, nchunks, ni // nchunks


def _linear(xp, xs, w, layer, *, col0=0, n=None, res=None, name):
    mp, k = xp.shape
    ms = xs.shape[0]
    n = w.shape[2] if n is None else n
    tm, tn, kc, nchunks, spc = _linear_tiles(mp, k, n, col0)
    assert mp % tm == 0 and n % tn == 0
    in_specs = [pl.BlockSpec((tm, k), lambda j, i: (i, 0)),
                pl.BlockSpec((ms, k), lambda j, i: (0, 0)),
                pl.BlockSpec(memory_space=pl.ANY)]
    args = [xp, xs, w]
    blocks = (2 * (_nbytes((tm, k), BF16) + _nbytes((ms, k), xs.dtype)
                   + _nbytes((tm, tn), F32) + _nbytes((ms, tn), F32))
              + 2 * _nbytes((k, tn), BF16) + 2 * _nbytes((kc, tn), F32))
    if res is not None:
        in_specs += [pl.BlockSpec((tm, tn), lambda j, i: (i, j)),
                     pl.BlockSpec((ms, tn), lambda j, i: (0, j))]
        args += list(res)
        blocks += 2 * (_nbytes((tm, tn), F32) + _nbytes((ms, tn), F32))
    return pl.pallas_call(
        functools.partial(_linear_body, layer=layer, col0=col0, kc=kc, nchunks=nchunks, spc=spc,
                          has_res=res is not None),
        grid=(n // tn, mp // tm),
        in_specs=in_specs,
        out_specs=[pl.BlockSpec((tm, tn), lambda j, i: (i, j)),
                   pl.BlockSpec((ms, tn), lambda j, i: (0, j))],
        out_shape=[jax.ShapeDtypeStruct((mp, n), F32), jax.ShapeDtypeStruct((ms, n), F32)],
        scratch_shapes=[pltpu.VMEM((2, k, tn), BF16), pltpu.VMEM((2, kc, tn), F32),
                        pltpu.SemaphoreType.DMA((2,))],
        compiler_params=pltpu.CompilerParams(
            dimension_semantics=("arbitrary", "arbitrary"),
            vmem_limit_bytes=_vmem_limit(blocks)),
        name=name,
    )(*args)


FFN_STEP_ROWS = 1024
FFN_SUB_ROWS = 256


def _ffn_up_body(hp_ref, hs_ref, wa_ref, wg_ref, cw_ref, cb_ref, exts_ref,
                 actp_ref, acts_ref, tailp_ref, as_ref,
                 wab_ref, wgb_ref, ext_ref, *, tm, tlen, ts):
    i = pl.program_id(1)
    cw = cw_ref[...]
    cb = cb_ref[...]

    def act_fn(a2, a1, a0, b):
        conv = cb + cw[0:1, :] * a2 + cw[1:2, :] * a1 + cw[2:3, :] * a0
        return jax.nn.silu(conv) * b

    @pl.when(i == 0)
    def _new_column_tile():
        _cast_weight(wa_ref, wab_ref)
        _cast_weight(wg_ref, wgb_ref)
        hs = hs_ref[...].astype(BF16)
        a = jnp.dot(hs, wab_ref[...], preferred_element_type=F32)
        b = jnp.dot(hs, wgb_ref[...], preferred_element_type=F32)
        ms = a.shape[0]
        t = lax.broadcasted_iota(jnp.int32, a.shape, 0) % ts
        ext = exts_ref[...]
        a1 = jnp.where(t == 0, pltpu.roll(ext, ms - 1, axis=0), pltpu.roll(a, 1, axis=0))
        a2 = jnp.where(t < 2, ext, pltpu.roll(a, 2, axis=0))
        acts_ref[...] = act_fn(a2, a1, a, b).astype(acts_ref.dtype)
        as_ref[...] = a

    @pl.when((i * tm) % tlen == 0)
    def _sequence_start():
        ext_ref[...] = jnp.zeros(ext_ref.shape, F32)

    tail = ext_ref[...]
    for s in range(tm // FFN_SUB_ROWS):
        rows = slice(s * FFN_SUB_ROWS, (s + 1) * FFN_SUB_ROWS)
        hp = hp_ref[rows, :]
        a = jnp.dot(hp, wab_ref[...], preferred_element_type=F32)
        b = jnp.dot(hp, wgb_ref[...], preferred_element_type=F32)
        ext = jnp.concatenate([tail, a], axis=0)
        a1 = ext[7:7 + FFN_SUB_ROWS, :]
        a2 = ext[6:6 + FFN_SUB_ROWS, :]
        actp_ref[rows, :] = act_fn(a2, a1, a, b).astype(actp_ref.dtype)
        tail = a[FFN_SUB_ROWS - 8:FFN_SUB_ROWS, :]
    ext_ref[...] = tail

    @pl.when(((i + 1) * tm) % tlen == 0)
    def _sequence_end():
        n = (i * tm) // tlen
        tailp_ref[pl.ds(pl.multiple_of(n * 8, 8), 8), :] = tail


def _ffn_up(hp, hs, w_up, conv_w, conv_b, ext_s, layer, *, tlen, ts):
    mp, d = hp.shape
    ms = hs.shape[0]
    f = w_up.shape[2] // 2
    nseq = mp // tlen
    tm = min(FFN_STEP_ROWS, tlen)
    tf = 256
    assert mp % tm == 0 and tlen % tm == 0 and f % tf == 0 and tm % FFN_SUB_ROWS == 0
    nf = f // tf
    blocks = (2 * (_nbytes((tm, d), BF16) + _nbytes((ms, d), hs.dtype) + 2 * _nbytes((d, tf), F32)
                   + _nbytes((tm, tf), BF16) + 3 * _nbytes((ms, tf), F32) + _nbytes((nseq * 8, tf), F32))
              + 2 * _nbytes((d, tf), BF16) + _nbytes((8, tf), F32) + 5 * _nbytes((tm, tf), F32))
    return pl.pallas_call(
        functools.partial(_ffn_up_body, tm=tm, tlen=tlen, ts=ts),
        grid=(nf, mp // tm),
        in_specs=[pl.BlockSpec((tm, d), lambda j, i: (i, 0)),
                  pl.BlockSpec((ms, d), lambda j, i: (0, 0)),
                  pl.BlockSpec((None, d, tf), lambda j, i: (layer, 0, j)),
                  pl.BlockSpec((None, d, tf), lambda j, i: (layer, 0, j + nf)),
                  pl.BlockSpec((None, CONV_W, tf), lambda j, i: (layer, 0, j)),
                  pl.BlockSpec((None, 1, tf), lambda j, i: (layer, 0, j)),
                  pl.BlockSpec((ms, tf), lambda j, i: (0, j))],
        out_specs=[pl.BlockSpec((tm, tf), lambda j, i: (i, j)),
                   pl.BlockSpec((ms, tf), lambda j, i: (0, j)),
                   pl.BlockSpec((nseq * 8, tf), lambda j, i: (0, j)),
                   pl.BlockSpec((ms, tf), lambda j, i: (0, j))],
        out_shape=[jax.ShapeDtypeStruct((mp, f), BF16),
                   jax.ShapeDtypeStruct((ms, f), BF16),
                   jax.ShapeDtypeStruct((nseq * 8, f), F32),
                   jax.ShapeDtypeStruct((ms, f), F32)],
        scratch_shapes=[pltpu.VMEM((d, tf), BF16), pltpu.VMEM((d, tf), BF16),
                        pltpu.VMEM((8, tf), F32)],
        compiler_params=pltpu.CompilerParams(
            dimension_semantics=("arbitrary", "arbitrary"),
            vmem_limit_bytes=_vmem_limit(blocks)),
        name="ffn_up",
    )(hp, hs, w_up, w_up, conv_w, conv_b, ext_s)


def _level_table(c):
    t = np.arange(c)[:, None]
    s = np.arange(c)[None, :]
    x = np.maximum(t ^ s, 1)
    lev = np.floor(np.log2(x)).astype(np.int32)
    return np.where(t > s, lev, np.where(t == s, -1, -2)).astype(np.int32)


def _cumsum_table(c):
    return np.tril(np.ones((c, c), np.float32))


def _split3(x):
    x0 = x.astype(BF16)
    r1 = x - x0.astype(F32)
    x1 = r1.astype(BF16)
    x2 = (r1 - x1.astype(F32)).astype(BF16)
    return x0, x1, x2


def _block_boundary_rows(b, blk):
    c, w = b.shape
    half = blk // 2
    if blk >= 2 * V7X_SUBLANES:
        x = b.reshape(c // blk, blk, w)
        r = jnp.broadcast_to(x[:, half - 1:half, :], x.shape)
        return r.reshape(c, w)
    x = b.reshape(c // V7X_SUBLANES, V7X_SUBLANES, w)
    sub = lax.broadcasted_iota(jnp.int32, x.shape, 1)

    def row(i):
        return jnp.broadcast_to(x[:, i:i + 1, :], x.shape)

    if blk == 8:
        r = row(3)
    elif blk == 4:
        r = jnp.where(sub < 4, row(1), row(5))
    else:
        r = jnp.where(sub < 2, row(0), jnp.where(sub < 4, row(2), jnp.where(sub < 6, row(4), row(6))))
    return r.reshape(c, w)


_NT = (((1,), (1,)), ((), ()))
_TN = (((0,), (0,)), ((), ()))


def _gla_body(*refs, nseq, tlen, valid, has_s0):
    if has_s0:
        (q_ref, f_ref, v_ref, g_ref, lb_ref, gain_ref, lev_ref, tri_ref, s0_ref,
         o_ref, so_ref) = refs
    else:
        q_ref, f_ref, v_ref, g_ref, lb_ref, gain_ref, lev_ref, tri_ref, o_ref, so_ref = refs
    c = GLA_CHUNK
    rows_in = min(c, valid)
    nchunk = max(tlen // c, 1)
    lb = lb_ref[...]
    log_lb = jnp.log(lb)
    log1m_lb = jnp.log1p(-lb)
    one_m_lb = 1.0 - lb
    gain = gain_ref[...]
    row_id = lax.broadcasted_iota(jnp.int32, (c, HEAD), 0)

    def padded(ref, rows):
        x = ref[rows, :]
        if rows_in < c:
            x = jnp.concatenate([x, jnp.zeros((c - rows_in, HEAD), F32)], axis=0)
        return x

    def chunk(n, ci, st):
        row0 = pl.multiple_of(n * tlen + ci * rows_in, rows_in)
        rows = pl.ds(row0, rows_in)
        qr = padded(q_ref, rows)
        fz = padded(f_ref, rows)
        v = padded(v_ref, rows)
        e = jnp.exp(-jnp.abs(fz))
        one_p_e = 1.0 + e
        log_sig = jnp.minimum(fz, 0.0) - jnp.log(one_p_e)
        u = log1m_lb + log_sig
        log_f = jnp.maximum(log_lb, u) + jnp.log(1.0 + jnp.exp(-jnp.abs(log_lb - u)))
        k = one_m_lb * (jnp.where(fz > 0.0, e, 1.0) / one_p_e)
        if rows_in < c:
            log_f = jnp.where(row_id < rows_in, log_f, 0.0)
            k = jnp.where(row_id < rows_in, k, 0.0)
        q = jax.nn.silu(qr) * (HEAD ** -0.5)
        vb = v.astype(BF16)

        tri = tri_ref[...]
        f0, f1, f2 = _split3(log_f)
        b = (jnp.dot(tri, f0, preferred_element_type=F32)
             + jnp.dot(tri, f1, preferred_element_type=F32)
             + jnp.dot(tri, f2, preferred_element_type=F32))
        b = b * LOG2_E
        b_last = b[c - 1:c, :]

        lev = lev_ref[...]
        diag = jnp.sum(q * k, axis=-1, keepdims=True)
        att = jnp.where(lev == -1, diag, 0.0)
        for l in range(int(math.log2(c))):
            r = _block_boundary_rows(b, 2 << l)
            decay = jnp.exp2(-jnp.abs(b - r))
            qt = (q * decay).astype(BF16)
            kt = (k * decay).astype(BF16)
            s_l = lax.dot_general(qt, kt, _NT, preferred_element_type=F32)
            att = jnp.where(lev == l, s_l, att)
        o = jnp.dot(att.astype(BF16), vb, preferred_element_type=F32)
        o = o + lax.dot_general((q * jnp.exp2(b)).astype(BF16), st.astype(BF16), _NT,
                                preferred_element_type=F32)
        kd = (k * jnp.exp2(b_last - b)).astype(BF16)
        st_new = st * jnp.exp2(b_last) + lax.dot_general(vb, kd, _TN, preferred_element_type=F32)

        o = o * lax.rsqrt(jnp.mean(o * o, axis=-1, keepdims=True) + EPS) * gain
        g = padded(g_ref, rows)
        o = o * jax.nn.silu(g)
        o_ref[rows, :] = o[0:rows_in, :].astype(o_ref.dtype)
        return st_new

    def sequence(n, carry):
        if has_s0:
            st0 = s0_ref[n, 0].T
        else:
            st0 = jnp.zeros((HEAD, HEAD), F32)
        st = lax.fori_loop(0, nchunk, lambda ci, st: chunk(n, ci, st), st0,
                           unroll=min(GLA_CHUNK_UNROLL, nchunk))
        so_ref[n, 0] = st.T
        return carry

    lax.fori_loop(0, nseq, sequence, 0)


def _gla(proj, lb, gain, s0, *, nseq_total, tlen, out_dtype):
    m, d4 = proj.shape
    d = d4 // 4
    nh = d // HEAD
    nseq = 1 if tlen >= GLA_CHUNK else nseq_total
    assert tlen % GLA_CHUNK == 0 or tlen < GLA_CHUNK
    rows = nseq * tlen
    col = lambda off: pl.BlockSpec((rows, HEAD), lambda n, h: (n, h + off))
    vec = pl.BlockSpec((1, HEAD), lambda n, h: (0, h))
    const = pl.BlockSpec((GLA_CHUNK, GLA_CHUNK), lambda n, h: (0, 0))
    in_specs = [col(0), col(nh), col(2 * nh), col(3 * nh), vec, vec, const, const]
    args = [proj, proj, proj, proj, lb.reshape(1, d), gain.reshape(1, d),
            jnp.asarray(_level_table(GLA_CHUNK)), jnp.asarray(_cumsum_table(GLA_CHUNK), BF16)]
    state_spec = pl.BlockSpec((nseq, 1, HEAD, HEAD), lambda n, h: (n, h, 0, 0))
    if s0 is not None:
        in_specs.append(state_spec)
        args.append(s0)
    blocks = 2 * (4 * _nbytes((rows, HEAD), F32) + _nbytes((rows, HEAD), out_dtype)
                  + 2 * _nbytes((nseq, 1, HEAD, HEAD), F32)) + (4 << 20)
    return pl.pallas_call(
        functools.partial(_gla_body, nseq=nseq, tlen=tlen, valid=tlen, has_s0=s0 is not None),
        grid=(nseq_total // nseq, nh),
        in_specs=in_specs,
        out_specs=[pl.BlockSpec((rows, HEAD), lambda n, h: (n, h)), state_spec],
        out_shape=[jax.ShapeDtypeStruct((m, d), out_dtype),
                   jax.ShapeDtypeStruct((nseq_total, nh, HEAD, HEAD), F32)],
        compiler_params=pltpu.CompilerParams(
            dimension_semantics=("arbitrary", "arbitrary"),
            vmem_limit_bytes=_vmem_limit(blocks)),
        name="hgrn2_recurrence",
    )(*args)


def _t5_buckets(dist):
    max_exact = N_BUCKETS // 2
    n = np.asarray(dist, dtype=np.int64)
    large = max_exact + (np.log(np.maximum(n, 1) / max_exact)
                         / np.log(BUCKET_MAX_DIST / max_exact)
                         * (N_BUCKETS - max_exact)).astype(np.int64)
    large = np.minimum(large, N_BUCKETS - 1)
    return np.where(n < max_exact, n, large).astype(np.int32)


def _bucket_lookup(tab, buckets):
    flat = jnp.asarray(buckets.reshape(1, -1))
    onehot = (flat == lax.broadcasted_iota(jnp.int32, (N_BUCKETS, flat.shape[1]), 0)).astype(F32)
    out = jnp.einsum("bh,bx->hx", tab, onehot, precision=lax.Precision.HIGHEST)
    return out.reshape((tab.shape[1],) + buckets.shape)


def _prompt_bias_tiles(rel_bias, nh):
    bl = SUB_WINDOW
    off = np.arange(bl)[:, None] + bl - np.arange(2 * bl)[None, :]
    band = (off >= 0) & (off <= SUB_WINDOW)
    tiles = []
    for g, dil in enumerate(DILATIONS):
        buckets = _t5_buckets(np.clip(off, 0, SUB_WINDOW) * dil)
        tab = rel_bias[:, g * nh:(g + 1) * nh].astype(F32)
        tiles.append(jnp.where(band[None], _bucket_lookup(tab, buckets), NEG))
    return jnp.stack(tiles)


ATTN_BLOCK_UNROLL = 8


def _attn_prompt_body(q1_ref, q2_ref, q3_ref, k_ref, v_ref, bias_ref, o_ref,
                      acc_ref, lse_ref, k4_ref, v4_ref, q34_ref, *, tlen):
    bl = SUB_WINDOW
    scale = HEAD ** -0.5
    mid = DILATIONS[1]
    seg = tlen // mid

    for src_ref, dst_ref in ((k_ref, k4_ref), (v_ref, v4_ref), (q3_ref, q34_ref)):
        for r in range(mid):
            dst_ref[r * seg:(r + 1) * seg, :] = src_ref[pl.ds(r, seg, stride=mid), :]

    for g, (dil, q_ref) in enumerate(zip(DILATIONS, (q1_ref, q2_ref, q3_ref))):
        nblk = tlen // (dil * bl)
        with_prev = nblk > 1

        def block(it, carry, dil=dil, q_ref=q_ref, nblk=nblk, g=g, with_prev=with_prev):
            r = it // nblk
            bi = it % nblk
            start = r + dil * bl * bi
            qrows = pl.ds(start, bl, stride=dil) if dil > 1 else pl.ds(start, bl)
            if with_prev:
                nkey = 2 * bl
                bias = bias_ref[g, 0]
                bias_first = jnp.concatenate([bias[:, bl:2 * bl], jnp.full((bl, bl), NEG, F32)], axis=1)
                bias = jnp.where(bi == 0, bias_first, bias)
                back = jnp.where(bi > 0, bl, 0)
            else:
                nkey = bl
                bias = bias_ref[g, 0, :, bl:2 * bl]
                back = 0
            if dil == 1:
                krows = pl.ds(start - back, nkey)
                qb, k2, v2 = q_ref[qrows, :], k_ref[krows, :], v_ref[krows, :]
            elif dil == mid:
                krows = pl.ds(r * seg + bl * bi - back, nkey)
                qb, k2, v2 = q_ref[qrows, :], k4_ref[krows, :], v4_ref[krows, :]
            else:
                sub = dil // mid
                rows4 = pl.ds((r % mid) * seg + r // mid + sub * (bl * bi - back), nkey, stride=sub)
                q4rows = pl.ds((r % mid) * seg + r // mid + sub * bl * bi, bl, stride=sub)
                qb, k2, v2 = q34_ref[q4rows, :], k4_ref[rows4, :], v4_ref[rows4, :]
            qb, k2, v2 = qb.astype(BF16), k2.astype(BF16), v2.astype(BF16)
            s = lax.dot_general(qb, k2, _NT, preferred_element_type=F32) * scale + bias
            mx = jnp.max(s, axis=-1, keepdims=True)
            p = jnp.exp(s - mx)
            sm = jnp.sum(p, axis=-1, keepdims=True)
            acc = jnp.dot(p.astype(BF16), v2, preferred_element_type=F32)
            acc_ref[g, qrows, :] = acc / sm
            lse_ref[g, qrows, :] = jnp.broadcast_to(mx + jnp.log(sm), (bl, HEAD))
            return carry

        def residue_pair(it, carry, dil=dil, g=g):
            sub = dil // mid
            parts = []
            for r in (2 * it, 2 * it + 1):
                rows4 = pl.ds((r % mid) * seg + r // mid, bl, stride=sub)
                parts.append((q34_ref[rows4, :], k4_ref[rows4, :], v4_ref[rows4, :],
                              pl.ds(r, bl, stride=dil)))
            qb = jnp.concatenate([pt[0] for pt in parts], axis=0).astype(BF16)
            k2 = jnp.concatenate([pt[1] for pt in parts], axis=0).astype(BF16)
            v2 = jnp.concatenate([pt[2] for pt in parts], axis=0).astype(BF16)
            own = bias_ref[g, 0, :, bl:2 * bl]
            neg = jnp.full((bl, bl), NEG, F32)
            bias = jnp.concatenate([jnp.concatenate([own, neg], axis=1),
                                    jnp.concatenate([neg, own], axis=1)], axis=0)
            s = lax.dot_general(qb, k2, _NT, preferred_element_type=F32) * scale + bias
            mx = jnp.max(s, axis=-1, keepdims=True)
            p = jnp.exp(s - mx)
            sm = jnp.sum(p, axis=-1, keepdims=True)
            o = jnp.dot(p.astype(BF16), v2, preferred_element_type=F32) / sm
            lse = jnp.broadcast_to(mx + jnp.log(sm), (2 * bl, HEAD))
            for half, pt in enumerate(parts):
                acc_ref[g, pt[3], :] = o[half * bl:(half + 1) * bl, :]
                lse_ref[g, pt[3], :] = lse[half * bl:(half + 1) * bl, :]
            return carry

        if with_prev:
            lax.fori_loop(0, dil * nblk, block, 0, unroll=ATTN_BLOCK_UNROLL)
        else:
            lax.fori_loop(0, dil // 2, residue_pair, 0, unroll=ATTN_BLOCK_UNROLL // 2)

    def merge(ci, carry):
        rows = pl.ds(pl.multiple_of(ci * bl, bl), bl)
        l1, l2, l3 = lse_ref[0, rows, :], lse_ref[1, rows, :], lse_ref[2, rows, :]
        mm = jnp.maximum(jnp.maximum(l1, l2), l3)
        w1, w2, w3 = jnp.exp(l1 - mm), jnp.exp(l2 - mm), jnp.exp(l3 - mm)
        num = w1 * acc_ref[0, rows, :] + w2 * acc_ref[1, rows, :] + w3 * acc_ref[2, rows, :]
        o_ref[rows, :] = (num / (w1 + w2 + w3)).astype(o_ref.dtype)
        return carry

    lax.fori_loop(0, tlen // bl, merge, 0, unroll=2)


def _attn_prompt(qproj, k, v, bias_tiles, *, nseq, tlen):
    m, d3 = qproj.shape
    d = d3 // 3
    nh = d // HEAD
    qspec = lambda g: pl.BlockSpec((tlen, HEAD), lambda n, h: (n, g * nh + h))
    kvspec = pl.BlockSpec((tlen, HEAD), lambda n, h: (n, h))
    blocks = (2 * (5 * _nbytes((tlen, HEAD), F32) + _nbytes((3, 1, SUB_WINDOW, 2 * SUB_WINDOW), F32)
                   + _nbytes((tlen, HEAD), BF16))
              + 9 * _nbytes((tlen, HEAD), F32))
    return pl.pallas_call(
        functools.partial(_attn_prompt_body, tlen=tlen),
        grid=(nseq, nh),
        in_specs=[qspec(0), qspec(1), qspec(2), kvspec, kvspec,
                  pl.BlockSpec((3, 1, SUB_WINDOW, 2 * SUB_WINDOW), lambda n, h: (0, h, 0, 0))],
        out_specs=pl.BlockSpec((tlen, HEAD), lambda n, h: (n, h)),
        out_shape=jax.ShapeDtypeStruct((m, d), BF16),
        scratch_shapes=[pltpu.VMEM((3, tlen, HEAD), F32), pltpu.VMEM((3, tlen, HEAD), F32),
                        pltpu.VMEM((tlen, HEAD), F32), pltpu.VMEM((tlen, HEAD), F32),
                        pltpu.VMEM((tlen, HEAD), F32)],
        compiler_params=pltpu.CompilerParams(
            dimension_semantics=("arbitrary", "arbitrary"),
            vmem_limit_bytes=_vmem_limit(blocks)),
        name="dilated_attention_prompt",
    )(qproj, qproj, qproj, k, v, bias_tiles)


SAMPLE_HEADS_PER_STEP = V7X_SUBLANES


def _sample_key_rows(p_len, ts):
    period = max(DILATIONS)
    assert p_len % period == 0 and ts < period
    n_i = p_len // period
    tail_i0 = (p_len - DILATIONS[1] * SUB_WINDOW) // period
    assert tail_i0 >= 0 and DILATIONS[0] * SUB_WINDOW <= DILATIONS[1] * SUB_WINDOW
    main = (np.arange(n_i)[:, None] * period + np.arange(ts)[None, :]).reshape(-1)
    tail = (np.arange(tail_i0, n_i)[:, None] * period + np.arange(ts, period)[None, :]).reshape(-1)
    rows = np.concatenate([main, tail, p_len + np.arange(ts)])
    nkeys = -(-rows.size // V7X_LANES) * V7X_LANES
    return np.concatenate([rows, np.full(nkeys - rows.size, -1)]), n_i, n_i - tail_i0


def _sample_bias_table(rel_bias, nh, p_len, ts, key_rows):
    t = np.arange(ts)[:, None]
    row = key_rows[None, :]
    dist = p_len + t - row
    out = []
    for g, dil in enumerate(DILATIONS):
        j = dist // dil
        ok = (row >= 0) & (dist >= 0) & (dist % dil == 0) & (j <= SUB_WINDOW)
        buckets = _t5_buckets(np.clip(j, 0, SUB_WINDOW) * dil)
        tab = rel_bias[:, g * nh:(g + 1) * nh].astype(F32)
        out.append(jnp.where(ok[None], _bucket_lookup(tab, buckets), NEG))
    return jnp.concatenate(out, axis=1)


def _attn_sample_body(q1_ref, q2_ref, q3_ref, km_ref, kt_ref, vm_ref, vt_ref, kn_ref, vn_ref,
                      bias_ref, o_ref, kall_ref, vall_ref, *, ts):
    nkeys = kall_ref.shape[0]
    n_main = km_ref.shape[0] * km_ref.shape[1]
    n_tail = kt_ref.shape[0] * kt_ref.shape[1]
    n_used = n_main + n_tail + ts
    scale = HEAD ** -0.5
    kall_ref[n_used:nkeys, :] = jnp.zeros((nkeys - n_used, HEAD), F32)
    vall_ref[n_used:nkeys, :] = jnp.zeros((nkeys - n_used, HEAD), F32)
    for hh in range(SAMPLE_HEADS_PER_STEP):
        cols = slice(hh * HEAD, (hh + 1) * HEAD)
        for src_main, src_tail, src_new, dst in ((km_ref, kt_ref, kn_ref, kall_ref),
                                                 (vm_ref, vt_ref, vn_ref, vall_ref)):
            flat_main = src_main.reshape(n_main * SAMPLE_HEADS_PER_STEP, HEAD)
            flat_tail = src_tail.reshape(n_tail * SAMPLE_HEADS_PER_STEP, HEAD)
            dst[0:n_main, :] = flat_main[pl.ds(hh, n_main, stride=SAMPLE_HEADS_PER_STEP), :]
            dst[n_main:n_main + n_tail, :] = flat_tail[pl.ds(hh, n_tail, stride=SAMPLE_HEADS_PER_STEP), :]
            dst[n_main + n_tail:n_used, :] = src_new[:, cols]
        kb = kall_ref[...].astype(BF16)
        vb = vall_ref[...].astype(BF16)
        bias = bias_ref[hh]
        s = []
        for g, q_ref in enumerate((q1_ref, q2_ref, q3_ref)):
            sg = lax.dot_general(q_ref[:, cols].astype(BF16), kb, _NT, preferred_element_type=F32)
            s.append(sg * scale + bias[g * ts:(g + 1) * ts, :])
        mx = [jnp.max(x, axis=-1, keepdims=True) for x in s]
        mm = jnp.maximum(jnp.maximum(mx[0], mx[1]), mx[2])
        p = jnp.exp(s[0] - mm) + jnp.exp(s[1] - mm) + jnp.exp(s[2] - mm)
        den = jnp.sum(p, axis=-1, keepdims=True)
        o = jnp.dot(p.astype(BF16), vb, preferred_element_type=F32)
        o_ref[:, cols] = (o / den).astype(o_ref.dtype)


def _attn_sample(qproj, k_new, v_new, cache_k, cache_v, bias_table, n_i, n_tail, *, nseq, ts):
    m, d3 = qproj.shape
    d = d3 // 3
    nh = d // HEAD
    hps = SAMPLE_HEADS_PER_STEP
    assert nh % hps == 0 and n_i % n_tail == 0
    p_len = cache_k.shape[1]
    period = p_len // n_i
    nkeys = bias_table.shape[-1]
    ngrp = nh // hps
    wide = hps * HEAD
    qspec = lambda g: pl.BlockSpec((ts, wide), lambda n, h: (n, g * ngrp + h))
    newspec = pl.BlockSpec((ts, wide), lambda n, h: (n, h))
    mainspec = pl.BlockSpec((None, n_i, ts, hps, HEAD), lambda n, h: (n, 0, 0, h, 0))
    tailspec = pl.BlockSpec((None, n_tail, period - ts, hps, HEAD),
                            lambda n, h: (n, n_i // n_tail - 1, 1, h, 0))
    assert period - ts == ts
    cache_bytes = _nbytes((n_i + n_tail, ts, hps, HEAD), F32)
    blocks = (2 * (2 * cache_bytes + _nbytes((hps, 3 * ts, nkeys), F32))
              + 2 * _nbytes((nkeys, HEAD), F32) + 2 * _nbytes((nkeys, HEAD), BF16)
              + 8 * _nbytes((3 * ts, nkeys), F32))
    k5 = cache_k.reshape(nseq, n_i, period, nh, HEAD)
    v5 = cache_v.reshape(nseq, n_i, period, nh, HEAD)
    return pl.pallas_call(
        functools.partial(_attn_sample_body, ts=ts),
        grid=(nseq, ngrp),
        in_specs=[qspec(0), qspec(1), qspec(2), mainspec, tailspec, mainspec, tailspec,
                  newspec, newspec,
                  pl.BlockSpec((hps, 3 * ts, nkeys), lambda n, h: (h, 0, 0))],
        out_specs=pl.BlockSpec((ts, wide), lambda n, h: (n, h)),
        out_shape=jax.ShapeDtypeStruct((m, d), F32),
        scratch_shapes=[pltpu.VMEM((nkeys, HEAD), F32), pltpu.VMEM((nkeys, HEAD), F32)],
        compiler_params=pltpu.CompilerParams(
            dimension_semantics=("arbitrary", "arbitrary"),
            vmem_limit_bytes=_vmem_limit(blocks)),
        name="dilated_attention_sample",
    )(qproj, qproj, qproj, k5, k5, v5, v5, k_new, v_new, bias_table)


def _lower_bounds(lb_raw):
    p = jax.nn.softmax(lb_raw.astype(F32), axis=0)
    c = jnp.cumsum(p, axis=0)
    return c - c[0:1]


def kernel(x_prompt, x_sample, state_hgrn, state_ffn_conv, cache_k_win, cache_v_win, a_w_in, a_lb_raw, a_out_gain, a_w_out, kv_norm, w_kv, b_w_q, b_w_out, rel_bias, ffn_w_up, ffn_conv_w, ffn_conv_b, ffn_w_down, norm_mix, norm_ffn, norm_final):
    nb_p, t_p, d = x_prompt.shape
    nb_s, t_s, _ = x_sample.shape
    depth = ffn_w_up.shape[0]
    n_a = a_w_in.shape[0]
    nh = d // HEAD
    f = ffn_w_down.shape[1]
    p_len = cache_k_win.shape[1]
    assert t_p % (SUB_WINDOW * max(DILATIONS)) == 0 and t_s == V7X_SUBLANES

    xp = x_prompt.reshape(nb_p * t_p, d)
    xs = x_sample.reshape(nb_s * t_s, d)
    lbs = _lower_bounds(a_lb_raw)
    key_rows, n_i, n_tail = _sample_key_rows(p_len, t_s)
    bias_p = _prompt_bias_tiles(rel_bias, nh)
    bias_s = _sample_bias_table(rel_bias, nh, p_len, t_s, key_rows)

    hgrn_p, hgrn_s, conv_p, conv_s = [], [], [], []
    kp = vp = ks = vs = None
    w_kv3 = w_kv.reshape(1, d, 2 * d)
    conv_b3 = ffn_conv_b.reshape(depth, 1, f)
    for l in range(depth):
        hp = _rmsnorm(xp, norm_mix[l], BF16)
        hs = _rmsnorm(xs, norm_mix[l], BF16)
        if l < n_a:
            pp, ps = _linear(hp, hs, a_w_in, l, name="hgrn2_in_proj")
            op, sp = _gla(pp, lbs[l], a_out_gain[l], None, nseq_total=nb_p, tlen=t_p, out_dtype=BF16)
            os_, ss = _gla(ps, lbs[l], a_out_gain[l], state_hgrn[l], nseq_total=nb_s, tlen=t_s,
                           out_dtype=F32)
            hgrn_p.append(sp)
            hgrn_s.append(ss)
            w_o, l_o = a_w_out, l
        else:
            if l == n_a:
                kvp = _rmsnorm(xp, kv_norm, BF16)
                kvs = _rmsnorm(xs, kv_norm, BF16)
                kp, ks = _linear(kvp, kvs, w_kv3, 0, col0=0, n=d, name="k_proj")
                vp, vs = _linear(kvp, kvs, w_kv3, 0, col0=d, n=d, name="v_proj")
            jb = l - n_a
            qp, qs = _linear(hp, hs, b_w_q, jb, name="q_proj")
            op = _attn_prompt(qp, kp, vp, bias_p, nseq=nb_p, tlen=t_p)
            os_ = _attn_sample(qs, ks, vs, cache_k_win, cache_v_win, bias_s, n_i, n_tail,
                               nseq=nb_s, ts=t_s)
            w_o, l_o = b_w_out, jb
        xp, xs = _linear(op, os_, w_o, l_o, res=(xp, xs), name="mixer_out_proj")

        hp = _rmsnorm(xp, norm_ffn[l], BF16)
        hs = _rmsnorm(xs, norm_ffn[l], BF16)
        prev = state_ffn_conv[l]
        ext_s = jnp.pad(prev, ((0, 0), (0, t_s - (CONV_W - 1)), (0, 0))).reshape(nb_s * t_s, f)
        actp, acts, tailp, a_s = _ffn_up(hp, hs, ffn_w_up, ffn_conv_w, conv_b3, ext_s, l,
                                         tlen=t_p, ts=t_s)
        conv_p.append(tailp.reshape(nb_p, 8, f)[:, 8 - (CONV_W - 1):, :])
        conv_s.append(a_s.reshape(nb_s, t_s, f)[:, t_s - (CONV_W - 1):, :])
        xp, xs = _linear(actp, acts, ffn_w_down, l, res=(xp, xs), name="ffn_down_proj")

    yp = _rmsnorm(xp, norm_final, F32).reshape(nb_p, t_p, d)
    ys = _rmsnorm(xs, norm_final, F32).reshape(nb_s, t_s, d)
    keep = min(BUCKET_MAX_DIST, t_p)
    k_win = kp.reshape(nb_p, t_p, nh, HEAD)[:, t_p - keep:]
    v_win = vp.reshape(nb_p, t_p, nh, HEAD)[:, t_p - keep:]
    return (yp, ys, jnp.stack(hgrn_p), jnp.stack(hgrn_s), jnp.stack(conv_p), jnp.stack(conv_s),
            k_win, v_win, ks.reshape(nb_s, t_s, nh, HEAD), vs.reshape(nb_s, t_s, nh, HEAD))
```

```python
import functools
import math

import numpy as np
import jax
import jax.numpy as jnp
from jax import lax
from jax.experimental import pallas as pl
from jax.experimental.pallas import tpu as pltpu

F32 = jnp.float32
BF16 = jnp.bfloat16

V7X_VMEM_BYTES = 64 * 1024 * 1024
V7X_LANES = 128
V7X_SUBLANES = 8
COMPILER_SCRATCH_BYTES = 12 * 1024 * 1024

HEAD = 128
GLA_CHUNK = 128
GLA_SEQ_UNROLL = 8
GLA_CHUNK_UNROLL = 8
SUB_WINDOW = 128
DILATIONS = (1, 4, 16)
N_BUCKETS = 32
BUCKET_MAX_DIST = 2048
CONV_W = 3
EPS = 1e-6
NEG = -1e30
LOG2_E = 1.4426950408889634


def _vmem_limit(block_bytes):
    return int(min(block_bytes + COMPILER_SCRATCH_BYTES, V7X_VMEM_BYTES - (2 << 20)))


def _nbytes(shape, dtype):
    return int(np.prod(shape)) * jnp.dtype(dtype).itemsize


def _rmsnorm_body(x_ref, g_ref, o_ref):
    x = x_ref[...]
    ms = jnp.mean(x * x, axis=-1, keepdims=True)
    o_ref[...] = (x * lax.rsqrt(ms + EPS) * g_ref[...]).astype(o_ref.dtype)


def _rmsnorm(x, g, out_dtype):
    m, d = x.shape
    tm = min(m, 256)
    assert m % tm == 0
    blocks = 2 * (_nbytes((tm, d), F32) + _nbytes((tm, d), out_dtype))
    return pl.pallas_call(
        _rmsnorm_body,
        grid=(m // tm,),
        in_specs=[pl.BlockSpec((tm, d), lambda i: (i, 0)),
                  pl.BlockSpec((1, d), lambda i: (0, 0))],
        out_specs=pl.BlockSpec((tm, d), lambda i: (i, 0)),
        out_shape=jax.ShapeDtypeStruct((m, d), out_dtype),
        compiler_params=pltpu.CompilerParams(vmem_limit_bytes=_vmem_limit(blocks)),
        name="rmsnorm",
    )(x, g.reshape(1, d))


W_CAST_ROWS = 256


def _cast_weight(w_ref, wb_ref, col0=0):
    k, width = w_ref.shape
    assert k % W_CAST_ROWS == 0

    def step(c, carry):
        rows = pl.ds(pl.multiple_of(c * W_CAST_ROWS, W_CAST_ROWS), W_CAST_ROWS)
        wb_ref[rows, col0:col0 + width] = w_ref[rows, :].astype(BF16)
        return carry

    lax.fori_loop(0, k // W_CAST_ROWS, step, 0)


def _linear_body(*refs, layer, col0, kc, nchunks, spc, has_res):
    if has_res:
        xp_ref, xs_ref, w_hbm, rp_ref, rs_ref, yp_ref, ys_ref, wb_ref, stg_ref, sem = refs
    else:
        xp_ref, xs_ref, w_hbm, yp_ref, ys_ref, wb_ref, stg_ref, sem = refs
    j = pl.program_id(0)
    i = pl.program_id(1)
    nj = pl.num_programs(0)
    tn = wb_ref.shape[2]
    cur = j % 2

    def chunk_copy(jt, c):
        rows = pl.ds(pl.multiple_of(c * kc, kc), kc)
        cols = pl.ds(pl.multiple_of(col0 + jt * tn, tn), tn)
        return pltpu.make_async_copy(w_hbm.at[layer, rows, cols], stg_ref.at[c % 2], sem.at[c % 2])

    def land(jt, c, wslot):
        chunk_copy(jt, c).wait()
        rows = pl.ds(pl.multiple_of(c * kc, kc), kc)
        wb_ref[wslot, rows, :] = stg_ref[c % 2].astype(BF16)

    @pl.when(jnp.logical_and(j == 0, i == 0))
    def _first_tile():
        chunk_copy(0, 0).start()

        def step(c, carry):
            @pl.when(c + 1 < nchunks)
            def _():
                chunk_copy(0, c + 1).start()
            land(0, c, 0)
            return carry

        lax.fori_loop(0, nchunks, step, 0)

    @pl.when(jnp.logical_and(j > 0, i == 0))
    def _finish_prefetched_tile():
        land(j, nchunks - 1, cur)

    @pl.when(jnp.logical_and(j + 1 < nj, i % spc == 0))
    def _prefetch_next_tile():
        c = i // spc
        chunk_copy(j + 1, c).start()

        @pl.when(c > 0)
        def _():
            land(j + 1, c - 1, 1 - cur)

    @pl.when(i == 0)
    def _sample_rows():
        ys = jnp.dot(xs_ref[...].astype(BF16), wb_ref[cur], preferred_element_type=F32)
        if has_res:
            ys = ys + rs_ref[...]
        ys_ref[...] = ys

    yp = jnp.dot(xp_ref[...], wb_ref[cur], preferred_element_type=F32)
    if has_res:
        yp = yp + rp_ref[...]
    yp_ref[...] = yp


BF16_SUBLANES = 16
MAX_W_CHUNKS = 16


def _linear_tiles(mp, k, n, col0):
    tm, tn_max = (512, 1024) if k <= 4096 else (256, 512)
    tm = min(tm, mp)
    tn = next(t for t in (tn_max, tn_max // 2, tn_max // 4, V7X_LANES)
              if n % t == 0 and col0 % t == 0)
    ni = mp // tm
    nchunks = next(c for c in range(min(MAX_W_CHUNKS, ni), 0, -1)
                   if ni % c == 0 and k % (c * BF16_SUBLANES) == 0)
    return tm, tn, k // nchunks, nchunks, ni // nchunks


def _linear(xp, xs, w, layer, *, col0=0, n=None, res=None, name):
    mp, k = xp.shape
    ms = xs.shape[0]
    n = w.shape[2] if n is None else n
    tm, tn, kc, nchunks, spc = _linear_tiles(mp, k, n, col0)
    assert mp % tm == 0 and n % tn == 0
    in_specs = [pl.BlockSpec((tm, k), lambda j, i: (i, 0)),
                pl.BlockSpec((ms, k), lambda j, i: (0, 0)),
                pl.BlockSpec(memory_space=pl.ANY)]
    args = [xp, xs, w]
    blocks = (2 * (_nbytes((tm, k), BF16) + _nbytes((ms, k), xs.dtype)
                   + _nbytes((tm, tn), F32) + _nbytes((ms, tn), F32))
              + 2 * _nbytes((k, tn), BF16) + 2 * _nbytes((kc, tn), F32))
    if res is not None:
        in_specs += [pl.BlockSpec((tm, tn), lambda j, i: (i, j)),
                     pl.BlockSpec((ms, tn), lambda j, i: (0, j))]
        args += list(res)
        blocks += 2 * (_nbytes((tm, tn), F32) + _nbytes((ms, tn), F32))
    return pl.pallas_call(
        functools.partial(_linear_body, layer=layer, col0=col0, kc=kc, nchunks=nchunks, spc=spc,
                          has_res=res is not None),
        grid=(n // tn, mp // tm),
        in_specs=in_specs,
        out_specs=[pl.BlockSpec((tm, tn), lambda j, i: (i, j)),
                   pl.BlockSpec((ms, tn), lambda j, i: (0, j))],
        out_shape=[jax.ShapeDtypeStruct((mp, n), F32), jax.ShapeDtypeStruct((ms, n), F32)],
        scratch_shapes=[pltpu.VMEM((2, k, tn), BF16), pltpu.VMEM((2, kc, tn), F32),
                        pltpu.SemaphoreType.DMA((2,))],
        compiler_params=pltpu.CompilerParams(
            dimension_semantics=("arbitrary", "arbitrary"),
            vmem_limit_bytes=_vmem_limit(blocks)),
        name=name,
    )(*args)


FFN_STEP_ROWS = 1024
FFN_SUB_ROWS = 256


def _ffn_up_body(hp_ref, hs_ref, wa_ref, wg_ref, cw_ref, cb_ref, exts_ref,
                 actp_ref, acts_ref, tailp_ref, as_ref,
                 wcat_ref, ext_ref, *, tm, tlen, ts):
    i = pl.program_id(1)
    tf = wa_ref.shape[1]
    cw = cw_ref[...]
    cb = cb_ref[...]

    def act_fn(a2, a1, a0, b):
        conv = cb + cw[0:1, :] * a2 + cw[1:2, :] * a1 + cw[2:3, :] * a0
        return jax.nn.silu(conv) * b

    @pl.when(i == 0)
    def _new_column_tile():
        _cast_weight(wa_ref, wcat_ref, 0)
        _cast_weight(wg_ref, wcat_ref, tf)
        ab = jnp.dot(hs_ref[...].astype(BF16), wcat_ref[...], preferred_element_type=F32)
        a, b = ab[:, 0:tf], ab[:, tf:2 * tf]
        ms = a.shape[0]
        t = lax.broadcasted_iota(jnp.int32, a.shape, 0) % ts
        ext = exts_ref[...]
        a1 = jnp.where(t == 0, pltpu.roll(ext, ms - 1, axis=0), pltpu.roll(a, 1, axis=0))
        a2 = jnp.where(t < 2, ext, pltpu.roll(a, 2, axis=0))
        acts_ref[...] = act_fn(a2, a1, a, b).astype(acts_ref.dtype)
        as_ref[...] = a

    @pl.when((i * tm) % tlen == 0)
    def _sequence_start():
        ext_ref[...] = jnp.zeros(ext_ref.shape, F32)

    tail = ext_ref[...]
    for s in range(tm // FFN_SUB_ROWS):
        rows = slice(s * FFN_SUB_ROWS, (s + 1) * FFN_SUB_ROWS)
        hp = hp_ref[rows, :]
        ab = jnp.dot(hp, wcat_ref[...], preferred_element_type=F32)
        a, b = ab[:, 0:tf], ab[:, tf:2 * tf]
        ext = jnp.concatenate([tail, a], axis=0)
        a1 = ext[7:7 + FFN_SUB_ROWS, :]
        a2 = ext[6:6 + FFN_SUB_ROWS, :]
        actp_ref[rows, :] = act_fn(a2, a1, a, b).astype(actp_ref.dtype)
        tail = a[FFN_SUB_ROWS - 8:FFN_SUB_ROWS, :]
    ext_ref[...] = tail

    @pl.when(((i + 1) * tm) % tlen == 0)
    def _sequence_end():
        n = (i * tm) // tlen
        tailp_ref[pl.ds(pl.multiple_of(n * 8, 8), 8), :] = tail


def _ffn_up(hp, hs, w_up, conv_w, conv_b, ext_s, layer, *, tlen, ts):
    mp, d = hp.shape
    ms = hs.shape[0]
    f = w_up.shape[2] // 2
    nseq = mp // tlen
    tm = min(FFN_STEP_ROWS, tlen)
    tf = 256
    assert mp % tm == 0 and tlen % tm == 0 and f % tf == 0 and tm % FFN_SUB_ROWS == 0
    nf = f // tf
    blocks = (2 * (_nbytes((tm, d), BF16) + _nbytes((ms, d), hs.dtype) + 2 * _nbytes((d, tf), F32)
                   + _nbytes((tm, tf), BF16) + 3 * _nbytes((ms, tf), F32) + _nbytes((nseq * 8, tf), F32))
              + 2 * _nbytes((d, tf), BF16) + _nbytes((8, tf), F32) + 5 * _nbytes((tm, tf), F32))
    return pl.pallas_call(
        functools.partial(_ffn_up_body, tm=tm, tlen=tlen, ts=ts),
        grid=(nf, mp // tm),
        in_specs=[pl.BlockSpec((tm, d), lambda j, i: (i, 0)),
                  pl.BlockSpec((ms, d), lambda j, i: (0, 0)),
                  pl.BlockSpec((None, d, tf), lambda j, i: (layer, 0, j)),
                  pl.BlockSpec((None, d, tf), lambda j, i: (layer, 0, j + nf)),
                  pl.BlockSpec((None, CONV_W, tf), lambda j, i: (layer, 0, j)),
                  pl.BlockSpec((None, 1, tf), lambda j, i: (layer, 0, j)),
                  pl.BlockSpec((ms, tf), lambda j, i: (0, j))],
        out_specs=[pl.BlockSpec((tm, tf), lambda j, i: (i, j)),
                   pl.BlockSpec((ms, tf), lambda j, i: (0, j)),
                   pl.BlockSpec((nseq * 8, tf), lambda j, i: (0, j)),
                   pl.BlockSpec((ms, tf), lambda j, i: (0, j))],
        out_shape=[jax.ShapeDtypeStruct((mp, f), BF16),
                   jax.ShapeDtypeStruct((ms, f), BF16),
                   jax.ShapeDtypeStruct((nseq * 8, f), F32),
                   jax.ShapeDtypeStruct((ms, f), F32)],
        scratch_shapes=[pltpu.VMEM((d, 2 * tf), BF16), pltpu.VMEM((8, tf), F32)],
        compiler_params=pltpu.CompilerParams(
            dimension_semantics=("arbitrary", "arbitrary"),
            vmem_limit_bytes=_vmem_limit(blocks)),
        name="ffn_up",
    )(hp, hs, w_up, w_up, conv_w, conv_b, ext_s)


def _level_table(c):
    t = np.arange(c)[:, None]
    s = np.arange(c)[None, :]
    x = np.maximum(t ^ s, 1)
    lev = np.floor(np.log2(x)).astype(np.int32)
    return np.where(t > s, lev, np.where(t == s, -1, -2)).astype(np.int32)


def _cumsum_table(c):
    return np.tril(np.ones((c, c), np.float32))


def _split3(x):
    x0 = x.astype(BF16)
    r1 = x - x0.astype(F32)
    x1 = r1.astype(BF16)
    x2 = (r1 - x1.astype(F32)).astype(BF16)
    return x0, x1, x2


def _block_boundary_rows(b, blk):
    c, w = b.shape
    half = blk // 2
    if blk >= 2 * V7X_SUBLANES:
        x = b.reshape(c // blk, blk, w)
        r = jnp.broadcast_to(x[:, half - 1:half, :], x.shape)
        return r.reshape(c, w)
    x = b.reshape(c // V7X_SUBLANES, V7X_SUBLANES, w)
    sub = lax.broadcasted_iota(jnp.int32, x.shape, 1)

    def row(i):
        return jnp.broadcast_to(x[:, i:i + 1, :], x.shape)

    if blk == 8:
        r = row(3)
    elif blk == 4:
        r = jnp.where(sub < 4, row(1), row(5))
    else:
        r = jnp.where(sub < 2, row(0), jnp.where(sub < 4, row(2), jnp.where(sub < 6, row(4), row(6))))
    return r.reshape(c, w)


_NT = (((1,), (1,)), ((), ()))
_TN = (((0,), (0,)), ((), ()))


def _gla_body(*refs, nseq, tlen, valid, has_s0):
    if has_s0:
        (q_ref, f_ref, v_ref, g_ref, lb_ref, gain_ref, lev_ref, tri_ref, s0_ref,
         o_ref, so_ref) = refs
    else:
        q_ref, f_ref, v_ref, g_ref, lb_ref, gain_ref, lev_ref, tri_ref, o_ref, so_ref = refs
    c = GLA_CHUNK
    rows_in = min(c, valid)
    nchunk = max(tlen // c, 1)
    lb = lb_ref[...]
    log_lb = jnp.log(lb)
    log1m_lb = jnp.log1p(-lb)
    one_m_lb = 1.0 - lb
    gain = gain_ref[...]
    n_levels = int(math.log2(rows_in))
    assert 1 << n_levels == rows_in

    def pad_rows(x):
        if rows_in == c:
            return x
        return jnp.concatenate([x, jnp.zeros((c - rows_in, x.shape[1]), x.dtype)], axis=0)

    def chunk(n, ci, st):
        row0 = pl.multiple_of(n * tlen + ci * rows_in, rows_in)
        rows = pl.ds(row0, rows_in)
        qr, fz, v = q_ref[rows, :], f_ref[rows, :], v_ref[rows, :]
        e = jnp.exp(-jnp.abs(fz))
        one_p_e = 1.0 + e
        log_sig = jnp.minimum(fz, 0.0) - jnp.log(one_p_e)
        u = log1m_lb + log_sig
        log_f = jnp.maximum(log_lb, u) + jnp.log(1.0 + jnp.exp(-jnp.abs(log_lb - u)))
        k = one_m_lb * (jnp.where(fz > 0.0, e, 1.0) / one_p_e)
        q = jax.nn.silu(qr) * (HEAD ** -0.5)
        vb = pad_rows(v.astype(BF16))

        tri = tri_ref[...]
        b = sum(jnp.dot(tri, pad_rows(part), preferred_element_type=F32) for part in _split3(log_f))
        b = b * LOG2_E
        b_last = b[c - 1:c, :]
        b = b[0:rows_in, :]

        lev = lev_ref[...]
        diag = pad_rows(jnp.sum(q * k, axis=-1, keepdims=True))
        att = jnp.where(lev == -1, diag, 0.0)
        for l in range(n_levels):
            r = _block_boundary_rows(b, 2 << l)
            decay = jnp.exp2(-jnp.abs(b - r))
            qt = pad_rows((q * decay).astype(BF16))
            kt = pad_rows((k * decay).astype(BF16))
            s_l = lax.dot_general(qt, kt, _NT, preferred_element_type=F32)
            att = jnp.where(lev == l, s_l, att)
        o = jnp.dot(att.astype(BF16), vb, preferred_element_type=F32)
        o = o + lax.dot_general(pad_rows((q * jnp.exp2(b)).astype(BF16)), st.astype(BF16), _NT,
                                preferred_element_type=F32)
        kd = pad_rows((k * jnp.exp2(b_last - b)).astype(BF16))
        st_new = st * jnp.exp2(b_last) + lax.dot_general(vb, kd, _TN, preferred_element_type=F32)

        o = o[0:rows_in, :]
        o = o * lax.rsqrt(jnp.mean(o * o, axis=-1, keepdims=True) + EPS) * gain
        o = o * jax.nn.silu(g_ref[rows, :])
        o_ref[rows, :] = o.astype(o_ref.dtype)
        return st_new

    def sequence(n, carry):
        if has_s0:
            st0 = s0_ref[n, 0].T
        else:
            st0 = jnp.zeros((HEAD, HEAD), F32)
        st = lax.fori_loop(0, nchunk, lambda ci, st: chunk(n, ci, st), st0,
                           unroll=min(GLA_CHUNK_UNROLL, nchunk))
        so_ref[n, 0] = st.T
        return carry

    lax.fori_loop(0, nseq, sequence, 0, unroll=min(nseq, GLA_SEQ_UNROLL))


def _gla(proj, lb, gain, s0, *, nseq_total, tlen, out_dtype):
    m, d4 = proj.shape
    d = d4 // 4
    nh = d // HEAD
    nseq = 1 if tlen >= GLA_CHUNK else nseq_total
    assert tlen % GLA_CHUNK == 0 or tlen < GLA_CHUNK
    rows = nseq * tlen
    col = lambda off: pl.BlockSpec((rows, HEAD), lambda n, h: (n, h + off))
    vec = pl.BlockSpec((1, HEAD), lambda n, h: (0, h))
    const = pl.BlockSpec((GLA_CHUNK, GLA_CHUNK), lambda n, h: (0, 0))
    in_specs = [col(0), col(nh), col(2 * nh), col(3 * nh), vec, vec, const, const]
    args = [proj, proj, proj, proj, lb.reshape(1, d), gain.reshape(1, d),
            jnp.asarray(_level_table(GLA_CHUNK)), jnp.asarray(_cumsum_table(GLA_CHUNK), BF16)]
    state_spec = pl.BlockSpec((nseq, 1, HEAD, HEAD), lambda n, h: (n, h, 0, 0))
    if s0 is not None:
        in_specs.append(state_spec)
        args.append(s0)
    blocks = 2 * (4 * _nbytes((rows, HEAD), F32) + _nbytes((rows, HEAD), out_dtype)
                  + 2 * _nbytes((nseq, 1, HEAD, HEAD), F32)) + (4 << 20)
    return pl.pallas_call(
        functools.partial(_gla_body, nseq=nseq, tlen=tlen, valid=tlen, has_s0=s0 is not None),
        grid=(nseq_total // nseq, nh),
        in_specs=in_specs,
        out_specs=[pl.BlockSpec((rows, HEAD), lambda n, h: (n, h)), state_spec],
        out_shape=[jax.ShapeDtypeStruct((m, d), out_dtype),
                   jax.ShapeDtypeStruct((nseq_total, nh, HEAD, HEAD), F32)],
        compiler_params=pltpu.CompilerParams(
            dimension_semantics=("arbitrary", "arbitrary"),
            vmem_limit_bytes=_vmem_limit(blocks)),
        name="hgrn2_recurrence",
    )(*args)


def _t5_buckets(dist):
    max_exact = N_BUCKETS // 2
    n = np.asarray(dist, dtype=np.int64)
    large = max_exact + (np.log(np.maximum(n, 1) / max_exact)
                         / np.log(BUCKET_MAX_DIST / max_exact)
                         * (N_BUCKETS - max_exact)).astype(np.int64)
    large = np.minimum(large, N_BUCKETS - 1)
    return np.where(n < max_exact, n, large).astype(np.int32)


def _bucket_lookup(tab, buckets):
    flat = jnp.asarray(buckets.reshape(1, -1))
    onehot = (flat == lax.broadcasted_iota(jnp.int32, (N_BUCKETS, flat.shape[1]), 0)).astype(F32)
    out = jnp.einsum("bh,bx->hx", tab, onehot, precision=lax.Precision.HIGHEST)
    return out.reshape((tab.shape[1],) + buckets.shape)


def _prompt_bias_tiles(rel_bias, nh):
    bl = SUB_WINDOW
    off = np.arange(bl)[:, None] + bl - np.arange(2 * bl)[None, :]
    band = (off >= 0) & (off <= SUB_WINDOW)
    tiles = []
    for g, dil in enumerate(DILATIONS):
        buckets = _t5_buckets(np.clip(off, 0, SUB_WINDOW) * dil)
        tab = rel_bias[:, g * nh:(g + 1) * nh].astype(F32)
        tiles.append(jnp.where(band[None], _bucket_lookup(tab, buckets), NEG))
    return jnp.stack(tiles)


ATTN_BLOCK_UNROLL = 8


def _attn_prompt_body(q1_ref, q2_ref, q3_ref, k_ref, v_ref, bias_ref, o_ref,
                      acc_ref, lse_ref, k4_ref, v4_ref, q34_ref, *, tlen):
    bl = SUB_WINDOW
    scale = HEAD ** -0.5
    mid = DILATIONS[1]
    seg = tlen // mid

    for src_ref, dst_ref in ((k_ref, k4_ref), (v_ref, v4_ref), (q3_ref, q34_ref)):
        for r in range(mid):
            dst_ref[r * seg:(r + 1) * seg, :] = src_ref[pl.ds(r, seg, stride=mid), :]

    for g, (dil, q_ref) in enumerate(zip(DILATIONS, (q1_ref, q2_ref, q3_ref))):
        nblk = tlen // (dil * bl)
        with_prev = nblk > 1

        def block(it, carry, dil=dil, q_ref=q_ref, nblk=nblk, g=g, with_prev=with_prev):
            r = it // nblk
            bi = it % nblk
            start = r + dil * bl * bi
            qrows = pl.ds(start, bl, stride=dil) if dil > 1 else pl.ds(start, bl)
            if with_prev:
                nkey = 2 * bl
                bias = bias_ref[g, 0]
                bias_first = jnp.concatenate([bias[:, bl:2 * bl], jnp.full((bl, bl), NEG, F32)], axis=1)
                bias = jnp.where(bi == 0, bias_first, bias)
                back = jnp.where(bi > 0, bl, 0)
            else:
                nkey = bl
                bias = bias_ref[g, 0, :, bl:2 * bl]
                back = 0
            if dil == 1:
                krows = pl.ds(start - back, nkey)
                qb, k2, v2 = q_ref[qrows, :], k_ref[krows, :], v_ref[krows, :]
            elif dil == mid:
                krows = pl.ds(r * seg + bl * bi - back, nkey)
                qb, k2, v2 = q_ref[qrows, :], k4_ref[krows, :], v4_ref[krows, :]
            else:
                sub = dil // mid
                rows4 = pl.ds((r % mid) * seg + r // mid + sub * (bl * bi - back), nkey, stride=sub)
                q4rows = pl.ds((r % mid) * seg + r // mid + sub * bl * bi, bl, stride=sub)
                qb, k2, v2 = q34_ref[q4rows, :], k4_ref[rows4, :], v4_ref[rows4, :]
            qb, k2, v2 = qb.astype(BF16), k2.astype(BF16), v2.astype(BF16)
            s = lax.dot_general(qb, k2, _NT, preferred_element_type=F32) * scale + bias
            mx = jnp.max(s, axis=-1, keepdims=True)
            p = jnp.exp(s - mx)
            sm = jnp.sum(p, axis=-1, keepdims=True)
            acc = jnp.dot(p.astype(BF16), v2, preferred_element_type=F32)
            acc_ref[g, qrows, :] = acc / sm
            lse_ref[g, qrows, :] = jnp.broadcast_to(mx + jnp.log(sm), (bl, HEAD))
            return carry

        def residue_pair(it, carry, dil=dil, g=g):
            sub = dil // mid
            parts = []
            for r in (2 * it, 2 * it + 1):
                rows4 = pl.ds((r % mid) * seg + r // mid, bl, stride=sub)
                parts.append((q34_ref[rows4, :], k4_ref[rows4, :], v4_ref[rows4, :],
                              pl.ds(r, bl, stride=dil)))
            qb = jnp.concatenate([pt[0] for pt in parts], axis=0).astype(BF16)
            k2 = jnp.concatenate([pt[1] for pt in parts], axis=0).astype(BF16)
            v2 = jnp.concatenate([pt[2] for pt in parts], axis=0).astype(BF16)
            own = bias_ref[g, 0, :, bl:2 * bl]
            neg = jnp.full((bl, bl), NEG, F32)
            bias = jnp.concatenate([jnp.concatenate([own, neg], axis=1),
                                    jnp.concatenate([neg, own], axis=1)], axis=0)
            s = lax.dot_general(qb, k2, _NT, preferred_element_type=F32) * scale + bias
            mx = jnp.max(s, axis=-1, keepdims=True)
            p = jnp.exp(s - mx)
            sm = jnp.sum(p, axis=-1, keepdims=True)
            o = jnp.dot(p.astype(BF16), v2, preferred_element_type=F32) / sm
            lse = jnp.broadcast_to(mx + jnp.log(sm), (2 * bl, HEAD))
            for half, pt in enumerate(parts):
                acc_ref[g, pt[3], :] = o[half * bl:(half + 1) * bl, :]
                lse_ref[g, pt[3], :] = lse[half * bl:(half + 1) * bl, :]
            return carry

        if with_prev:
            lax.fori_loop(0, dil * nblk, block, 0, unroll=ATTN_BLOCK_UNROLL)
        else:
            lax.fori_loop(0, dil // 2, residue_pair, 0, unroll=ATTN_BLOCK_UNROLL // 2)

    def merge(ci, carry):
        rows = pl.ds(pl.multiple_of(ci * bl, bl), bl)
        l1, l2, l3 = lse_ref[0, rows, :], lse_ref[1, rows, :], lse_ref[2, rows, :]
        mm = jnp.maximum(jnp.maximum(l1, l2), l3)
        w1, w2, w3 = jnp.exp(l1 - mm), jnp.exp(l2 - mm), jnp.exp(l3 - mm)
        num = w1 * acc_ref[0, rows, :] + w2 * acc_ref[1, rows, :] + w3 * acc_ref[2, rows, :]
        o_ref[rows, :] = (num / (w1 + w2 + w3)).astype(o_ref.dtype)
        return carry

    lax.fori_loop(0, tlen // bl, merge, 0, unroll=2)


def _attn_prompt(qproj, k, v, bias_tiles, *, nseq, tlen):
    m, d3 = qproj.shape
    d = d3 // 3
    nh = d // HEAD
    qspec = lambda g: pl.BlockSpec((tlen, HEAD), lambda n, h: (n, g * nh + h))
    kvspec = pl.BlockSpec((tlen, HEAD), lambda n, h: (n, h))
    blocks = (2 * (5 * _nbytes((tlen, HEAD), F32) + _nbytes((3, 1, SUB_WINDOW, 2 * SUB_WINDOW), F32)
                   + _nbytes((tlen, HEAD), BF16))
              + 9 * _nbytes((tlen, HEAD), F32))
    return pl.pallas_call(
        functools.partial(_attn_prompt_body, tlen=tlen),
        grid=(nseq, nh),
        in_specs=[qspec(0), qspec(1), qspec(2), kvspec, kvspec,
                  pl.BlockSpec((3, 1, SUB_WINDOW, 2 * SUB_WINDOW), lambda n, h: (0, h, 0, 0))],
        out_specs=pl.BlockSpec((tlen, HEAD), lambda n, h: (n, h)),
        out_shape=jax.ShapeDtypeStruct((m, d), BF16),
        scratch_shapes=[pltpu.VMEM((3, tlen, HEAD), F32), pltpu.VMEM((3, tlen, HEAD), F32),
                        pltpu.VMEM((tlen, HEAD), F32), pltpu.VMEM((tlen, HEAD), F32),
                        pltpu.VMEM((tlen, HEAD), F32)],
        compiler_params=pltpu.CompilerParams(
            dimension_semantics=("arbitrary", "arbitrary"),
            vmem_limit_bytes=_vmem_limit(blocks)),
        name="dilated_attention_prompt",
    )(qproj, qproj, qproj, k, v, bias_tiles)


SAMPLE_HEADS_PER_STEP = V7X_SUBLANES


def _sample_key_rows(p_len, ts):
    period = max(DILATIONS)
    assert p_len % period == 0 and ts < period
    n_i = p_len // period
    tail_i0 = (p_len - DILATIONS[1] * SUB_WINDOW) // period
    assert tail_i0 >= 0 and DILATIONS[0] * SUB_WINDOW <= DILATIONS[1] * SUB_WINDOW
    main = (np.arange(n_i)[:, None] * period + np.arange(ts)[None, :]).reshape(-1)
    tail = (np.arange(tail_i0, n_i)[:, None] * period + np.arange(ts, period)[None, :]).reshape(-1)
    rows = np.concatenate([main, tail, p_len + np.arange(ts)])
    nkeys = -(-rows.size // V7X_LANES) * V7X_LANES
    return np.concatenate([rows, np.full(nkeys - rows.size, -1)]), n_i, n_i - tail_i0


def _sample_bias_table(rel_bias, nh, p_len, ts, key_rows):
    t = np.arange(ts)[:, None]
    row = key_rows[None, :]
    dist = p_len + t - row
    out = []
    for g, dil in enumerate(DILATIONS):
        j = dist // dil
        ok = (row >= 0) & (dist >= 0) & (dist % dil == 0) & (j <= SUB_WINDOW)
        buckets = _t5_buckets(np.clip(j, 0, SUB_WINDOW) * dil)
        tab = rel_bias[:, g * nh:(g + 1) * nh].astype(F32)
        out.append(jnp.where(ok[None], _bucket_lookup(tab, buckets), NEG))
    return jnp.concatenate(out, axis=1)


def _attn_sample_body(q1_ref, q2_ref, q3_ref, km_ref, kt_ref, vm_ref, vt_ref, kn_ref, vn_ref,
                      bias_ref, o_ref, kall_ref, vall_ref, *, ts):
    nkeys = kall_ref.shape[0]
    n_main = km_ref.shape[0] * km_ref.shape[1]
    n_tail = kt_ref.shape[0] * kt_ref.shape[1]
    n_used = n_main + n_tail + ts
    scale = HEAD ** -0.5
    kall_ref[n_used:nkeys, :] = jnp.zeros((nkeys - n_used, HEAD), F32)
    vall_ref[n_used:nkeys, :] = jnp.zeros((nkeys - n_used, HEAD), F32)
    for hh in range(SAMPLE_HEADS_PER_STEP):
        cols = slice(hh * HEAD, (hh + 1) * HEAD)
        for src_main, src_tail, src_new, dst in ((km_ref, kt_ref, kn_ref, kall_ref),
                                                 (vm_ref, vt_ref, vn_ref, vall_ref)):
            flat_main = src_main.reshape(n_main * SAMPLE_HEADS_PER_STEP, HEAD)
            flat_tail = src_tail.reshape(n_tail * SAMPLE_HEADS_PER_STEP, HEAD)
            dst[0:n_main, :] = flat_main[pl.ds(hh, n_main, stride=SAMPLE_HEADS_PER_STEP), :]
            dst[n_main:n_main + n_tail, :] = flat_tail[pl.ds(hh, n_tail, stride=SAMPLE_HEADS_PER_STEP), :]
            dst[n_main + n_tail:n_used, :] = src_new[:, cols]
        kb = kall_ref[...].astype(BF16)
        vb = vall_ref[...].astype(BF16)
        bias = bias_ref[hh]
        s = []
        for g, q_ref in enumerate((q1_ref, q2_ref, q3_ref)):
            sg = lax.dot_general(q_ref[:, cols].astype(BF16), kb, _NT, preferred_element_type=F32)
            s.append(sg * scale + bias[g * ts:(g + 1) * ts, :])
        mx = [jnp.max(x, axis=-1, keepdims=True) for x in s]
        mm = jnp.maximum(jnp.maximum(mx[0], mx[1]), mx[2])
        p = jnp.exp(s[0] - mm) + jnp.exp(s[1] - mm) + jnp.exp(s[2] - mm)
        den = jnp.sum(p, axis=-1, keepdims=True)
        o = jnp.dot(p.astype(BF16), vb, preferred_element_type=F32)
        o_ref[:, cols] = (o / den).astype(o_ref.dtype)


def _attn_sample(qproj, k_new, v_new, cache_k, cache_v, bias_table, n_i, n_tail, *, nseq, ts):
    m, d3 = qproj.shape
    d = d3 // 3
    nh = d // HEAD
    hps = SAMPLE_HEADS_PER_STEP
    assert nh % hps == 0 and n_i % n_tail == 0
    p_len = cache_k.shape[1]
    period = p_len // n_i
    nkeys = bias_table.shape[-1]
    ngrp = nh // hps
    wide = hps * HEAD
    qspec = lambda g: pl.BlockSpec((ts, wide), lambda n, h: (n, g * ngrp + h))
    newspec = pl.BlockSpec((ts, wide), lambda n, h: (n, h))
    mainspec = pl.BlockSpec((None, n_i, ts, hps, HEAD), lambda n, h: (n, 0, 0, h, 0))
    tailspec = pl.BlockSpec((None, n_tail, period - ts, hps, HEAD),
                            lambda n, h: (n, n_i // n_tail - 1, 1, h, 0))
    assert period - ts == ts
    cache_bytes = _nbytes((n_i + n_tail, ts, hps, HEAD), F32)
    blocks = (2 * (2 * cache_bytes + _nbytes((hps, 3 * ts, nkeys), F32))
              + 2 * _nbytes((nkeys, HEAD), F32) + 2 * _nbytes((nkeys, HEAD), BF16)
              + 8 * _nbytes((3 * ts, nkeys), F32))
    k5 = cache_k.reshape(nseq, n_i, period, nh, HEAD)
    v5 = cache_v.reshape(nseq, n_i, period, nh, HEAD)
    return pl.pallas_call(
        functools.partial(_attn_sample_body, ts=ts),
        grid=(nseq, ngrp),
        in_specs=[qspec(0), qspec(1), qspec(2), mainspec, tailspec, mainspec, tailspec,
                  newspec, newspec,
                  pl.BlockSpec((hps, 3 * ts, nkeys), lambda n, h: (h, 0, 0))],
        out_specs=pl.BlockSpec((ts, wide), lambda n, h: (n, h)),
        out_shape=jax.ShapeDtypeStruct((m, d), F32),
        scratch_shapes=[pltpu.VMEM((nkeys, HEAD), F32), pltpu.VMEM((nkeys, HEAD), F32)],
        compiler_params=pltpu.CompilerParams(
            dimension_semantics=("arbitrary", "arbitrary"),
            vmem_limit_bytes=_vmem_limit(blocks)),
        name="dilated_attention_sample",
    )(qproj, qproj, qproj, k5, k5, v5, v5, k_new, v_new, bias_table)


def _lower_bounds(lb_raw):
    p = jax.nn.softmax(lb_raw.astype(F32), axis=0)
    c = jnp.cumsum(p, axis=0)
    return c - c[0:1]


def kernel(x_prompt, x_sample, state_hgrn, state_ffn_conv, cache_k_win, cache_v_win, a_w_in, a_lb_raw, a_out_gain, a_w_out, kv_norm, w_kv, b_w_q, b_w_out, rel_bias, ffn_w_up, ffn_conv_w, ffn_conv_b, ffn_w_down, norm_mix, norm_ffn, norm_final):
    nb_p, t_p, d = x_prompt.shape
    nb_s, t_s, _ = x_sample.shape
    depth = ffn_w_up.shape[0]
    n_a = a_w_in.shape[0]
    nh = d // HEAD
    f = ffn_w_down.shape[1]
    p_len = cache_k_win.shape[1]
    assert t_p % (SUB_WINDOW * max(DILATIONS)) == 0 and t_s == V7X_SUBLANES

    xp = x_prompt.reshape(nb_p * t_p, d)
    xs = x_sample.reshape(nb_s * t_s, d)
    lbs = _lower_bounds(a_lb_raw)
    key_rows, n_i, n_tail = _sample_key_rows(p_len, t_s)
    bias_p = _prompt_bias_tiles(rel_bias, nh)
    bias_s = _sample_bias_table(rel_bias, nh, p_len, t_s, key_rows)

    hgrn_p, hgrn_s, conv_p, conv_s = [], [], [], []
    kp = vp = ks = vs = None
    w_kv3 = w_kv.reshape(1, d, 2 * d)
    conv_b3 = ffn_conv_b.reshape(depth, 1, f)
    for l in range(depth):
        hp = _rmsnorm(xp, norm_mix[l], BF16)
        hs = _rmsnorm(xs, norm_mix[l], BF16)
        if l < n_a:
            pp, ps = _linear(hp, hs, a_w_in, l, name="hgrn2_in_proj")
            op, sp = _gla(pp, lbs[l], a_out_gain[l], None, nseq_total=nb_p, tlen=t_p, out_dtype=BF16)
            os_, ss = _gla(ps, lbs[l], a_out_gain[l], state_hgrn[l], nseq_total=nb_s, tlen=t_s,
                           out_dtype=F32)
            hgrn_p.append(sp)
            hgrn_s.append(ss)
            w_o, l_o = a_w_out, l
        else:
            if l == n_a:
                kvp = _rmsnorm(xp, kv_norm, BF16)
                kvs = _rmsnorm(xs, kv_norm, BF16)
                kp, ks = _linear(kvp, kvs, w_kv3, 0, col0=0, n=d, name="k_proj")
                vp, vs = _linear(kvp, kvs, w_kv3, 0, col0=d, n=d, name="v_proj")
            jb = l - n_a
            qp, qs = _linear(hp, hs, b_w_q, jb, name="q_proj")
            op = _attn_prompt(qp, kp, vp, bias_p, nseq=nb_p, tlen=t_p)
            os_ = _attn_sample(qs, ks, vs, cache_k_win, cache_v_win, bias_s, n_i, n_tail,
                               nseq=nb_s, ts=t_s)
            w_o, l_o = b_w_out, jb
        xp, xs = _linear(op, os_, w_o, l_o, res=(xp, xs), name="mixer_out_proj")

        hp = _rmsnorm(xp, norm_ffn[l], BF16)
        hs = _rmsnorm(xs, norm_ffn[l], BF16)
        prev = state_ffn_conv[l]
        ext_s = jnp.pad(prev, ((0, 0), (0, t_s - (CONV_W - 1)), (0, 0))).reshape(nb_s * t_s, f)
        actp, acts, tailp, a_s = _ffn_up(hp, hs, ffn_w_up, ffn_conv_w, conv_b3, ext_s, l,
                                         tlen=t_p, ts=t_s)
        conv_p.append(tailp.reshape(nb_p, 8, f)[:, 8 - (CONV_W - 1):, :])
        conv_s.append(a_s.reshape(nb_s, t_s, f)[:, t_s - (CONV_W - 1):, :])
        xp, xs = _linear(actp, acts, ffn_w_down, l, res=(xp, xs), name="ffn_down_proj")

    yp = _rmsnorm(xp, norm_final, F32).reshape(nb_p, t_p, d)
    ys = _rmsnorm(xs, norm_final, F32).reshape(nb_s, t_s, d)
    keep = min(BUCKET_MAX_DIST, t_p)
    k_win = kp.reshape(nb_p, t_p, nh, HEAD)[:, t_p - keep:]
    v_win = vp.reshape(nb_p, t_p, nh, HEAD)[:, t_p - keep:]
    return (yp, ys, jnp.stack(hgrn_p), jnp.stack(hgrn_s), jnp.stack(conv_p), jnp.stack(conv_s),
            k_win, v_win, ks.reshape(nb_s, t_s, nh, HEAD), vs.reshape(nb_s, t_s, nh, HEAD))
```

```python
import functools
import math

import numpy as np
import jax
import jax.numpy as jnp
from jax import lax
from jax.experimental import pallas as pl
from jax.experimental.pallas import tpu as pltpu

F32 = jnp.float32
BF16 = jnp.bfloat16

V7X_VMEM_BYTES = 64 * 1024 * 1024
V7X_LANES = 128
V7X_SUBLANES = 8
COMPILER_SCRATCH_BYTES = 12 * 1024 * 1024

HEAD = 128
GLA_CHUNK = 128
GLA_SEQ_UNROLL = 8
GLA_HEADS_PER_STEP = 2
GLA_CHUNK_UNROLL = 8
SUB_WINDOW = 128
DILATIONS = (1, 4, 16)
N_BUCKETS = 32
BUCKET_MAX_DIST = 2048
CONV_W = 3
EPS = 1e-6
NEG = -1e30
LOG2_E = 1.4426950408889634


def _vmem_limit(block_bytes):
    return int(min(block_bytes + COMPILER_SCRATCH_BYTES, V7X_VMEM_BYTES - (2 << 20)))


def _nbytes(shape, dtype):
    return int(np.prod(shape)) * jnp.dtype(dtype).itemsize


def _rmsnorm_body(x_ref, *refs):
    g_refs, o_refs = refs[:len(refs) // 2], refs[len(refs) // 2:]
    x = x_ref[...]
    ms = jnp.mean(x * x, axis=-1, keepdims=True)
    y = x * lax.rsqrt(ms + EPS)
    for g_ref, o_ref in zip(g_refs, o_refs):
        o_ref[...] = (y * g_ref[...]).astype(o_ref.dtype)


def _rmsnorm(x, gains, out_dtype):
    m, d = x.shape
    tm = min(m, 256)
    assert m % tm == 0
    n = len(gains)
    blocks = 2 * (_nbytes((tm, d), F32) + n * _nbytes((tm, d), out_dtype))
    row_spec = pl.BlockSpec((tm, d), lambda i: (i, 0))
    return pl.pallas_call(
        _rmsnorm_body,
        grid=(m // tm,),
        in_specs=[row_spec] + [pl.BlockSpec((1, d), lambda i: (0, 0))] * n,
        out_specs=[row_spec] * n,
        out_shape=[jax.ShapeDtypeStruct((m, d), out_dtype)] * n,
        compiler_params=pltpu.CompilerParams(vmem_limit_bytes=_vmem_limit(blocks)),
        name="rmsnorm",
    )(x, *[g.reshape(1, d) for g in gains])


W_CAST_ROWS = 256


def _cast_weight(w_ref, wb_ref, col0=0):
    k, width = w_ref.shape
    assert k % W_CAST_ROWS == 0

    def step(c, carry):
        rows = pl.ds(pl.multiple_of(c * W_CAST_ROWS, W_CAST_ROWS), W_CAST_ROWS)
        wb_ref[rows, col0:col0 + width] = w_ref[rows, :].astype(BF16)
        return carry

    lax.fori_loop(0, k // W_CAST_ROWS, step, 0)


def _linear_body(*refs, layer, col0, kc, nchunks, spc, has_res):
    if has_res:
        xp_ref, xs_ref, w_hbm, rp_ref, rs_ref, yp_ref, ys_ref, wb_ref, stg_ref, sem = refs
    else:
        xp_ref, xs_ref, w_hbm, yp_ref, ys_ref, wb_ref, stg_ref, sem = refs
    j = pl.program_id(0)
    i = pl.program_id(1)
    nj = pl.num_programs(0)
    tn = wb_ref.shape[2]
    cur = j % 2

    def chunk_copy(jt, c):
        rows = pl.ds(pl.multiple_of(c * kc, kc), kc)
        cols = pl.ds(pl.multiple_of(col0 + jt * tn, tn), tn)
        return pltpu.make_async_copy(w_hbm.at[layer, rows, cols], stg_ref.at[c % 2], sem.at[c % 2])

    def land(jt, c, wslot):
        chunk_copy(jt, c).wait()
        rows = pl.ds(pl.multiple_of(c * kc, kc), kc)
        wb_ref[wslot, rows, :] = stg_ref[c % 2].astype(BF16)

    @pl.when(jnp.logical_and(j == 0, i == 0))
    def _first_tile():
        chunk_copy(0, 0).start()

        def step(c, carry):
            @pl.when(c + 1 < nchunks)
            def _():
                chunk_copy(0, c + 1).start()
            land(0, c, 0)
            return carry

        lax.fori_loop(0, nchunks, step, 0)

    @pl.when(jnp.logical_and(j > 0, i == 0))
    def _finish_prefetched_tile():
        land(j, nchunks - 1, cur)

    @pl.when(jnp.logical_and(j + 1 < nj, i % spc == 0))
    def _prefetch_next_tile():
        c = i // spc
        chunk_copy(j + 1, c).start()

        @pl.when(c > 0)
        def _():
            land(j + 1, c - 1, 1 - cur)

    @pl.when(i == 0)
    def _sample_rows():
        ys = jnp.dot(xs_ref[...].astype(BF16), wb_ref[cur], preferred_element_type=F32)
        if has_res:
            ys = ys + rs_ref[...]
        ys_ref[...] = ys

    yp = jnp.dot(xp_ref[...], wb_ref[cur], preferred_element_type=F32)
    if has_res:
        yp = yp + rp_ref[...]
    yp_ref[...] = yp


BF16_SUBLANES = 16
MAX_W_CHUNKS = 16


def _linear_tiles(mp, k, n, col0):
    tm, tn_max = (512, 1024) if k <= 4096 else (256, 512)
    tm = min(tm, mp)
    tn = next(t for t in (tn_max, tn_max // 2, tn_max // 4, V7X_LANES)
              if n % t == 0 and col0 % t == 0)
    ni = mp // tm
    nchunks = next(c for c in range(min(MAX_W_CHUNKS, ni), 0, -1)
                   if ni % c == 0 and k % (c * BF16_SUBLANES) == 0)
    return tm, tn, k // nchunks, nchunks, ni // nchunks


def _linear(xp, xs, w, layer, *, col0=0, n=None, res=None, name):
    mp, k = xp.shape
    ms = xs.shape[0]
    n = w.shape[2] if n is None else n
    tm, tn, kc, nchunks, spc = _linear_tiles(mp, k, n, col0)
    assert mp % tm == 0 and n % tn == 0
    in_specs = [pl.BlockSpec((tm, k), lambda j, i: (i, 0)),
                pl.BlockSpec((ms, k), lambda j, i: (0, 0)),
                pl.BlockSpec(memory_space=pl.ANY)]
    args = [xp, xs, w]
    blocks = (2 * (_nbytes((tm, k), BF16) + _nbytes((ms, k), xs.dtype)
                   + _nbytes((tm, tn), F32) + _nbytes((ms, tn), F32))
              + 2 * _nbytes((k, tn), BF16) + 2 * _nbytes((kc, tn), F32))
    if res is not None:
        in_specs += [pl.BlockSpec((tm, tn), lambda j, i: (i, j)),
                     pl.BlockSpec((ms, tn), lambda j, i: (0, j))]
        args += list(res)
        blocks += 2 * (_nbytes((tm, tn), F32) + _nbytes((ms, tn), F32))
    return pl.pallas_call(
        functools.partial(_linear_body, layer=layer, col0=col0, kc=kc, nchunks=nchunks, spc=spc,
                          has_res=res is not None),
        grid=(n // tn, mp // tm),
        in_specs=in_specs,
        out_specs=[pl.BlockSpec((tm, tn), lambda j, i: (i, j)),
                   pl.BlockSpec((ms, tn), lambda j, i: (0, j))],
        out_shape=[jax.ShapeDtypeStruct((mp, n), F32), jax.ShapeDtypeStruct((ms, n), F32)],
        scratch_shapes=[pltpu.VMEM((2, k, tn), BF16), pltpu.VMEM((2, kc, tn), F32),
                        pltpu.SemaphoreType.DMA((2,))],
        compiler_params=pltpu.CompilerParams(
            dimension_semantics=("arbitrary", "arbitrary"),
            vmem_limit_bytes=_vmem_limit(blocks)),
        name=name,
    )(*args)


FFN_STEP_ROWS = 1024
FFN_SUB_ROWS = 256


def _ffn_up_body(hp_ref, hs_ref, wa_ref, wg_ref, cw_ref, cb_ref, exts_ref,
                 actp_ref, acts_ref, tailp_ref, as_ref,
                 wcat_ref, ext_ref, *, tm, tlen, ts):
    i = pl.program_id(1)
    tf = wa_ref.shape[1]
    cw = cw_ref[...]
    cb = cb_ref[...]

    def act_fn(a2, a1, a0, b):
        conv = cb + cw[0:1, :] * a2 + cw[1:2, :] * a1 + cw[2:3, :] * a0
        return jax.nn.silu(conv) * b

    @pl.when(i == 0)
    def _new_column_tile():
        _cast_weight(wa_ref, wcat_ref, 0)
        _cast_weight(wg_ref, wcat_ref, tf)
        ab = jnp.dot(hs_ref[...].astype(BF16), wcat_ref[...], preferred_element_type=F32)
        a, b = ab[:, 0:tf], ab[:, tf:2 * tf]
        ms = a.shape[0]
        t = lax.broadcasted_iota(jnp.int32, a.shape, 0) % ts
        ext = exts_ref[...]
        a1 = jnp.where(t == 0, pltpu.roll(ext, ms - 1, axis=0), pltpu.roll(a, 1, axis=0))
        a2 = jnp.where(t < 2, ext, pltpu.roll(a, 2, axis=0))
        acts_ref[...] = act_fn(a2, a1, a, b).astype(acts_ref.dtype)
        as_ref[...] = a

    @pl.when((i * tm) % tlen == 0)
    def _sequence_start():
        ext_ref[...] = jnp.zeros(ext_ref.shape, F32)

    tail = ext_ref[...]
    for s in range(tm // FFN_SUB_ROWS):
        rows = slice(s * FFN_SUB_ROWS, (s + 1) * FFN_SUB_ROWS)
        hp = hp_ref[rows, :]
        ab = jnp.dot(hp, wcat_ref[...], preferred_element_type=F32)
        a, b = ab[:, 0:tf], ab[:, tf:2 * tf]
        ext = jnp.concatenate([tail, a], axis=0)
        a1 = ext[7:7 + FFN_SUB_ROWS, :]
        a2 = ext[6:6 + FFN_SUB_ROWS, :]
        actp_ref[rows, :] = act_fn(a2, a1, a, b).astype(actp_ref.dtype)
        tail = a[FFN_SUB_ROWS - 8:FFN_SUB_ROWS, :]
    ext_ref[...] = tail

    @pl.when(((i + 1) * tm) % tlen == 0)
    def _sequence_end():
        n = (i * tm) // tlen
        tailp_ref[pl.ds(pl.multiple_of(n * 8, 8), 8), :] = tail


def _ffn_up(hp, hs, w_up, conv_w, conv_b, ext_s, layer, *, tlen, ts):
    mp, d = hp.shape
    ms = hs.shape[0]
    f = w_up.shape[2] // 2
    nseq = mp // tlen
    tm = min(FFN_STEP_ROWS, tlen)
    tf = 256
    assert mp % tm == 0 and tlen % tm == 0 and f % tf == 0 and tm % FFN_SUB_ROWS == 0
    nf = f // tf
    blocks = (2 * (_nbytes((tm, d), BF16) + _nbytes((ms, d), hs.dtype) + 2 * _nbytes((d, tf), F32)
                   + _nbytes((tm, tf), BF16) + 3 * _nbytes((ms, tf), F32) + _nbytes((nseq * 8, tf), F32))
              + 2 * _nbytes((d, tf), BF16) + _nbytes((8, tf), F32) + 5 * _nbytes((tm, tf), F32))
    return pl.pallas_call(
        functools.partial(_ffn_up_body, tm=tm, tlen=tlen, ts=ts),
        grid=(nf, mp // tm),
        in_specs=[pl.BlockSpec((tm, d), lambda j, i: (i, 0)),
                  pl.BlockSpec((ms, d), lambda j, i: (0, 0)),
                  pl.BlockSpec((None, d, tf), lambda j, i: (layer, 0, j)),
                  pl.BlockSpec((None, d, tf), lambda j, i: (layer, 0, j + nf)),
                  pl.BlockSpec((None, CONV_W, tf), lambda j, i: (layer, 0, j)),
                  pl.BlockSpec((None, 1, tf), lambda j, i: (layer, 0, j)),
                  pl.BlockSpec((ms, tf), lambda j, i: (0, j))],
        out_specs=[pl.BlockSpec((tm, tf), lambda j, i: (i, j)),
                   pl.BlockSpec((ms, tf), lambda j, i: (0, j)),
                   pl.BlockSpec((nseq * 8, tf), lambda j, i: (0, j)),
                   pl.BlockSpec((ms, tf), lambda j, i: (0, j))],
        out_shape=[jax.ShapeDtypeStruct((mp, f), BF16),
                   jax.ShapeDtypeStruct((ms, f), BF16),
                   jax.ShapeDtypeStruct((nseq * 8, f), F32),
                   jax.ShapeDtypeStruct((ms, f), F32)],
        scratch_shapes=[pltpu.VMEM((d, 2 * tf), BF16), pltpu.VMEM((8, tf), F32)],
        compiler_params=pltpu.CompilerParams(
            dimension_semantics=("arbitrary", "arbitrary"),
            vmem_limit_bytes=_vmem_limit(blocks)),
        name="ffn_up",
    )(hp, hs, w_up, w_up, conv_w, conv_b, ext_s)


def _level_table(c):
    t = np.arange(c)[:, None]
    s = np.arange(c)[None, :]
    x = np.maximum(t ^ s, 1)
    lev = np.floor(np.log2(x)).astype(np.int32)
    return np.where(t > s, lev, np.where(t == s, -1, -2)).astype(np.int32)


def _cumsum_table(c):
    return np.tril(np.ones((c, c), np.float32))


def _split3(x):
    x0 = x.astype(BF16)
    r1 = x - x0.astype(F32)
    x1 = r1.astype(BF16)
    x2 = (r1 - x1.astype(F32)).astype(BF16)
    return x0, x1, x2


def _block_boundary_rows(b, blk):
    c, w = b.shape
    half = blk // 2
    if blk >= 2 * V7X_SUBLANES:
        x = b.reshape(c // blk, blk, w)
        r = jnp.broadcast_to(x[:, half - 1:half, :], x.shape)
        return r.reshape(c, w)
    x = b.reshape(c // V7X_SUBLANES, V7X_SUBLANES, w)
    sub = lax.broadcasted_iota(jnp.int32, x.shape, 1)

    def row(i):
        return jnp.broadcast_to(x[:, i:i + 1, :], x.shape)

    if blk == 8:
        r = row(3)
    elif blk == 4:
        r = jnp.where(sub < 4, row(1), row(5))
    else:
        r = jnp.where(sub < 2, row(0), jnp.where(sub < 4, row(2), jnp.where(sub < 6, row(4), row(6))))
    return r.reshape(c, w)


_NT = (((1,), (1,)), ((), ()))
_TN = (((0,), (0,)), ((), ()))


def _gla_body(*refs, nseq, tlen, valid, has_s0, heads):
    if has_s0:
        (q_ref, f_ref, v_ref, g_ref, lb_ref, gain_ref, lev_ref, tri_ref, s0_ref,
         o_ref, so_ref) = refs
    else:
        q_ref, f_ref, v_ref, g_ref, lb_ref, gain_ref, lev_ref, tri_ref, o_ref, so_ref = refs
    for hh in range(heads):
        _gla_head(q_ref, f_ref, v_ref, g_ref, lb_ref, gain_ref, lev_ref, tri_ref,
                  s0_ref if has_s0 else None, o_ref, so_ref, hh, nseq=nseq, tlen=tlen, valid=valid)


def _gla_head(q_ref, f_ref, v_ref, g_ref, lb_ref, gain_ref, lev_ref, tri_ref, s0_ref, o_ref, so_ref,
              hh, *, nseq, tlen, valid):
    has_s0 = s0_ref is not None
    cols = slice(hh * HEAD, (hh + 1) * HEAD)
    c = GLA_CHUNK
    rows_in = min(c, valid)
    nchunk = max(tlen // c, 1)
    lb = lb_ref[:, cols]
    log_lb = jnp.log(lb)
    log1m_lb = jnp.log1p(-lb)
    one_m_lb = 1.0 - lb
    gain = gain_ref[:, cols]
    n_levels = int(math.log2(rows_in))
    assert 1 << n_levels == rows_in

    def pad_rows(x):
        if rows_in == c:
            return x
        return jnp.concatenate([x, jnp.zeros((c - rows_in, x.shape[1]), x.dtype)], axis=0)

    def chunk(n, ci, st):
        row0 = pl.multiple_of(n * tlen + ci * rows_in, rows_in)
        rows = pl.ds(row0, rows_in)
        qr, fz, v = q_ref[rows, cols], f_ref[rows, cols], v_ref[rows, cols]
        e = jnp.exp(-jnp.abs(fz))
        one_p_e = 1.0 + e
        log_sig = jnp.minimum(fz, 0.0) - jnp.log(one_p_e)
        u = log1m_lb + log_sig
        log_f = jnp.maximum(log_lb, u) + jnp.log(1.0 + jnp.exp(-jnp.abs(log_lb - u)))
        k = one_m_lb * (jnp.where(fz > 0.0, e, 1.0) / one_p_e)
        q = jax.nn.silu(qr) * (HEAD ** -0.5)
        vb = pad_rows(v.astype(BF16))

        tri = tri_ref[...]
        b = sum(jnp.dot(tri, pad_rows(part), preferred_element_type=F32) for part in _split3(log_f))
        b = b * LOG2_E
        b_last = b[c - 1:c, :]
        b = b[0:rows_in, :]

        lev = lev_ref[...]
        diag = pad_rows(jnp.sum(q * k, axis=-1, keepdims=True))
        att = jnp.where(lev == -1, diag, 0.0)
        for l in range(n_levels):
            r = _block_boundary_rows(b, 2 << l)
            decay = jnp.exp2(-jnp.abs(b - r))
            qt = pad_rows((q * decay).astype(BF16))
            kt = pad_rows((k * decay).astype(BF16))
            s_l = lax.dot_general(qt, kt, _NT, preferred_element_type=F32)
            att = jnp.where(lev == l, s_l, att)
        o = jnp.dot(att.astype(BF16), vb, preferred_element_type=F32)
        o = o + lax.dot_general(pad_rows((q * jnp.exp2(b)).astype(BF16)), st.astype(BF16), _NT,
                                preferred_element_type=F32)
        kd = pad_rows((k * jnp.exp2(b_last - b)).astype(BF16))
        st_new = st * jnp.exp2(b_last) + lax.dot_general(vb, kd, _TN, preferred_element_type=F32)

        o = o[0:rows_in, :]
        o = o * lax.rsqrt(jnp.mean(o * o, axis=-1, keepdims=True) + EPS) * gain
        o = o * jax.nn.silu(g_ref[rows, cols])
        o_ref[rows, cols] = o.astype(o_ref.dtype)
        return st_new

    def sequence(n, carry):
        if has_s0:
            st0 = s0_ref[n, hh].T
        else:
            st0 = jnp.zeros((HEAD, HEAD), F32)
        st = lax.fori_loop(0, nchunk, lambda ci, st: chunk(n, ci, st), st0,
                           unroll=min(GLA_CHUNK_UNROLL, nchunk))
        so_ref[n, hh] = st.T
        return carry

    lax.fori_loop(0, nseq, sequence, 0, unroll=min(nseq, GLA_SEQ_UNROLL))


def _gla(proj, lb, gain, s0, *, nseq_total, tlen, out_dtype):
    m, d4 = proj.shape
    d = d4 // 4
    nh = d // HEAD
    nseq = 1 if tlen >= GLA_CHUNK else nseq_total
    assert tlen % GLA_CHUNK == 0 or tlen < GLA_CHUNK
    rows = nseq * tlen
    hps = GLA_HEADS_PER_STEP if nh % GLA_HEADS_PER_STEP == 0 else 1
    wide = hps * HEAD
    ngrp = nh // hps
    col = lambda off: pl.BlockSpec((rows, wide), lambda n, h: (n, h + off))
    vec = pl.BlockSpec((1, wide), lambda n, h: (0, h))
    const = pl.BlockSpec((GLA_CHUNK, GLA_CHUNK), lambda n, h: (0, 0))
    in_specs = [col(0), col(ngrp), col(2 * ngrp), col(3 * ngrp), vec, vec, const, const]
    args = [proj, proj, proj, proj, lb.reshape(1, d), gain.reshape(1, d),
            jnp.asarray(_level_table(GLA_CHUNK)), jnp.asarray(_cumsum_table(GLA_CHUNK), BF16)]
    state_spec = pl.BlockSpec((nseq, hps, HEAD, HEAD), lambda n, h: (n, h, 0, 0))
    if s0 is not None:
        in_specs.append(state_spec)
        args.append(s0)
    blocks = 2 * (4 * _nbytes((rows, wide), F32) + _nbytes((rows, wide), out_dtype)
                  + 2 * _nbytes((nseq, hps, HEAD, HEAD), F32)) + (4 << 20)
    return pl.pallas_call(
        functools.partial(_gla_body, nseq=nseq, tlen=tlen, valid=tlen, has_s0=s0 is not None,
                          heads=hps),
        grid=(nseq_total // nseq, ngrp),
        in_specs=in_specs,
        out_specs=[pl.BlockSpec((rows, wide), lambda n, h: (n, h)), state_spec],
        out_shape=[jax.ShapeDtypeStruct((m, d), out_dtype),
                   jax.ShapeDtypeStruct((nseq_total, nh, HEAD, HEAD), F32)],
        compiler_params=pltpu.CompilerParams(
            dimension_semantics=("arbitrary", "arbitrary"),
            vmem_limit_bytes=_vmem_limit(blocks)),
        name="hgrn2_recurrence",
    )(*args)


def _t5_buckets(dist):
    max_exact = N_BUCKETS // 2
    n = np.asarray(dist, dtype=np.int64)
    large = max_exact + (np.log(np.maximum(n, 1) / max_exact)
                         / np.log(BUCKET_MAX_DIST / max_exact)
                         * (N_BUCKETS - max_exact)).astype(np.int64)
    large = np.minimum(large, N_BUCKETS - 1)
    return np.where(n < max_exact, n, large).astype(np.int32)


def _bucket_lookup(tab, buckets):
    flat = jnp.asarray(buckets.reshape(1, -1))
    onehot = (flat == lax.broadcasted_iota(jnp.int32, (N_BUCKETS, flat.shape[1]), 0)).astype(F32)
    out = jnp.einsum("bh,bx->hx", tab, onehot, precision=lax.Precision.HIGHEST)
    return out.reshape((tab.shape[1],) + buckets.shape)


def _prompt_bias_tiles(rel_bias, nh):
    bl = SUB_WINDOW
    off = np.arange(bl)[:, None] + bl - np.arange(2 * bl)[None, :]
    band = (off >= 0) & (off <= SUB_WINDOW)
    tiles = []
    for g, dil in enumerate(DILATIONS):
        buckets = _t5_buckets(np.clip(off, 0, SUB_WINDOW) * dil)
        tab = rel_bias[:, g * nh:(g + 1) * nh].astype(F32)
        tiles.append(jnp.where(band[None], _bucket_lookup(tab, buckets), NEG))
    return jnp.stack(tiles)


ATTN_BLOCK_UNROLL = 8


def _attn_prompt_body(q1_ref, q2_ref, q3_ref, k_ref, v_ref, bias_ref, o_ref,
                      acc_ref, lse_ref, k4_ref, v4_ref, q34_ref, *, tlen):
    bl = SUB_WINDOW
    scale = HEAD ** -0.5
    mid = DILATIONS[1]
    seg = tlen // mid

    for src_ref, dst_ref in ((k_ref, k4_ref), (v_ref, v4_ref), (q3_ref, q34_ref)):
        for r in range(mid):
            dst_ref[r * seg:(r + 1) * seg, :] = src_ref[pl.ds(r, seg, stride=mid), :]

    for g, (dil, q_ref) in enumerate(zip(DILATIONS, (q1_ref, q2_ref, q3_ref))):
        nblk = tlen // (dil * bl)
        with_prev = nblk > 1

        def block(it, carry, dil=dil, q_ref=q_ref, nblk=nblk, g=g, with_prev=with_prev):
            r = it // nblk
            bi = it % nblk
            start = r + dil * bl * bi
            qrows = pl.ds(start, bl, stride=dil) if dil > 1 else pl.ds(start, bl)
            if with_prev:
                nkey = 2 * bl
                bias = bias_ref[g, 0]
                bias_first = jnp.concatenate([bias[:, bl:2 * bl], jnp.full((bl, bl), NEG, F32)], axis=1)
                bias = jnp.where(bi == 0, bias_first, bias)
                back = jnp.where(bi > 0, bl, 0)
            else:
                nkey = bl
                bias = bias_ref[g, 0, :, bl:2 * bl]
                back = 0
            if dil == 1:
                krows = pl.ds(start - back, nkey)
                qb, k2, v2 = q_ref[qrows, :], k_ref[krows, :], v_ref[krows, :]
            elif dil == mid:
                krows = pl.ds(r * seg + bl * bi - back, nkey)
                qb, k2, v2 = q_ref[qrows, :], k4_ref[krows, :], v4_ref[krows, :]
            else:
                sub = dil // mid
                rows4 = pl.ds((r % mid) * seg + r // mid + sub * (bl * bi - back), nkey, stride=sub)
                q4rows = pl.ds((r % mid) * seg + r // mid + sub * bl * bi, bl, stride=sub)
                qb, k2, v2 = q34_ref[q4rows, :], k4_ref[rows4, :], v4_ref[rows4, :]
            qb, k2, v2 = qb.astype(BF16), k2.astype(BF16), v2.astype(BF16)
            s = lax.dot_general(qb, k2, _NT, preferred_element_type=F32) * scale + bias
            mx = jnp.max(s, axis=-1, keepdims=True)
            p = jnp.exp(s - mx)
            sm = jnp.sum(p, axis=-1, keepdims=True)
            acc = jnp.dot(p.astype(BF16), v2, preferred_element_type=F32)
            acc_ref[g, qrows, :] = acc / sm
            lse_ref[g, qrows, :] = jnp.broadcast_to(mx + jnp.log(sm), (bl, HEAD))
            return carry

        def residue_pair(it, carry, dil=dil, g=g):
            sub = dil // mid
            parts = []
            for r in (2 * it, 2 * it + 1):
                rows4 = pl.ds((r % mid) * seg + r // mid, bl, stride=sub)
                parts.append((q34_ref[rows4, :], k4_ref[rows4, :], v4_ref[rows4, :],
                              pl.ds(r, bl, stride=dil)))
            qb = jnp.concatenate([pt[0] for pt in parts], axis=0).astype(BF16)
            k2 = jnp.concatenate([pt[1] for pt in parts], axis=0).astype(BF16)
            v2 = jnp.concatenate([pt[2] for pt in parts], axis=0).astype(BF16)
            own = bias_ref[g, 0, :, bl:2 * bl]
            neg = jnp.full((bl, bl), NEG, F32)
            bias = jnp.concatenate([jnp.concatenate([own, neg], axis=1),
                                    jnp.concatenate([neg, own], axis=1)], axis=0)
            s = lax.dot_general(qb, k2, _NT, preferred_element_type=F32) * scale + bias
            mx = jnp.max(s, axis=-1, keepdims=True)
            p = jnp.exp(s - mx)
            sm = jnp.sum(p, axis=-1, keepdims=True)
            o = jnp.dot(p.astype(BF16), v2, preferred_element_type=F32) / sm
            lse = jnp.broadcast_to(mx + jnp.log(sm), (2 * bl, HEAD))
            for half, pt in enumerate(parts):
                acc_ref[g, pt[3], :] = o[half * bl:(half + 1) * bl, :]
                lse_ref[g, pt[3], :] = lse[half * bl:(half + 1) * bl, :]
            return carry

        if with_prev:
            lax.fori_loop(0, dil * nblk, block, 0, unroll=ATTN_BLOCK_UNROLL)
        else:
            lax.fori_loop(0, dil // 2, residue_pair, 0, unroll=ATTN_BLOCK_UNROLL // 2)

    def merge(ci, carry):
        rows = pl.ds(pl.multiple_of(ci * bl, bl), bl)
        l1, l2, l3 = lse_ref[0, rows, :], lse_ref[1, rows, :], lse_ref[2, rows, :]
        mm = jnp.maximum(jnp.maximum(l1, l2), l3)
        w1, w2, w3 = jnp.exp(l1 - mm), jnp.exp(l2 - mm), jnp.exp(l3 - mm)
        num = w1 * acc_ref[0, rows, :] + w2 * acc_ref[1, rows, :] + w3 * acc_ref[2, rows, :]
        o_ref[rows, :] = (num / (w1 + w2 + w3)).astype(o_ref.dtype)
        return carry

    lax.fori_loop(0, tlen // bl, merge, 0, unroll=2)


def _attn_prompt(qproj, k, v, bias_tiles, *, nseq, tlen):
    m, d3 = qproj.shape
    d = d3 // 3
    nh = d // HEAD
    qspec = lambda g: pl.BlockSpec((tlen, HEAD), lambda n, h: (n, g * nh + h))
    kvspec = pl.BlockSpec((tlen, HEAD), lambda n, h: (n, h))
    blocks = (2 * (5 * _nbytes((tlen, HEAD), F32) + _nbytes((3, 1, SUB_WINDOW, 2 * SUB_WINDOW), F32)
                   + _nbytes((tlen, HEAD), BF16))
              + 9 * _nbytes((tlen, HEAD), F32))
    return pl.pallas_call(
        functools.partial(_attn_prompt_body, tlen=tlen),
        grid=(nseq, nh),
        in_specs=[qspec(0), qspec(1), qspec(2), kvspec, kvspec,
                  pl.BlockSpec((3, 1, SUB_WINDOW, 2 * SUB_WINDOW), lambda n, h: (0, h, 0, 0))],
        out_specs=pl.BlockSpec((tlen, HEAD), lambda n, h: (n, h)),
        out_shape=jax.ShapeDtypeStruct((m, d), BF16),
        scratch_shapes=[pltpu.VMEM((3, tlen, HEAD), F32), pltpu.VMEM((3, tlen, HEAD), F32),
                        pltpu.VMEM((tlen, HEAD), F32), pltpu.VMEM((tlen, HEAD), F32),
                        pltpu.VMEM((tlen, HEAD), F32)],
        compiler_params=pltpu.CompilerParams(
            dimension_semantics=("arbitrary", "arbitrary"),
            vmem_limit_bytes=_vmem_limit(blocks)),
        name="dilated_attention_prompt",
    )(qproj, qproj, qproj, k, v, bias_tiles)


SAMPLE_HEADS_PER_STEP = V7X_SUBLANES


def _sample_key_rows(p_len, ts):
    period = max(DILATIONS)
    assert p_len % period == 0 and ts < period
    n_i = p_len // period
    tail_i0 = (p_len - DILATIONS[1] * SUB_WINDOW) // period
    assert tail_i0 >= 0 and DILATIONS[0] * SUB_WINDOW <= DILATIONS[1] * SUB_WINDOW
    main = (np.arange(n_i)[:, None] * period + np.arange(ts)[None, :]).reshape(-1)
    tail = (np.arange(tail_i0, n_i)[:, None] * period + np.arange(ts, period)[None, :]).reshape(-1)
    rows = np.concatenate([main, tail, p_len + np.arange(ts)])
    nkeys = -(-rows.size // V7X_LANES) * V7X_LANES
    return np.concatenate([rows, np.full(nkeys - rows.size, -1)]), n_i, n_i - tail_i0


def _sample_bias_table(rel_bias, nh, p_len, ts, key_rows):
    t = np.arange(ts)[:, None]
    row = key_rows[None, :]
    dist = p_len + t - row
    out = []
    for g, dil in enumerate(DILATIONS):
        j = dist // dil
        ok = (row >= 0) & (dist >= 0) & (dist % dil == 0) & (j <= SUB_WINDOW)
        buckets = _t5_buckets(np.clip(j, 0, SUB_WINDOW) * dil)
        tab = rel_bias[:, g * nh:(g + 1) * nh].astype(F32)
        out.append(jnp.where(ok[None], _bucket_lookup(tab, buckets), NEG))
    return jnp.concatenate(out, axis=1)


def _attn_sample_body(q1_ref, q2_ref, q3_ref, km_ref, kt_ref, vm_ref, vt_ref, kn_ref, vn_ref,
                      bias_ref, o_ref, kall_ref, vall_ref, *, ts):
    nkeys = kall_ref.shape[0]
    n_main = km_ref.shape[0] * km_ref.shape[1]
    n_tail = kt_ref.shape[0] * kt_ref.shape[1]
    n_used = n_main + n_tail + ts
    scale = HEAD ** -0.5
    kall_ref[n_used:nkeys, :] = jnp.zeros((nkeys - n_used, HEAD), F32)
    vall_ref[n_used:nkeys, :] = jnp.zeros((nkeys - n_used, HEAD), F32)
    for hh in range(SAMPLE_HEADS_PER_STEP):
        cols = slice(hh * HEAD, (hh + 1) * HEAD)
        for src_main, src_tail, src_new, dst in ((km_ref, kt_ref, kn_ref, kall_ref),
                                                 (vm_ref, vt_ref, vn_ref, vall_ref)):
            flat_main = src_main.reshape(n_main * SAMPLE_HEADS_PER_STEP, HEAD)
            flat_tail = src_tail.reshape(n_tail * SAMPLE_HEADS_PER_STEP, HEAD)
            dst[0:n_main, :] = flat_main[pl.ds(hh, n_main, stride=SAMPLE_HEADS_PER_STEP), :]
            dst[n_main:n_main + n_tail, :] = flat_tail[pl.ds(hh, n_tail, stride=SAMPLE_HEADS_PER_STEP), :]
            dst[n_main + n_tail:n_used, :] = src_new[:, cols]
        kb = kall_ref[...].astype(BF16)
        vb = vall_ref[...].astype(BF16)
        bias = bias_ref[hh]
        s = []
        for g, q_ref in enumerate((q1_ref, q2_ref, q3_ref)):
            sg = lax.dot_general(q_ref[:, cols].astype(BF16), kb, _NT, preferred_element_type=F32)
            s.append(sg * scale + bias[g * ts:(g + 1) * ts, :])
        mx = [jnp.max(x, axis=-1, keepdims=True) for x in s]
        mm = jnp.maximum(jnp.maximum(mx[0], mx[1]), mx[2])
        p = jnp.exp(s[0] - mm) + jnp.exp(s[1] - mm) + jnp.exp(s[2] - mm)
        den = jnp.sum(p, axis=-1, keepdims=True)
        o = jnp.dot(p.astype(BF16), vb, preferred_element_type=F32)
        o_ref[:, cols] = (o / den).astype(o_ref.dtype)


def _attn_sample(qproj, k_new, v_new, cache_k, cache_v, bias_table, n_i, n_tail, *, nseq, ts):
    m, d3 = qproj.shape
    d = d3 // 3
    nh = d // HEAD
    hps = SAMPLE_HEADS_PER_STEP
    assert nh % hps == 0 and n_i % n_tail == 0
    p_len = cache_k.shape[1]
    period = p_len // n_i
    nkeys = bias_table.shape[-1]
    ngrp = nh // hps
    wide = hps * HEAD
    qspec = lambda g: pl.BlockSpec((ts, wide), lambda n, h: (n, g * ngrp + h))
    newspec = pl.BlockSpec((ts, wide), lambda n, h: (n, h))
    mainspec = pl.BlockSpec((None, n_i, ts, hps, HEAD), lambda n, h: (n, 0, 0, h, 0))
    tailspec = pl.BlockSpec((None, n_tail, period - ts, hps, HEAD),
                            lambda n, h: (n, n_i // n_tail - 1, 1, h, 0))
    assert period - ts == ts
    cache_bytes = _nbytes((n_i + n_tail, ts, hps, HEAD), F32)
    blocks = (2 * (2 * cache_bytes + _nbytes((hps, 3 * ts, nkeys), F32))
              + 2 * _nbytes((nkeys, HEAD), F32) + 2 * _nbytes((nkeys, HEAD), BF16)
              + 8 * _nbytes((3 * ts, nkeys), F32))
    k5 = cache_k.reshape(nseq, n_i, period, nh, HEAD)
    v5 = cache_v.reshape(nseq, n_i, period, nh, HEAD)
    return pl.pallas_call(
        functools.partial(_attn_sample_body, ts=ts),
        grid=(nseq, ngrp),
        in_specs=[qspec(0), qspec(1), qspec(2), mainspec, tailspec, mainspec, tailspec,
                  newspec, newspec,
                  pl.BlockSpec((hps, 3 * ts, nkeys), lambda n, h: (h, 0, 0))],
        out_specs=pl.BlockSpec((ts, wide), lambda n, h: (n, h)),
        out_shape=jax.ShapeDtypeStruct((m, d), F32),
        scratch_shapes=[pltpu.VMEM((nkeys, HEAD), F32), pltpu.VMEM((nkeys, HEAD), F32)],
        compiler_params=pltpu.CompilerParams(
            dimension_semantics=("arbitrary", "arbitrary"),
            vmem_limit_bytes=_vmem_limit(blocks)),
        name="dilated_attention_sample",
    )(qproj, qproj, qproj, k5, k5, v5, v5, k_new, v_new, bias_table)


def _lower_bounds(lb_raw):
    p = jax.nn.softmax(lb_raw.astype(F32), axis=0)
    c = jnp.cumsum(p, axis=0)
    return c - c[0:1]


def kernel(x_prompt, x_sample, state_hgrn, state_ffn_conv, cache_k_win, cache_v_win, a_w_in, a_lb_raw, a_out_gain, a_w_out, kv_norm, w_kv, b_w_q, b_w_out, rel_bias, ffn_w_up, ffn_conv_w, ffn_conv_b, ffn_w_down, norm_mix, norm_ffn, norm_final):
    nb_p, t_p, d = x_prompt.shape
    nb_s, t_s, _ = x_sample.shape
    depth = ffn_w_up.shape[0]
    n_a = a_w_in.shape[0]
    nh = d // HEAD
    f = ffn_w_down.shape[1]
    p_len = cache_k_win.shape[1]
    assert t_p % (SUB_WINDOW * max(DILATIONS)) == 0 and t_s == V7X_SUBLANES

    xp = x_prompt.reshape(nb_p * t_p, d)
    xs = x_sample.reshape(nb_s * t_s, d)
    lbs = _lower_bounds(a_lb_raw)
    key_rows, n_i, n_tail = _sample_key_rows(p_len, t_s)
    bias_p = _prompt_bias_tiles(rel_bias, nh)
    bias_s = _sample_bias_table(rel_bias, nh, p_len, t_s, key_rows)

    hgrn_p, hgrn_s, conv_p, conv_s = [], [], [], []
    kp = vp = ks = vs = None
    w_kv3 = w_kv.reshape(1, d, 2 * d)
    conv_b3 = ffn_conv_b.reshape(depth, 1, f)
    for l in range(depth):
        gains = [norm_mix[l], kv_norm] if l == n_a else [norm_mix[l]]
        hp, *kvp = _rmsnorm(xp, gains, BF16)
        hs, *kvs = _rmsnorm(xs, gains, BF16)
        if l < n_a:
            pp, ps = _linear(hp, hs, a_w_in, l, name="hgrn2_in_proj")
            op, sp = _gla(pp, lbs[l], a_out_gain[l], None, nseq_total=nb_p, tlen=t_p, out_dtype=BF16)
            os_, ss = _gla(ps, lbs[l], a_out_gain[l], state_hgrn[l], nseq_total=nb_s, tlen=t_s,
                           out_dtype=F32)
            hgrn_p.append(sp)
            hgrn_s.append(ss)
            w_o, l_o = a_w_out, l
        else:
            if l == n_a:
                kp, ks = _linear(kvp[0], kvs[0], w_kv3, 0, col0=0, n=d, name="k_proj")
                vp, vs = _linear(kvp[0], kvs[0], w_kv3, 0, col0=d, n=d, name="v_proj")
            jb = l - n_a
            qp, qs = _linear(hp, hs, b_w_q, jb, name="q_proj")
            op = _attn_prompt(qp, kp, vp, bias_p, nseq=nb_p, tlen=t_p)
            os_ = _attn_sample(qs, ks, vs, cache_k_win, cache_v_win, bias_s, n_i, n_tail,
                               nseq=nb_s, ts=t_s)
            w_o, l_o = b_w_out, jb
        xp, xs = _linear(op, os_, w_o, l_o, res=(xp, xs), name="mixer_out_proj")

        hp, = _rmsnorm(xp, [norm_ffn[l]], BF16)
        hs, = _rmsnorm(xs, [norm_ffn[l]], BF16)
        prev = state_ffn_conv[l]
        ext_s = jnp.pad(prev, ((0, 0), (0, t_s - (CONV_W - 1)), (0, 0))).reshape(nb_s * t_s, f)
        actp, acts, tailp, a_s = _ffn_up(hp, hs, ffn_w_up, ffn_conv_w, conv_b3, ext_s, l,
                                         tlen=t_p, ts=t_s)
        conv_p.append(tailp.reshape(nb_p, 8, f)[:, 8 - (CONV_W - 1):, :])
        conv_s.append(a_s.reshape(nb_s, t_s, f)[:, t_s - (CONV_W - 1):, :])
        xp, xs = _linear(actp, acts, ffn_w_down, l, res=(xp, xs), name="ffn_down_proj")

    yp = _rmsnorm(xp, [norm_final], F32)[0].reshape(nb_p, t_p, d)
    ys = _rmsnorm(xs, [norm_final], F32)[0].reshape(nb_s, t_s, d)
    keep = min(BUCKET_MAX_DIST, t_p)
    k_win = kp.reshape(nb_p, t_p, nh, HEAD)[:, t_p - keep:]
    v_win = vp.reshape(nb_p, t_p, nh, HEAD)[:, t_p - keep:]
    return (yp, ys, jnp.stack(hgrn_p), jnp.stack(hgrn_s), jnp.stack(conv_p), jnp.stack(conv_s),
            k_win, v_win, ks.reshape(nb_s, t_s, nh, HEAD), vs.reshape(nb_s, t_s, nh, HEAD))
```

```python
import functools
import math

import numpy as np
import jax
import jax.numpy as jnp
from jax import lax
from jax.experimental import pallas as pl
from jax.experimental.pallas import tpu as pltpu

F32 = jnp.float32
BF16 = jnp.bfloat16

V7X_VMEM_BYTES = 64 * 1024 * 1024
V7X_LANES = 128
V7X_SUBLANES = 8
COMPILER_SCRATCH_BYTES = 12 * 1024 * 1024

HEAD = 128
GLA_CHUNK = 128
GLA_SEQ_UNROLL = 8
GLA_HEADS_PER_STEP = 2
GLA_CHUNK_UNROLL = 8
SUB_WINDOW = 128
DILATIONS = (1, 4, 16)
N_BUCKETS = 32
BUCKET_MAX_DIST = 2048
CONV_W = 3
EPS = 1e-6
NEG = -1e30
LOG2_E = 1.4426950408889634


def _vmem_limit(block_bytes):
    return int(min(block_bytes + COMPILER_SCRATCH_BYTES, V7X_VMEM_BYTES - (2 << 20)))


def _nbytes(shape, dtype):
    return int(np.prod(shape)) * jnp.dtype(dtype).itemsize


def _rmsnorm_body(x_ref, *refs):
    g_refs, o_refs = refs[:len(refs) // 2], refs[len(refs) // 2:]
    x = x_ref[...]
    ms = jnp.mean(x * x, axis=-1, keepdims=True)
    y = x * lax.rsqrt(ms + EPS)
    for g_ref, o_ref in zip(g_refs, o_refs):
        o_ref[...] = (y * g_ref[...]).astype(o_ref.dtype)


def _rmsnorm(x, gains, out_dtype):
    m, d = x.shape
    tm = min(m, 256)
    assert m % tm == 0
    n = len(gains)
    blocks = 2 * (_nbytes((tm, d), F32) + n * _nbytes((tm, d), out_dtype))
    row_spec = pl.BlockSpec((tm, d), lambda i: (i, 0))
    return pl.pallas_call(
        _rmsnorm_body,
        grid=(m // tm,),
        in_specs=[row_spec] + [pl.BlockSpec((1, d), lambda i: (0, 0))] * n,
        out_specs=[row_spec] * n,
        out_shape=[jax.ShapeDtypeStruct((m, d), out_dtype)] * n,
        compiler_params=pltpu.CompilerParams(vmem_limit_bytes=_vmem_limit(blocks)),
        name="rmsnorm",
    )(x, *[g.reshape(1, d) for g in gains])


def _linear_body(*refs, layer, col0, kc, nchunks, spc, has_res):
    if has_res:
        xp_ref, xs_ref, w_hbm, rp_ref, rs_ref, yp_ref, ys_ref, wb_ref, stg_ref, sem = refs
    else:
        xp_ref, xs_ref, w_hbm, yp_ref, ys_ref, wb_ref, stg_ref, sem = refs
    j = pl.program_id(0)
    i = pl.program_id(1)
    nj = pl.num_programs(0)
    tn = wb_ref.shape[2]
    cur = j % 2

    def chunk_copy(jt, c):
        rows = pl.ds(pl.multiple_of(c * kc, kc), kc)
        cols = pl.ds(pl.multiple_of(col0 + jt * tn, tn), tn)
        return pltpu.make_async_copy(w_hbm.at[layer, rows, cols], stg_ref.at[c % 2], sem.at[c % 2])

    def land(jt, c, wslot):
        chunk_copy(jt, c).wait()
        rows = pl.ds(pl.multiple_of(c * kc, kc), kc)
        wb_ref[wslot, rows, :] = stg_ref[c % 2].astype(BF16)

    @pl.when(jnp.logical_and(j == 0, i == 0))
    def _first_tile():
        chunk_copy(0, 0).start()

        def step(c, carry):
            @pl.when(c + 1 < nchunks)
            def _():
                chunk_copy(0, c + 1).start()
            land(0, c, 0)
            return carry

        lax.fori_loop(0, nchunks, step, 0)

    @pl.when(jnp.logical_and(j > 0, i == 0))
    def _finish_prefetched_tile():
        land(j, nchunks - 1, cur)

    @pl.when(jnp.logical_and(j + 1 < nj, i % spc == 0))
    def _prefetch_next_tile():
        c = i // spc
        chunk_copy(j + 1, c).start()

        @pl.when(c > 0)
        def _():
            land(j + 1, c - 1, 1 - cur)

    @pl.when(i == 0)
    def _sample_rows():
        ys = jnp.dot(xs_ref[...].astype(BF16), wb_ref[cur], preferred_element_type=F32)
        if has_res:
            ys = ys + rs_ref[...]
        ys_ref[...] = ys

    yp = jnp.dot(xp_ref[...], wb_ref[cur], preferred_element_type=F32)
    if has_res:
        yp = yp + rp_ref[...]
    yp_ref[...] = yp


BF16_SUBLANES = 16
MAX_W_CHUNKS = 16


def _linear_tiles(mp, k, n, col0):
    tm, tn_max = (512, 1024) if k <= 4096 else (256, 512)
    tm = min(tm, mp)
    tn = next(t for t in (tn_max, tn_max // 2, tn_max // 4, V7X_LANES)
              if n % t == 0 and col0 % t == 0)
    ni = mp // tm
    nchunks = next(c for c in range(min(MAX_W_CHUNKS, ni), 0, -1)
                   if ni % c == 0 and k % (c * BF16_SUBLANES) == 0)
    return tm, tn, k // nchunks, nchunks, ni // nchunks


def _linear(xp, xs, w, layer, *, col0=0, n=None, res=None, name):
    mp, k = xp.shape
    ms = xs.shape[0]
    n = w.shape[2] if n is None else n
    tm, tn, kc, nchunks, spc = _linear_tiles(mp, k, n, col0)
    assert mp % tm == 0 and n % tn == 0
    in_specs = [pl.BlockSpec((tm, k), lambda j, i: (i, 0)),
                pl.BlockSpec((ms, k), lambda j, i: (0, 0)),
                pl.BlockSpec(memory_space=pl.ANY)]
    args = [xp, xs, w]
    blocks = (2 * (_nbytes((tm, k), BF16) + _nbytes((ms, k), xs.dtype)
                   + _nbytes((tm, tn), F32) + _nbytes((ms, tn), F32))
              + 2 * _nbytes((k, tn), BF16) + 2 * _nbytes((kc, tn), F32))
    if res is not None:
        in_specs += [pl.BlockSpec((tm, tn), lambda j, i: (i, j)),
                     pl.BlockSpec((ms, tn), lambda j, i: (0, j))]
        args += list(res)
        blocks += 2 * (_nbytes((tm, tn), F32) + _nbytes((ms, tn), F32))
    return pl.pallas_call(
        functools.partial(_linear_body, layer=layer, col0=col0, kc=kc, nchunks=nchunks, spc=spc,
                          has_res=res is not None),
        grid=(n // tn, mp // tm),
        in_specs=in_specs,
        out_specs=[pl.BlockSpec((tm, tn), lambda j, i: (i, j)),
                   pl.BlockSpec((ms, tn), lambda j, i: (0, j))],
        out_shape=[jax.ShapeDtypeStruct((mp, n), F32), jax.ShapeDtypeStruct((ms, n), F32)],
        scratch_shapes=[pltpu.VMEM((2, k, tn), BF16), pltpu.VMEM((2, kc, tn), F32),
                        pltpu.SemaphoreType.DMA((2,))],
        compiler_params=pltpu.CompilerParams(
            dimension_semantics=("arbitrary", "arbitrary"),
            vmem_limit_bytes=_vmem_limit(blocks)),
        name=name,
    )(*args)


FFN_STEP_ROWS = 1024
FFN_SUB_ROWS = 256


def _ffn_up_body(hp_ref, hs_ref, w_hbm, cw_ref, cb_ref, exts_ref,
                 actp_ref, acts_ref, tailp_ref, as_ref,
                 wcat_ref, stg_ref, sem, ext_ref, *, layer, f, kc, nchunks, tm, tlen, ts):
    j = pl.program_id(0)
    i = pl.program_id(1)
    nj = pl.num_programs(0)
    tf = cw_ref.shape[1]
    cur = j % 2
    cw = cw_ref[...]
    cb = cb_ref[...]

    def act_fn(a2, a1, a0, b):
        conv = cb + cw[0:1, :] * a2 + cw[1:2, :] * a1 + cw[2:3, :] * a0
        return jax.nn.silu(conv) * b

    def chunk_copy(jt, c, half):
        rows = pl.ds(pl.multiple_of(c * kc, kc), kc)
        cols = pl.ds(pl.multiple_of(half * f + jt * tf, tf), tf)
        return pltpu.make_async_copy(w_hbm.at[layer, rows, cols], stg_ref.at[c % 2, half],
                                     sem.at[c % 2, half])

    def start(jt, c):
        for half in (0, 1):
            chunk_copy(jt, c, half).start()

    def land(jt, c, wslot):
        rows = pl.ds(pl.multiple_of(c * kc, kc), kc)
        for half in (0, 1):
            chunk_copy(jt, c, half).wait()
            wcat_ref[wslot, rows, half * tf:(half + 1) * tf] = stg_ref[c % 2, half].astype(BF16)

    @pl.when(jnp.logical_and(j == 0, i == 0))
    def _first_tile():
        start(0, 0)

        def step(c, carry):
            @pl.when(c + 1 < nchunks)
            def _():
                start(0, c + 1)
            land(0, c, 0)
            return carry

        lax.fori_loop(0, nchunks, step, 0)

    @pl.when(jnp.logical_and(j > 0, i == 0))
    def _finish_prefetched_tile():
        land(j, nchunks - 1, cur)

    @pl.when(j + 1 < nj)
    def _prefetch_next_tile():
        start(j + 1, i)

        @pl.when(i > 0)
        def _():
            land(j + 1, i - 1, 1 - cur)

    @pl.when(i == 0)
    def _sample_rows():
        ab = jnp.dot(hs_ref[...].astype(BF16), wcat_ref[cur], preferred_element_type=F32)
        a, b = ab[:, 0:tf], ab[:, tf:2 * tf]
        ms = a.shape[0]
        t = lax.broadcasted_iota(jnp.int32, a.shape, 0) % ts
        ext = exts_ref[...]
        a1 = jnp.where(t == 0, pltpu.roll(ext, ms - 1, axis=0), pltpu.roll(a, 1, axis=0))
        a2 = jnp.where(t < 2, ext, pltpu.roll(a, 2, axis=0))
        acts_ref[...] = act_fn(a2, a1, a, b).astype(acts_ref.dtype)
        as_ref[...] = a

    @pl.when((i * tm) % tlen == 0)
    def _sequence_start():
        ext_ref[...] = jnp.zeros(ext_ref.shape, F32)

    tail = ext_ref[...]
    for s in range(tm // FFN_SUB_ROWS):
        rows = slice(s * FFN_SUB_ROWS, (s + 1) * FFN_SUB_ROWS)
        hp = hp_ref[rows, :]
        ab = jnp.dot(hp, wcat_ref[cur], preferred_element_type=F32)
        a, b = ab[:, 0:tf], ab[:, tf:2 * tf]
        ext = jnp.concatenate([tail, a], axis=0)
        a1 = ext[7:7 + FFN_SUB_ROWS, :]
        a2 = ext[6:6 + FFN_SUB_ROWS, :]
        actp_ref[rows, :] = act_fn(a2, a1, a, b).astype(actp_ref.dtype)
        tail = a[FFN_SUB_ROWS - 8:FFN_SUB_ROWS, :]
    ext_ref[...] = tail

    @pl.when(((i + 1) * tm) % tlen == 0)
    def _sequence_end():
        n = (i * tm) // tlen
        tailp_ref[pl.ds(pl.multiple_of(n * 8, 8), 8), :] = tail


def _ffn_up(hp, hs, w_up, conv_w, conv_b, ext_s, layer, *, tlen, ts):
    mp, d = hp.shape
    ms = hs.shape[0]
    f = w_up.shape[2] // 2
    nseq = mp // tlen
    tm = min(FFN_STEP_ROWS, tlen)
    tf = 256
    assert mp % tm == 0 and tlen % tm == 0 and f % tf == 0 and tm % FFN_SUB_ROWS == 0
    nf = f // tf
    nchunks = mp // tm
    assert d % (nchunks * BF16_SUBLANES) == 0
    kc = d // nchunks
    blocks = (2 * (_nbytes((tm, d), BF16) + _nbytes((ms, d), hs.dtype)
                   + _nbytes((tm, tf), BF16) + 3 * _nbytes((ms, tf), F32) + _nbytes((nseq * 8, tf), F32))
              + 2 * _nbytes((d, 2 * tf), BF16) + 4 * _nbytes((kc, tf), F32)
              + _nbytes((8, tf), F32) + 5 * _nbytes((tm, tf), F32))
    return pl.pallas_call(
        functools.partial(_ffn_up_body, layer=layer, f=f, kc=kc, nchunks=nchunks, tm=tm, tlen=tlen,
                          ts=ts),
        grid=(nf, mp // tm),
        in_specs=[pl.BlockSpec((tm, d), lambda j, i: (i, 0)),
                  pl.BlockSpec((ms, d), lambda j, i: (0, 0)),
                  pl.BlockSpec(memory_space=pl.ANY),
                  pl.BlockSpec((None, CONV_W, tf), lambda j, i: (layer, 0, j)),
                  pl.BlockSpec((None, 1, tf), lambda j, i: (layer, 0, j)),
                  pl.BlockSpec((ms, tf), lambda j, i: (0, j))],
        out_specs=[pl.BlockSpec((tm, tf), lambda j, i: (i, j)),
                   pl.BlockSpec((ms, tf), lambda j, i: (0, j)),
                   pl.BlockSpec((nseq * 8, tf), lambda j, i: (0, j)),
                   pl.BlockSpec((ms, tf), lambda j, i: (0, j))],
        out_shape=[jax.ShapeDtypeStruct((mp, f), BF16),
                   jax.ShapeDtypeStruct((ms, f), BF16),
                   jax.ShapeDtypeStruct((nseq * 8, f), F32),
                   jax.ShapeDtypeStruct((ms, f), F32)],
        scratch_shapes=[pltpu.VMEM((2, d, 2 * tf), BF16), pltpu.VMEM((2, 2, kc, tf), F32),
                        pltpu.SemaphoreType.DMA((2, 2)), pltpu.VMEM((8, tf), F32)],
        compiler_params=pltpu.CompilerParams(
            dimension_semantics=("arbitrary", "arbitrary"),
            vmem_limit_bytes=_vmem_limit(blocks)),
        name="ffn_up",
    )(hp, hs, w_up, conv_w, conv_b, ext_s)


def _level_table(c):
    t = np.arange(c)[:, None]
    s = np.arange(c)[None, :]
    x = np.maximum(t ^ s, 1)
    lev = np.floor(np.log2(x)).astype(np.int32)
    return np.where(t > s, lev, np.where(t == s, -1, -2)).astype(np.int32)


def _cumsum_table(c):
    return np.tril(np.ones((c, c), np.float32))


def _split3(x):
    x0 = x.astype(BF16)
    r1 = x - x0.astype(F32)
    x1 = r1.astype(BF16)
    x2 = (r1 - x1.astype(F32)).astype(BF16)
    return x0, x1, x2


def _block_boundary_rows(b, blk):
    c, w = b.shape
    half = blk // 2
    if blk >= 2 * V7X_SUBLANES:
        x = b.reshape(c // blk, blk, w)
        r = jnp.broadcast_to(x[:, half - 1:half, :], x.shape)
        return r.reshape(c, w)
    x = b.reshape(c // V7X_SUBLANES, V7X_SUBLANES, w)
    sub = lax.broadcasted_iota(jnp.int32, x.shape, 1)

    def row(i):
        return jnp.broadcast_to(x[:, i:i + 1, :], x.shape)

    if blk == 8:
        r = row(3)
    elif blk == 4:
        r = jnp.where(sub < 4, row(1), row(5))
    else:
        r = jnp.where(sub < 2, row(0), jnp.where(sub < 4, row(2), jnp.where(sub < 6, row(4), row(6))))
    return r.reshape(c, w)


_NT = (((1,), (1,)), ((), ()))
_TN = (((0,), (0,)), ((), ()))


def _gla_body(*refs, nseq, tlen, valid, has_s0, heads):
    if has_s0:
        (q_ref, f_ref, v_ref, g_ref, lb_ref, gain_ref, lev_ref, tri_ref, s0_ref,
         o_ref, so_ref) = refs
    else:
        q_ref, f_ref, v_ref, g_ref, lb_ref, gain_ref, lev_ref, tri_ref, o_ref, so_ref = refs
    for hh in range(heads):
        _gla_head(q_ref, f_ref, v_ref, g_ref, lb_ref, gain_ref, lev_ref, tri_ref,
                  s0_ref if has_s0 else None, o_ref, so_ref, hh, nseq=nseq, tlen=tlen, valid=valid)


def _gla_head(q_ref, f_ref, v_ref, g_ref, lb_ref, gain_ref, lev_ref, tri_ref, s0_ref, o_ref, so_ref,
              hh, *, nseq, tlen, valid):
    has_s0 = s0_ref is not None
    cols = slice(hh * HEAD, (hh + 1) * HEAD)
    c = GLA_CHUNK
    rows_in = min(c, valid)
    nchunk = max(tlen // c, 1)
    lb = lb_ref[:, cols]
    log_lb = jnp.log(lb)
    log1m_lb = jnp.log1p(-lb)
    one_m_lb = 1.0 - lb
    gain = gain_ref[:, cols]
    n_levels = int(math.log2(rows_in))
    assert 1 << n_levels == rows_in

    def pad_rows(x):
        if rows_in == c:
            return x
        return jnp.concatenate([x, jnp.zeros((c - rows_in, x.shape[1]), x.dtype)], axis=0)

    def chunk(n, ci, st):
        row0 = pl.multiple_of(n * tlen + ci * rows_in, rows_in)
        rows = pl.ds(row0, rows_in)
        qr, fz, v = q_ref[rows, cols], f_ref[rows, cols], v_ref[rows, cols]
        e = jnp.exp(-jnp.abs(fz))
        one_p_e = 1.0 + e
        log_sig = jnp.minimum(fz, 0.0) - jnp.log(one_p_e)
        u = log1m_lb + log_sig
        log_f = jnp.maximum(log_lb, u) + jnp.log(1.0 + jnp.exp(-jnp.abs(log_lb - u)))
        k = one_m_lb * (jnp.where(fz > 0.0, e, 1.0) / one_p_e)
        q = jax.nn.silu(qr) * (HEAD ** -0.5)
        vb = pad_rows(v.astype(BF16))

        tri = tri_ref[...]
        b = sum(jnp.dot(tri, pad_rows(part), preferred_element_type=F32) for part in _split3(log_f))
        b = b * LOG2_E
        b_last = b[c - 1:c, :]
        b = b[0:rows_in, :]

        lev = lev_ref[...]
        diag = pad_rows(jnp.sum(q * k, axis=-1, keepdims=True))
        att = jnp.where(lev == -1, diag, 0.0)
        for l in range(n_levels):
            r = _block_boundary_rows(b, 2 << l)
            decay = jnp.exp2(-jnp.abs(b - r))
            qt = pad_rows((q * decay).astype(BF16))
            kt = pad_rows((k * decay).astype(BF16))
            s_l = lax.dot_general(qt, kt, _NT, preferred_element_type=F32)
            att = jnp.where(lev == l, s_l, att)
        o = jnp.dot(att.astype(BF16), vb, preferred_element_type=F32)
        o = o + lax.dot_general(pad_rows((q * jnp.exp2(b)).astype(BF16)), st.astype(BF16), _NT,
                                preferred_element_type=F32)
        kd = pad_rows((k * jnp.exp2(b_last - b)).astype(BF16))
        st_new = st * jnp.exp2(b_last) + lax.dot_general(vb, kd, _TN, preferred_element_type=F32)

        o = o[0:rows_in, :]
        o = o * lax.rsqrt(jnp.mean(o * o, axis=-1, keepdims=True) + EPS) * gain
        o = o * jax.nn.silu(g_ref[rows, cols])
        o_ref[rows, cols] = o.astype(o_ref.dtype)
        return st_new

    def sequence(n, carry):
        if has_s0:
            st0 = s0_ref[n, hh].T
        else:
            st0 = jnp.zeros((HEAD, HEAD), F32)
        st = lax.fori_loop(0, nchunk, lambda ci, st: chunk(n, ci, st), st0,
                           unroll=min(GLA_CHUNK_UNROLL, nchunk))
        so_ref[n, hh] = st.T
        return carry

    lax.fori_loop(0, nseq, sequence, 0, unroll=min(nseq, GLA_SEQ_UNROLL))


def _gla(proj, lb, gain, s0, *, nseq_total, tlen, out_dtype):
    m, d4 = proj.shape
    d = d4 // 4
    nh = d // HEAD
    nseq = 1 if tlen >= GLA_CHUNK else nseq_total
    assert tlen % GLA_CHUNK == 0 or tlen < GLA_CHUNK
    rows = nseq * tlen
    hps = GLA_HEADS_PER_STEP if nh % GLA_HEADS_PER_STEP == 0 else 1
    wide = hps * HEAD
    ngrp = nh // hps
    col = lambda off: pl.BlockSpec((rows, wide), lambda n, h: (n, h + off))
    vec = pl.BlockSpec((1, wide), lambda n, h: (0, h))
    const = pl.BlockSpec((GLA_CHUNK, GLA_CHUNK), lambda n, h: (0, 0))
    in_specs = [col(0), col(ngrp), col(2 * ngrp), col(3 * ngrp), vec, vec, const, const]
    args = [proj, proj, proj, proj, lb.reshape(1, d), gain.reshape(1, d),
            jnp.asarray(_level_table(GLA_CHUNK)), jnp.asarray(_cumsum_table(GLA_CHUNK), BF16)]
    state_spec = pl.BlockSpec((nseq, hps, HEAD, HEAD), lambda n, h: (n, h, 0, 0))
    if s0 is not None:
        in_specs.append(state_spec)
        args.append(s0)
    blocks = 2 * (4 * _nbytes((rows, wide), F32) + _nbytes((rows, wide), out_dtype)
                  + 2 * _nbytes((nseq, hps, HEAD, HEAD), F32)) + (4 << 20)
    return pl.pallas_call(
        functools.partial(_gla_body, nseq=nseq, tlen=tlen, valid=tlen, has_s0=s0 is not None,
                          heads=hps),
        grid=(nseq_total // nseq, ngrp),
        in_specs=in_specs,
        out_specs=[pl.BlockSpec((rows, wide), lambda n, h: (n, h)), state_spec],
        out_shape=[jax.ShapeDtypeStruct((m, d), out_dtype),
                   jax.ShapeDtypeStruct((nseq_total, nh, HEAD, HEAD), F32)],
        compiler_params=pltpu.CompilerParams(
            dimension_semantics=("arbitrary", "arbitrary"),
            vmem_limit_bytes=_vmem_limit(blocks)),
        name="hgrn2_recurrence",
    )(*args)


def _t5_buckets(dist):
    max_exact = N_BUCKETS // 2
    n = np.asarray(dist, dtype=np.int64)
    large = max_exact + (np.log(np.maximum(n, 1) / max_exact)
                         / np.log(BUCKET_MAX_DIST / max_exact)
                         * (N_BUCKETS - max_exact)).astype(np.int64)
    large = np.minimum(large, N_BUCKETS - 1)
    return np.where(n < max_exact, n, large).astype(np.int32)


def _bucket_lookup(tab, buckets):
    flat = jnp.asarray(buckets.reshape(1, -1))
    onehot = (flat == lax.broadcasted_iota(jnp.int32, (N_BUCKETS, flat.shape[1]), 0)).astype(F32)
    out = jnp.einsum("bh,bx->hx", tab, onehot, precision=lax.Precision.HIGHEST)
    return out.reshape((tab.shape[1],) + buckets.shape)


def _prompt_bias_tiles(rel_bias, nh):
    bl = SUB_WINDOW
    off = np.arange(bl)[:, None] + bl - np.arange(2 * bl)[None, :]
    band = (off >= 0) & (off <= SUB_WINDOW)
    tiles = []
    for g, dil in enumerate(DILATIONS):
        buckets = _t5_buckets(np.clip(off, 0, SUB_WINDOW) * dil)
        tab = rel_bias[:, g * nh:(g + 1) * nh].astype(F32)
        tiles.append(jnp.where(band[None], _bucket_lookup(tab, buckets), NEG))
    return jnp.stack(tiles)


ATTN_BLOCK_UNROLL = 8


def _attn_prompt_body(q1_ref, q2_ref, q3_ref, k_ref, v_ref, bias_ref, o_ref,
                      acc_ref, lse_ref, k4_ref, v4_ref, q34_ref, *, tlen):
    bl = SUB_WINDOW
    scale = HEAD ** -0.5
    mid = DILATIONS[1]
    seg = tlen // mid

    for src_ref, dst_ref in ((k_ref, k4_ref), (v_ref, v4_ref), (q3_ref, q34_ref)):
        for r in range(mid):
            dst_ref[r * seg:(r + 1) * seg, :] = src_ref[pl.ds(r, seg, stride=mid), :]

    for g, (dil, q_ref) in enumerate(zip(DILATIONS, (q1_ref, q2_ref, q3_ref))):
        nblk = tlen // (dil * bl)
        with_prev = nblk > 1

        def block(it, carry, dil=dil, q_ref=q_ref, nblk=nblk, g=g, with_prev=with_prev):
            r = it // nblk
            bi = it % nblk
            start = r + dil * bl * bi
            qrows = pl.ds(start, bl, stride=dil) if dil > 1 else pl.ds(start, bl)
            if with_prev:
                nkey = 2 * bl
                bias = bias_ref[g, 0]
                bias_first = jnp.concatenate([bias[:, bl:2 * bl], jnp.full((bl, bl), NEG, F32)], axis=1)
                bias = jnp.where(bi == 0, bias_first, bias)
                back = jnp.where(bi > 0, bl, 0)
            else:
                nkey = bl
                bias = bias_ref[g, 0, :, bl:2 * bl]
                back = 0
            if dil == 1:
                krows = pl.ds(start - back, nkey)
                qb, k2, v2 = q_ref[qrows, :], k_ref[krows, :], v_ref[krows, :]
            elif dil == mid:
                krows = pl.ds(r * seg + bl * bi - back, nkey)
                qb, k2, v2 = q_ref[qrows, :], k4_ref[krows, :], v4_ref[krows, :]
            else:
                sub = dil // mid
                rows4 = pl.ds((r % mid) * seg + r // mid + sub * (bl * bi - back), nkey, stride=sub)
                q4rows = pl.ds((r % mid) * seg + r // mid + sub * bl * bi, bl, stride=sub)
                qb, k2, v2 = q34_ref[q4rows, :], k4_ref[rows4, :], v4_ref[rows4, :]
            qb, k2, v2 = qb.astype(BF16), k2.astype(BF16), v2.astype(BF16)
            s = lax.dot_general(qb, k2, _NT, preferred_element_type=F32) * scale + bias
            mx = jnp.max(s, axis=-1, keepdims=True)
            p = jnp.exp(s - mx)
            sm = jnp.sum(p, axis=-1, keepdims=True)
            acc = jnp.dot(p.astype(BF16), v2, preferred_element_type=F32)
            acc_ref[g, qrows, :] = acc / sm
            lse_ref[g, qrows, :] = jnp.broadcast_to(mx + jnp.log(sm), (bl, HEAD))
            return carry

        def residue_pair(it, carry, dil=dil, g=g):
            sub = dil // mid
            parts = []
            for r in (2 * it, 2 * it + 1):
                rows4 = pl.ds((r % mid) * seg + r // mid, bl, stride=sub)
                parts.append((q34_ref[rows4, :], k4_ref[rows4, :], v4_ref[rows4, :],
                              pl.ds(r, bl, stride=dil)))
            qb = jnp.concatenate([pt[0] for pt in parts], axis=0).astype(BF16)
            k2 = jnp.concatenate([pt[1] for pt in parts], axis=0).astype(BF16)
            v2 = jnp.concatenate([pt[2] for pt in parts], axis=0).astype(BF16)
            own = bias_ref[g, 0, :, bl:2 * bl]
            neg = jnp.full((bl, bl), NEG, F32)
            bias = jnp.concatenate([jnp.concatenate([own, neg], axis=1),
                                    jnp.concatenate([neg, own], axis=1)], axis=0)
            s = lax.dot_general(qb, k2, _NT, preferred_element_type=F32) * scale + bias
            mx = jnp.max(s, axis=-1, keepdims=True)
            p = jnp.exp(s - mx)
            sm = jnp.sum(p, axis=-1, keepdims=True)
            o = jnp.dot(p.astype(BF16), v2, preferred_element_type=F32) / sm
            lse = jnp.broadcast_to(mx + jnp.log(sm), (2 * bl, HEAD))
            for half, pt in enumerate(parts):
                acc_ref[g, pt[3], :] = o[half * bl:(half + 1) * bl, :]
                lse_ref[g, pt[3], :] = lse[half * bl:(half + 1) * bl, :]
            return carry

        if with_prev:
            lax.fori_loop(0, dil * nblk, block, 0, unroll=ATTN_BLOCK_UNROLL)
        else:
            lax.fori_loop(0, dil // 2, residue_pair, 0, unroll=ATTN_BLOCK_UNROLL // 2)

    def merge(ci, carry):
        rows = pl.ds(pl.multiple_of(ci * bl, bl), bl)
        l1, l2, l3 = lse_ref[0, rows, :], lse_ref[1, rows, :], lse_ref[2, rows, :]
        mm = jnp.maximum(jnp.maximum(l1, l2), l3)
        w1, w2, w3 = jnp.exp(l1 - mm), jnp.exp(l2 - mm), jnp.exp(l3 - mm)
        num = w1 * acc_ref[0, rows, :] + w2 * acc_ref[1, rows, :] + w3 * acc_ref[2, rows, :]
        o_ref[rows, :] = (num / (w1 + w2 + w3)).astype(o_ref.dtype)
        return carry

    lax.fori_loop(0, tlen // bl, merge, 0, unroll=2)


def _attn_prompt(qproj, k, v, bias_tiles, *, nseq, tlen):
    m, d3 = qproj.shape
    d = d3 // 3
    nh = d // HEAD
    qspec = lambda g: pl.BlockSpec((tlen, HEAD), lambda n, h: (n, g * nh + h))
    kvspec = pl.BlockSpec((tlen, HEAD), lambda n, h: (n, h))
    blocks = (2 * (5 * _nbytes((tlen, HEAD), F32) + _nbytes((3, 1, SUB_WINDOW, 2 * SUB_WINDOW), F32)
                   + _nbytes((tlen, HEAD), BF16))
              + 9 * _nbytes((tlen, HEAD), F32))
    return pl.pallas_call(
        functools.partial(_attn_prompt_body, tlen=tlen),
        grid=(nseq, nh),
        in_specs=[qspec(0), qspec(1), qspec(2), kvspec, kvspec,
                  pl.BlockSpec((3, 1, SUB_WINDOW, 2 * SUB_WINDOW), lambda n, h: (0, h, 0, 0))],
        out_specs=pl.BlockSpec((tlen, HEAD), lambda n, h: (n, h)),
        out_shape=jax.ShapeDtypeStruct((m, d), BF16),
        scratch_shapes=[pltpu.VMEM((3, tlen, HEAD), F32), pltpu.VMEM((3, tlen, HEAD), F32),
                        pltpu.VMEM((tlen, HEAD), F32), pltpu.VMEM((tlen, HEAD), F32),
                        pltpu.VMEM((tlen, HEAD), F32)],
        compiler_params=pltpu.CompilerParams(
            dimension_semantics=("arbitrary", "arbitrary"),
            vmem_limit_bytes=_vmem_limit(blocks)),
        name="dilated_attention_prompt",
    )(qproj, qproj, qproj, k, v, bias_tiles)


SAMPLE_HEADS_PER_STEP = V7X_SUBLANES


def _sample_key_rows(p_len, ts):
    period = max(DILATIONS)
    assert p_len % period == 0 and ts < period
    n_i = p_len // period
    tail_i0 = (p_len - DILATIONS[1] * SUB_WINDOW) // period
    assert tail_i0 >= 0 and DILATIONS[0] * SUB_WINDOW <= DILATIONS[1] * SUB_WINDOW
    main = (np.arange(n_i)[:, None] * period + np.arange(ts)[None, :]).reshape(-1)
    tail = (np.arange(tail_i0, n_i)[:, None] * period + np.arange(ts, period)[None, :]).reshape(-1)
    rows = np.concatenate([main, tail, p_len + np.arange(ts)])
    nkeys = -(-rows.size // V7X_LANES) * V7X_LANES
    return np.concatenate([rows, np.full(nkeys - rows.size, -1)]), n_i, n_i - tail_i0


def _sample_bias_table(rel_bias, nh, p_len, ts, key_rows):
    t = np.arange(ts)[:, None]
    row = key_rows[None, :]
    dist = p_len + t - row
    out = []
    for g, dil in enumerate(DILATIONS):
        j = dist // dil
        ok = (row >= 0) & (dist >= 0) & (dist % dil == 0) & (j <= SUB_WINDOW)
        buckets = _t5_buckets(np.clip(j, 0, SUB_WINDOW) * dil)
        tab = rel_bias[:, g * nh:(g + 1) * nh].astype(F32)
        out.append(jnp.where(ok[None], _bucket_lookup(tab, buckets), NEG))
    return jnp.concatenate(out, axis=1)


def _attn_sample_body(q1_ref, q2_ref, q3_ref, km_ref, kt_ref, vm_ref, vt_ref, kn_ref, vn_ref,
                      bias_ref, o_ref, kall_ref, vall_ref, *, ts):
    nkeys = kall_ref.shape[0]
    n_main = km_ref.shape[0] * km_ref.shape[1]
    n_tail = kt_ref.shape[0] * kt_ref.shape[1]
    n_used = n_main + n_tail + ts
    scale = HEAD ** -0.5
    kall_ref[n_used:nkeys, :] = jnp.zeros((nkeys - n_used, HEAD), F32)
    vall_ref[n_used:nkeys, :] = jnp.zeros((nkeys - n_used, HEAD), F32)
    for hh in range(SAMPLE_HEADS_PER_STEP):
        cols = slice(hh * HEAD, (hh + 1) * HEAD)
        for src_main, src_tail, src_new, dst in ((km_ref, kt_ref, kn_ref, kall_ref),
                                                 (vm_ref, vt_ref, vn_ref, vall_ref)):
            flat_main = src_main.reshape(n_main * SAMPLE_HEADS_PER_STEP, HEAD)
            flat_tail = src_tail.reshape(n_tail * SAMPLE_HEADS_PER_STEP, HEAD)
            dst[0:n_main, :] = flat_main[pl.ds(hh, n_main, stride=SAMPLE_HEADS_PER_STEP), :]
            dst[n_main:n_main + n_tail, :] = flat_tail[pl.ds(hh, n_tail, stride=SAMPLE_HEADS_PER_STEP), :]
            dst[n_main + n_tail:n_used, :] = src_new[:, cols]
        kb = kall_ref[...].astype(BF16)
        vb = vall_ref[...].astype(BF16)
        bias = bias_ref[hh]
        s = []
        for g, q_ref in enumerate((q1_ref, q2_ref, q3_ref)):
            sg = lax.dot_general(q_ref[:, cols].astype(BF16), kb, _NT, preferred_element_type=F32)
            s.append(sg * scale + bias[g * ts:(g + 1) * ts, :])
        mx = [jnp.max(x, axis=-1, keepdims=True) for x in s]
        mm = jnp.maximum(jnp.maximum(mx[0], mx[1]), mx[2])
        p = jnp.exp(s[0] - mm) + jnp.exp(s[1] - mm) + jnp.exp(s[2] - mm)
        den = jnp.sum(p, axis=-1, keepdims=True)
        o = jnp.dot(p.astype(BF16), vb, preferred_element_type=F32)
        o_ref[:, cols] = (o / den).astype(o_ref.dtype)


def _attn_sample(qproj, k_new, v_new, cache_k, cache_v, bias_table, n_i, n_tail, *, nseq, ts):
    m, d3 = qproj.shape
    d = d3 // 3
    nh = d // HEAD
    hps = SAMPLE_HEADS_PER_STEP
    assert nh % hps == 0 and n_i % n_tail == 0
    p_len = cache_k.shape[1]
    period = p_len // n_i
    nkeys = bias_table.shape[-1]
    ngrp = nh // hps
    wide = hps * HEAD
    qspec = lambda g: pl.BlockSpec((ts, wide), lambda n, h: (n, g * ngrp + h))
    newspec = pl.BlockSpec((ts, wide), lambda n, h: (n, h))
    mainspec = pl.BlockSpec((None, n_i, ts, hps, HEAD), lambda n, h: (n, 0, 0, h, 0))
    tailspec = pl.BlockSpec((None, n_tail, period - ts, hps, HEAD),
                            lambda n, h: (n, n_i // n_tail - 1, 1, h, 0))
    assert period - ts == ts
    cache_bytes = _nbytes((n_i + n_tail, ts, hps, HEAD), F32)
    blocks = (2 * (2 * cache_bytes + _nbytes((hps, 3 * ts, nkeys), F32))
              + 2 * _nbytes((nkeys, HEAD), F32) + 2 * _nbytes((nkeys, HEAD), BF16)
              + 8 * _nbytes((3 * ts, nkeys), F32))
    k5 = cache_k.reshape(nseq, n_i, period, nh, HEAD)
    v5 = cache_v.reshape(nseq, n_i, period, nh, HEAD)
    return pl.pallas_call(
        functools.partial(_attn_sample_body, ts=ts),
        grid=(nseq, ngrp),
        in_specs=[qspec(0), qspec(1), qspec(2), mainspec, tailspec, mainspec, tailspec,
                  newspec, newspec,
                  pl.BlockSpec((hps, 3 * ts, nkeys), lambda n, h: (h, 0, 0))],
        out_specs=pl.BlockSpec((ts, wide), lambda n, h: (n, h)),
        out_shape=jax.ShapeDtypeStruct((m, d), F32),
        scratch_shapes=[pltpu.VMEM((nkeys, HEAD), F32), pltpu.VMEM((nkeys, HEAD), F32)],
        compiler_params=pltpu.CompilerParams(
            dimension_semantics=("arbitrary", "arbitrary"),
            vmem_limit_bytes=_vmem_limit(blocks)),
        name="dilated_attention_sample",
    )(qproj, qproj, qproj, k5, k5, v5, v5, k_new, v_new, bias_table)


def _lower_bounds(lb_raw):
    p = jax.nn.softmax(lb_raw.astype(F32), axis=0)
    c = jnp.cumsum(p, axis=0)
    return c - c[0:1]


def kernel(x_prompt, x_sample, state_hgrn, state_ffn_conv, cache_k_win, cache_v_win, a_w_in, a_lb_raw, a_out_gain, a_w_out, kv_norm, w_kv, b_w_q, b_w_out, rel_bias, ffn_w_up, ffn_conv_w, ffn_conv_b, ffn_w_down, norm_mix, norm_ffn, norm_final):
    nb_p, t_p, d = x_prompt.shape
    nb_s, t_s, _ = x_sample.shape
    depth = ffn_w_up.shape[0]
    n_a = a_w_in.shape[0]
    nh = d // HEAD
    f = ffn_w_down.shape[1]
    p_len = cache_k_win.shape[1]
    assert t_p % (SUB_WINDOW * max(DILATIONS)) == 0 and t_s == V7X_SUBLANES

    xp = x_prompt.reshape(nb_p * t_p, d)
    xs = x_sample.reshape(nb_s * t_s, d)
    lbs = _lower_bounds(a_lb_raw)
    key_rows, n_i, n_tail = _sample_key_rows(p_len, t_s)
    bias_p = _prompt_bias_tiles(rel_bias, nh)
    bias_s = _sample_bias_table(rel_bias, nh, p_len, t_s, key_rows)

    hgrn_p, hgrn_s, conv_p, conv_s = [], [], [], []
    kp = vp = ks = vs = None
    w_kv3 = w_kv.reshape(1, d, 2 * d)
    conv_b3 = ffn_conv_b.reshape(depth, 1, f)
    for l in range(depth):
        gains = [norm_mix[l], kv_norm] if l == n_a else [norm_mix[l]]
        hp, *kvp = _rmsnorm(xp, gains, BF16)
        hs, *kvs = _rmsnorm(xs, gains, BF16)
        if l < n_a:
            pp, ps = _linear(hp, hs, a_w_in, l, name="hgrn2_in_proj")
            op, sp = _gla(pp, lbs[l], a_out_gain[l], None, nseq_total=nb_p, tlen=t_p, out_dtype=BF16)
            os_, ss = _gla(ps, lbs[l], a_out_gain[l], state_hgrn[l], nseq_total=nb_s, tlen=t_s,
                           out_dtype=F32)
            hgrn_p.append(sp)
            hgrn_s.append(ss)
            w_o, l_o = a_w_out, l
        else:
            if l == n_a:
                kp, ks = _linear(kvp[0], kvs[0], w_kv3, 0, col0=0, n=d, name="k_proj")
                vp, vs = _linear(kvp[0], kvs[0], w_kv3, 0, col0=d, n=d, name="v_proj")
            jb = l - n_a
            qp, qs = _linear(hp, hs, b_w_q, jb, name="q_proj")
            op = _attn_prompt(qp, kp, vp, bias_p, nseq=nb_p, tlen=t_p)
            os_ = _attn_sample(qs, ks, vs, cache_k_win, cache_v_win, bias_s, n_i, n_tail,
                               nseq=nb_s, ts=t_s)
            w_o, l_o = b_w_out, jb
        xp, xs = _linear(op, os_, w_o, l_o, res=(xp, xs), name="mixer_out_proj")

        hp, = _rmsnorm(xp, [norm_ffn[l]], BF16)
        hs, = _rmsnorm(xs, [norm_ffn[l]], BF16)
        prev = state_ffn_conv[l]
        ext_s = jnp.pad(prev, ((0, 0), (0, t_s - (CONV_W - 1)), (0, 0))).reshape(nb_s * t_s, f)
        actp, acts, tailp, a_s = _ffn_up(hp, hs, ffn_w_up, ffn_conv_w, conv_b3, ext_s, l,
                                         tlen=t_p, ts=t_s)
        conv_p.append(tailp.reshape(nb_p, 8, f)[:, 8 - (CONV_W - 1):, :])
        conv_s.append(a_s.reshape(nb_s, t_s, f)[:, t_s - (CONV_W - 1):, :])
        xp, xs = _linear(actp, acts, ffn_w_down, l, res=(xp, xs), name="ffn_down_proj")

    yp = _rmsnorm(xp, [norm_final], F32)[0].reshape(nb_p, t_p, d)
    ys = _rmsnorm(xs, [norm_final], F32)[0].reshape(nb_s, t_s, d)
    keep = min(BUCKET_MAX_DIST, t_p)
    k_win = kp.reshape(nb_p, t_p, nh, HEAD)[:, t_p - keep:]
    v_win = vp.reshape(nb_p, t_p, nh, HEAD)[:, t_p - keep:]
    return (yp, ys, jnp.stack(hgrn_p), jnp.stack(hgrn_s), jnp.stack(conv_p), jnp.stack(conv_s),
            k_win, v_win, ks.reshape(nb_s, t_s, nh, HEAD), vs.reshape(nb_s, t_s, nh, HEAD))
```

```python
import functools
import math

import numpy as np
import jax
import jax.numpy as jnp
from jax import lax
from jax.experimental import pallas as pl
from jax.experimental.pallas import tpu as pltpu

F32 = jnp.float32
BF16 = jnp.bfloat16

V7X_VMEM_BYTES = 64 * 1024 * 1024
V7X_LANES = 128
V7X_SUBLANES = 8
COMPILER_SCRATCH_BYTES = 12 * 1024 * 1024

HEAD = 128
GLA_CHUNK = 128
GLA_SEQ_UNROLL = 8
GLA_HEADS_PER_STEP = 2
GLA_CHUNK_UNROLL = 8
SUB_WINDOW = 128
DILATIONS = (1, 4, 16)
N_BUCKETS = 32
BUCKET_MAX_DIST = 2048
CONV_W = 3
EPS = 1e-6
NEG = -1e30
LOG2_E = 1.4426950408889634


def _vmem_limit(block_bytes):
    return int(min(block_bytes + COMPILER_SCRATCH_BYTES, V7X_VMEM_BYTES - (2 << 20)))


def _nbytes(shape, dtype):
    return int(np.prod(shape)) * jnp.dtype(dtype).itemsize


def _rmsnorm_body(x_ref, *refs):
    g_refs, o_refs = refs[:len(refs) // 2], refs[len(refs) // 2:]
    x = x_ref[...]
    ms = jnp.mean(x * x, axis=-1, keepdims=True)
    y = x * lax.rsqrt(ms + EPS)
    for g_ref, o_ref in zip(g_refs, o_refs):
        o_ref[...] = (y * g_ref[...]).astype(o_ref.dtype)


def _rmsnorm(x, gains, out_dtype):
    m, d = x.shape
    tm = min(m, 256)
    assert m % tm == 0
    n = len(gains)
    blocks = 2 * (_nbytes((tm, d), F32) + n * _nbytes((tm, d), out_dtype))
    row_spec = pl.BlockSpec((tm, d), lambda i: (i, 0))
    return pl.pallas_call(
        _rmsnorm_body,
        grid=(m // tm,),
        in_specs=[row_spec] + [pl.BlockSpec((1, d), lambda i: (0, 0))] * n,
        out_specs=[row_spec] * n,
        out_shape=[jax.ShapeDtypeStruct((m, d), out_dtype)] * n,
        compiler_params=pltpu.CompilerParams(vmem_limit_bytes=_vmem_limit(blocks)),
        name="rmsnorm",
    )(x, *[g.reshape(1, d) for g in gains])


def _linear_body(*refs, layer, col0, kc, nchunks, spc, has_res):
    if has_res:
        xp_ref, xs_ref, w_hbm, rp_ref, rs_ref, yp_ref, ys_ref, wb_ref, stg_ref, sem = refs
    else:
        xp_ref, xs_ref, w_hbm, yp_ref, ys_ref, wb_ref, stg_ref, sem = refs
    j = pl.program_id(0)
    i = pl.program_id(1)
    nj = pl.num_programs(0)
    tn = wb_ref.shape[2]
    cur = j % 2

    def chunk_copy(jt, c):
        rows = pl.ds(pl.multiple_of(c * kc, kc), kc)
        cols = pl.ds(pl.multiple_of(col0 + jt * tn, tn), tn)
        return pltpu.make_async_copy(w_hbm.at[layer, rows, cols], stg_ref.at[c % 2], sem.at[c % 2])

    def land(jt, c, wslot):
        chunk_copy(jt, c).wait()
        rows = pl.ds(pl.multiple_of(c * kc, kc), kc)
        wb_ref[wslot, rows, :] = stg_ref[c % 2].astype(BF16)

    @pl.when(jnp.logical_and(j == 0, i == 0))
    def _first_tile():
        chunk_copy(0, 0).start()

        def step(c, carry):
            @pl.when(c + 1 < nchunks)
            def _():
                chunk_copy(0, c + 1).start()
            land(0, c, 0)
            return carry

        lax.fori_loop(0, nchunks, step, 0)

    @pl.when(jnp.logical_and(j > 0, i == 0))
    def _finish_prefetched_tile():
        land(j, nchunks - 1, cur)

    @pl.when(jnp.logical_and(j + 1 < nj, i % spc == 0))
    def _prefetch_next_tile():
        c = i // spc
        chunk_copy(j + 1, c).start()

        @pl.when(c > 0)
        def _():
            land(j + 1, c - 1, 1 - cur)

    @pl.when(i == 0)
    def _sample_rows():
        ys = jnp.dot(xs_ref[...].astype(BF16), wb_ref[cur], preferred_element_type=F32)
        if has_res:
            ys = ys + rs_ref[...]
        ys_ref[...] = ys

    yp = jnp.dot(xp_ref[...], wb_ref[cur], preferred_element_type=F32)
    if has_res:
        yp = yp + rp_ref[...]
    yp_ref[...] = yp


BF16_SUBLANES = 16
MAX_W_CHUNKS = 16


def _linear_tiles(mp, k, n, col0):
    tm, tn_max = (512, 1024) if k <= 4096 else (256, 512)
    tm = min(tm, mp)
    tn = next(t for t in (tn_max, tn_max // 2, tn_max // 4, V7X_LANES)
              if n % t == 0 and col0 % t == 0)
    ni = mp // tm
    nchunks = next(c for c in range(min(MAX_W_CHUNKS, ni), 0, -1)
                   if ni % c == 0 and k % (c * BF16_SUBLANES) == 0)
    return tm, tn, k // nchunks, nchunks, ni // nchunks


def _linear(xp, xs, w, layer, *, col0=0, n=None, res=None, name):
    mp, k = xp.shape
    ms = xs.shape[0]
    n = w.shape[2] if n is None else n
    tm, tn, kc, nchunks, spc = _linear_tiles(mp, k, n, col0)
    assert mp % tm == 0 and n % tn == 0
    in_specs = [pl.BlockSpec((tm, k), lambda j, i: (i, 0)),
                pl.BlockSpec((ms, k), lambda j, i: (0, 0)),
                pl.BlockSpec(memory_space=pl.ANY)]
    args = [xp, xs, w]
    blocks = (2 * (_nbytes((tm, k), BF16) + _nbytes((ms, k), xs.dtype)
                   + _nbytes((tm, tn), F32) + _nbytes((ms, tn), F32))
              + 2 * _nbytes((k, tn), BF16) + 2 * _nbytes((kc, tn), F32))
    if res is not None:
        in_specs += [pl.BlockSpec((tm, tn), lambda j, i: (i, j)),
                     pl.BlockSpec((ms, tn), lambda j, i: (0, j))]
        args += list(res)
        blocks += 2 * (_nbytes((tm, tn), F32) + _nbytes((ms, tn), F32))
    return pl.pallas_call(
        functools.partial(_linear_body, layer=layer, col0=col0, kc=kc, nchunks=nchunks, spc=spc,
                          has_res=res is not None),
        grid=(n // tn, mp // tm),
        in_specs=in_specs,
        out_specs=[pl.BlockSpec((tm, tn), lambda j, i: (i, j)),
                   pl.BlockSpec((ms, tn), lambda j, i: (0, j))],
        out_shape=[jax.ShapeDtypeStruct((mp, n), F32), jax.ShapeDtypeStruct((ms, n), F32)],
        scratch_shapes=[pltpu.VMEM((2, k, tn), BF16), pltpu.VMEM((2, kc, tn), F32),
                        pltpu.SemaphoreType.DMA((2,))],
        compiler_params=pltpu.CompilerParams(
            dimension_semantics=("arbitrary", "arbitrary"),
            vmem_limit_bytes=_vmem_limit(blocks)),
        name=name,
    )(*args)


FFN_STEP_ROWS = 2048
FFN_SUB_ROWS = 256


def _ffn_up_body(hp_ref, hs_ref, w_hbm, cw_ref, cb_ref, exts_ref,
                 actp_ref, acts_ref, tailp_ref, as_ref,
                 wcat_ref, stg_ref, sem, ext_ref, *, layer, f, kc, nchunks, tm, tlen, ts):
    j = pl.program_id(0)
    i = pl.program_id(1)
    nj = pl.num_programs(0)
    tf = cw_ref.shape[1]
    cur = j % 2
    cw = cw_ref[...]
    cb = cb_ref[...]

    def act_fn(a2, a1, a0, b):
        conv = cb + cw[0:1, :] * a2 + cw[1:2, :] * a1 + cw[2:3, :] * a0
        return jax.nn.silu(conv) * b

    def chunk_copy(jt, c, half):
        rows = pl.ds(pl.multiple_of(c * kc, kc), kc)
        cols = pl.ds(pl.multiple_of(half * f + jt * tf, tf), tf)
        return pltpu.make_async_copy(w_hbm.at[layer, rows, cols], stg_ref.at[c % 2, half],
                                     sem.at[c % 2, half])

    def start(jt, c):
        for half in (0, 1):
            chunk_copy(jt, c, half).start()

    def land(jt, c, wslot):
        rows = pl.ds(pl.multiple_of(c * kc, kc), kc)
        for half in (0, 1):
            chunk_copy(jt, c, half).wait()
            wcat_ref[wslot, rows, half * tf:(half + 1) * tf] = stg_ref[c % 2, half].astype(BF16)

    @pl.when(jnp.logical_and(j == 0, i == 0))
    def _first_tile():
        start(0, 0)

        def step(c, carry):
            @pl.when(c + 1 < nchunks)
            def _():
                start(0, c + 1)
            land(0, c, 0)
            return carry

        lax.fori_loop(0, nchunks, step, 0)

    @pl.when(jnp.logical_and(j > 0, i == 0))
    def _finish_prefetched_tile():
        land(j, nchunks - 1, cur)

    @pl.when(j + 1 < nj)
    def _prefetch_next_tile():
        start(j + 1, i)

        @pl.when(i > 0)
        def _():
            land(j + 1, i - 1, 1 - cur)

    @pl.when(i == 0)
    def _sample_rows():
        ab = jnp.dot(hs_ref[...].astype(BF16), wcat_ref[cur], preferred_element_type=F32)
        a, b = ab[:, 0:tf], ab[:, tf:2 * tf]
        ms = a.shape[0]
        t = lax.broadcasted_iota(jnp.int32, a.shape, 0) % ts
        ext = exts_ref[...]
        a1 = jnp.where(t == 0, pltpu.roll(ext, ms - 1, axis=0), pltpu.roll(a, 1, axis=0))
        a2 = jnp.where(t < 2, ext, pltpu.roll(a, 2, axis=0))
        acts_ref[...] = act_fn(a2, a1, a, b).astype(acts_ref.dtype)
        as_ref[...] = a

    @pl.when((i * tm) % tlen == 0)
    def _sequence_start():
        ext_ref[...] = jnp.zeros(ext_ref.shape, F32)

    tail = ext_ref[...]
    for s in range(tm // FFN_SUB_ROWS):
        rows = slice(s * FFN_SUB_ROWS, (s + 1) * FFN_SUB_ROWS)
        hp = hp_ref[rows, :]
        ab = jnp.dot(hp, wcat_ref[cur], preferred_element_type=F32)
        a, b = ab[:, 0:tf], ab[:, tf:2 * tf]
        ext = jnp.concatenate([tail, a], axis=0)
        a1 = ext[7:7 + FFN_SUB_ROWS, :]
        a2 = ext[6:6 + FFN_SUB_ROWS, :]
        actp_ref[rows, :] = act_fn(a2, a1, a, b).astype(actp_ref.dtype)
        tail = a[FFN_SUB_ROWS - 8:FFN_SUB_ROWS, :]
    ext_ref[...] = tail

    @pl.when(((i + 1) * tm) % tlen == 0)
    def _sequence_end():
        n = (i * tm) // tlen
        tailp_ref[pl.ds(pl.multiple_of(n * 8, 8), 8), :] = tail


def _ffn_up(hp, hs, w_up, conv_w, conv_b, ext_s, layer, *, tlen, ts):
    mp, d = hp.shape
    ms = hs.shape[0]
    f = w_up.shape[2] // 2
    nseq = mp // tlen
    tm = min(FFN_STEP_ROWS, tlen)
    tf = 256
    assert mp % tm == 0 and tlen % tm == 0 and f % tf == 0 and tm % FFN_SUB_ROWS == 0
    nf = f // tf
    nchunks = mp // tm
    assert d % (nchunks * BF16_SUBLANES) == 0
    kc = d // nchunks
    blocks = (2 * (_nbytes((tm, d), BF16) + _nbytes((ms, d), hs.dtype)
                   + _nbytes((tm, tf), BF16) + 3 * _nbytes((ms, tf), F32) + _nbytes((nseq * 8, tf), F32))
              + 2 * _nbytes((d, 2 * tf), BF16) + 4 * _nbytes((kc, tf), F32)
              + _nbytes((8, tf), F32) + 5 * _nbytes((tm, tf), F32))
    return pl.pallas_call(
        functools.partial(_ffn_up_body, layer=layer, f=f, kc=kc, nchunks=nchunks, tm=tm, tlen=tlen,
                          ts=ts),
        grid=(nf, mp // tm),
        in_specs=[pl.BlockSpec((tm, d), lambda j, i: (i, 0)),
                  pl.BlockSpec((ms, d), lambda j, i: (0, 0)),
                  pl.BlockSpec(memory_space=pl.ANY),
                  pl.BlockSpec((None, CONV_W, tf), lambda j, i: (layer, 0, j)),
                  pl.BlockSpec((None, 1, tf), lambda j, i: (layer, 0, j)),
                  pl.BlockSpec((ms, tf), lambda j, i: (0, j))],
        out_specs=[pl.BlockSpec((tm, tf), lambda j, i: (i, j)),
                   pl.BlockSpec((ms, tf), lambda j, i: (0, j)),
                   pl.BlockSpec((nseq * 8, tf), lambda j, i: (0, j)),
                   pl.BlockSpec((ms, tf), lambda j, i: (0, j))],
        out_shape=[jax.ShapeDtypeStruct((mp, f), BF16),
                   jax.ShapeDtypeStruct((ms, f), BF16),
                   jax.ShapeDtypeStruct((nseq * 8, f), F32),
                   jax.ShapeDtypeStruct((ms, f), F32)],
        scratch_shapes=[pltpu.VMEM((2, d, 2 * tf), BF16), pltpu.VMEM((2, 2, kc, tf), F32),
                        pltpu.SemaphoreType.DMA((2, 2)), pltpu.VMEM((8, tf), F32)],
        compiler_params=pltpu.CompilerParams(
            dimension_semantics=("arbitrary", "arbitrary"),
            vmem_limit_bytes=_vmem_limit(blocks)),
        name="ffn_up",
    )(hp, hs, w_up, conv_w, conv_b, ext_s)


def _level_table(c):
    t = np.arange(c)[:, None]
    s = np.arange(c)[None, :]
    x = np.maximum(t ^ s, 1)
    lev = np.floor(np.log2(x)).astype(np.int32)
    return np.where(t > s, lev, np.where(t == s, -1, -2)).astype(np.int32)


def _cumsum_table(c):
    return np.tril(np.ones((c, c), np.float32))


def _split3(x):
    x0 = x.astype(BF16)
    r1 = x - x0.astype(F32)
    x1 = r1.astype(BF16)
    x2 = (r1 - x1.astype(F32)).astype(BF16)
    return x0, x1, x2


def _block_boundary_rows(b, blk):
    c, w = b.shape
    half = blk // 2
    if blk >= 2 * V7X_SUBLANES:
        x = b.reshape(c // blk, blk, w)
        r = jnp.broadcast_to(x[:, half - 1:half, :], x.shape)
        return r.reshape(c, w)
    x = b.reshape(c // V7X_SUBLANES, V7X_SUBLANES, w)
    sub = lax.broadcasted_iota(jnp.int32, x.shape, 1)

    def row(i):
        return jnp.broadcast_to(x[:, i:i + 1, :], x.shape)

    if blk == 8:
        r = row(3)
    elif blk == 4:
        r = jnp.where(sub < 4, row(1), row(5))
    else:
        r = jnp.where(sub < 2, row(0), jnp.where(sub < 4, row(2), jnp.where(sub < 6, row(4), row(6))))
    return r.reshape(c, w)


_NT = (((1,), (1,)), ((), ()))
_TN = (((0,), (0,)), ((), ()))


def _gla_body(*refs, nseq, tlen, valid, has_s0, heads):
    if has_s0:
        (q_ref, f_ref, v_ref, g_ref, lb_ref, gain_ref, lev_ref, tri_ref, s0_ref,
         o_ref, so_ref) = refs
    else:
        q_ref, f_ref, v_ref, g_ref, lb_ref, gain_ref, lev_ref, tri_ref, o_ref, so_ref = refs
    for hh in range(heads):
        _gla_head(q_ref, f_ref, v_ref, g_ref, lb_ref, gain_ref, lev_ref, tri_ref,
                  s0_ref if has_s0 else None, o_ref, so_ref, hh, nseq=nseq, tlen=tlen, valid=valid)


def _gla_head(q_ref, f_ref, v_ref, g_ref, lb_ref, gain_ref, lev_ref, tri_ref, s0_ref, o_ref, so_ref,
              hh, *, nseq, tlen, valid):
    has_s0 = s0_ref is not None
    cols = slice(hh * HEAD, (hh + 1) * HEAD)
    c = GLA_CHUNK
    rows_in = min(c, valid)
    nchunk = max(tlen // c, 1)
    lb = lb_ref[:, cols]
    log_lb = jnp.log(lb)
    log1m_lb = jnp.log1p(-lb)
    one_m_lb = 1.0 - lb
    gain = gain_ref[:, cols]
    n_levels = int(math.log2(rows_in))
    assert 1 << n_levels == rows_in

    def pad_rows(x):
        if rows_in == c:
            return x
        return jnp.concatenate([x, jnp.zeros((c - rows_in, x.shape[1]), x.dtype)], axis=0)

    def chunk(n, ci, st):
        row0 = pl.multiple_of(n * tlen + ci * rows_in, rows_in)
        rows = pl.ds(row0, rows_in)
        qr, fz, v = q_ref[rows, cols], f_ref[rows, cols], v_ref[rows, cols]
        e = jnp.exp(-jnp.abs(fz))
        one_p_e = 1.0 + e
        log_sig = jnp.minimum(fz, 0.0) - jnp.log(one_p_e)
        u = log1m_lb + log_sig
        log_f = jnp.maximum(log_lb, u) + jnp.log(1.0 + jnp.exp(-jnp.abs(log_lb - u)))
        k = one_m_lb * (jnp.where(fz > 0.0, e, 1.0) / one_p_e)
        q = jax.nn.silu(qr) * (HEAD ** -0.5)
        vb = pad_rows(v.astype(BF16))

        tri = tri_ref[...]
        b = sum(jnp.dot(tri, pad_rows(part), preferred_element_type=F32) for part in _split3(log_f))
        b = b * LOG2_E
        b_last = b[c - 1:c, :]
        b = b[0:rows_in, :]

        lev = lev_ref[...]
        diag = pad_rows(jnp.sum(q * k, axis=-1, keepdims=True))
        att = jnp.where(lev == -1, diag, 0.0)
        for l in range(n_levels):
            r = _block_boundary_rows(b, 2 << l)
            decay = jnp.exp2(-jnp.abs(b - r))
            qt = pad_rows((q * decay).astype(BF16))
            kt = pad_rows((k * decay).astype(BF16))
            s_l = lax.dot_general(qt, kt, _NT, preferred_element_type=F32)
            att = jnp.where(lev == l, s_l, att)
        o = jnp.dot(att.astype(BF16), vb, preferred_element_type=F32)
        o = o + lax.dot_general(pad_rows((q * jnp.exp2(b)).astype(BF16)), st.astype(BF16), _NT,
                                preferred_element_type=F32)
        kd = pad_rows((k * jnp.exp2(b_last - b)).astype(BF16))
        st_new = st * jnp.exp2(b_last) + lax.dot_general(vb, kd, _TN, preferred_element_type=F32)

        o = o[0:rows_in, :]
        o = o * lax.rsqrt(jnp.mean(o * o, axis=-1, keepdims=True) + EPS) * gain
        o = o * jax.nn.silu(g_ref[rows, cols])
        o_ref[rows, cols] = o.astype(o_ref.dtype)
        return st_new

    def sequence(n, carry):
        if has_s0:
            st0 = s0_ref[n, hh].T
        else:
            st0 = jnp.zeros((HEAD, HEAD), F32)
        st = lax.fori_loop(0, nchunk, lambda ci, st: chunk(n, ci, st), st0,
                           unroll=min(GLA_CHUNK_UNROLL, nchunk))
        so_ref[n, hh] = st.T
        return carry

    lax.fori_loop(0, nseq, sequence, 0, unroll=min(nseq, GLA_SEQ_UNROLL))


def _gla(proj, lb, gain, s0, *, nseq_total, tlen, out_dtype):
    m, d4 = proj.shape
    d = d4 // 4
    nh = d // HEAD
    nseq = 1 if tlen >= GLA_CHUNK else nseq_total
    assert tlen % GLA_CHUNK == 0 or tlen < GLA_CHUNK
    rows = nseq * tlen
    hps = GLA_HEADS_PER_STEP if nh % GLA_HEADS_PER_STEP == 0 else 1
    wide = hps * HEAD
    ngrp = nh // hps
    col = lambda off: pl.BlockSpec((rows, wide), lambda n, h: (n, h + off))
    vec = pl.BlockSpec((1, wide), lambda n, h: (0, h))
    const = pl.BlockSpec((GLA_CHUNK, GLA_CHUNK), lambda n, h: (0, 0))
    in_specs = [col(0), col(ngrp), col(2 * ngrp), col(3 * ngrp), vec, vec, const, const]
    args = [proj, proj, proj, proj, lb.reshape(1, d), gain.reshape(1, d),
            jnp.asarray(_level_table(GLA_CHUNK)), jnp.asarray(_cumsum_table(GLA_CHUNK), BF16)]
    state_spec = pl.BlockSpec((nseq, hps, HEAD, HEAD), lambda n, h: (n, h, 0, 0))
    if s0 is not None:
        in_specs.append(state_spec)
        args.append(s0)
    blocks = 2 * (4 * _nbytes((rows, wide), F32) + _nbytes((rows, wide), out_dtype)
                  + 2 * _nbytes((nseq, hps, HEAD, HEAD), F32)) + (4 << 20)
    return pl.pallas_call(
        functools.partial(_gla_body, nseq=nseq, tlen=tlen, valid=tlen, has_s0=s0 is not None,
                          heads=hps),
        grid=(nseq_total // nseq, ngrp),
        in_specs=in_specs,
        out_specs=[pl.BlockSpec((rows, wide), lambda n, h: (n, h)), state_spec],
        out_shape=[jax.ShapeDtypeStruct((m, d), out_dtype),
                   jax.ShapeDtypeStruct((nseq_total, nh, HEAD, HEAD), F32)],
        compiler_params=pltpu.CompilerParams(
            dimension_semantics=("arbitrary", "arbitrary"),
            vmem_limit_bytes=_vmem_limit(blocks)),
        name="hgrn2_recurrence",
    )(*args)


def _t5_buckets(dist):
    max_exact = N_BUCKETS // 2
    n = np.asarray(dist, dtype=np.int64)
    large = max_exact + (np.log(np.maximum(n, 1) / max_exact)
                         / np.log(BUCKET_MAX_DIST / max_exact)
                         * (N_BUCKETS - max_exact)).astype(np.int64)
    large = np.minimum(large, N_BUCKETS - 1)
    return np.where(n < max_exact, n, large).astype(np.int32)


def _bucket_lookup(tab, buckets):
    flat = jnp.asarray(buckets.reshape(1, -1))
    onehot = (flat == lax.broadcasted_iota(jnp.int32, (N_BUCKETS, flat.shape[1]), 0)).astype(F32)
    out = jnp.einsum("bh,bx->hx", tab, onehot, precision=lax.Precision.HIGHEST)
    return out.reshape((tab.shape[1],) + buckets.shape)


def _prompt_bias_tiles(rel_bias, nh):
    bl = SUB_WINDOW
    off = np.arange(bl)[:, None] + bl - np.arange(2 * bl)[None, :]
    band = (off >= 0) & (off <= SUB_WINDOW)
    tiles = []
    for g, dil in enumerate(DILATIONS):
        buckets = _t5_buckets(np.clip(off, 0, SUB_WINDOW) * dil)
        tab = rel_bias[:, g * nh:(g + 1) * nh].astype(F32)
        tiles.append(jnp.where(band[None], _bucket_lookup(tab, buckets), NEG))
    return jnp.stack(tiles)


ATTN_BLOCK_UNROLL = 8


def _attn_prompt_body(q1_ref, q2_ref, q3_ref, k_ref, v_ref, bias_ref, o_ref,
                      acc_ref, lse_ref, k4_ref, v4_ref, q34_ref, *, tlen):
    bl = SUB_WINDOW
    scale = HEAD ** -0.5
    mid = DILATIONS[1]
    seg = tlen // mid

    for src_ref, dst_ref in ((k_ref, k4_ref), (v_ref, v4_ref), (q3_ref, q34_ref)):
        for r in range(mid):
            dst_ref[r * seg:(r + 1) * seg, :] = src_ref[pl.ds(r, seg, stride=mid), :]

    for g, (dil, q_ref) in enumerate(zip(DILATIONS, (q1_ref, q2_ref, q3_ref))):
        nblk = tlen // (dil * bl)
        with_prev = nblk > 1

        def block(it, carry, dil=dil, q_ref=q_ref, nblk=nblk, g=g, with_prev=with_prev):
            r = it // nblk
            bi = it % nblk
            start = r + dil * bl * bi
            qrows = pl.ds(start, bl, stride=dil) if dil > 1 else pl.ds(start, bl)
            if with_prev:
                nkey = 2 * bl
                bias = bias_ref[g, 0]
                bias_first = jnp.concatenate([bias[:, bl:2 * bl], jnp.full((bl, bl), NEG, F32)], axis=1)
                bias = jnp.where(bi == 0, bias_first, bias)
                back = jnp.where(bi > 0, bl, 0)
            else:
                nkey = bl
                bias = bias_ref[g, 0, :, bl:2 * bl]
                back = 0
            if dil == 1:
                krows = pl.ds(start - back, nkey)
                qb, k2, v2 = q_ref[qrows, :], k_ref[krows, :], v_ref[krows, :]
            elif dil == mid:
                krows = pl.ds(r * seg + bl * bi - back, nkey)
                qb, k2, v2 = q_ref[qrows, :], k4_ref[krows, :], v4_ref[krows, :]
            else:
                sub = dil // mid
                rows4 = pl.ds((r % mid) * seg + r // mid + sub * (bl * bi - back), nkey, stride=sub)
                q4rows = pl.ds((r % mid) * seg + r // mid + sub * bl * bi, bl, stride=sub)
                qb, k2, v2 = q34_ref[q4rows, :], k4_ref[rows4, :], v4_ref[rows4, :]
            qb, k2, v2 = qb.astype(BF16), k2.astype(BF16), v2.astype(BF16)
            s = lax.dot_general(qb, k2, _NT, preferred_element_type=F32) * scale + bias
            mx = jnp.max(s, axis=-1, keepdims=True)
            p = jnp.exp(s - mx)
            sm = jnp.sum(p, axis=-1, keepdims=True)
            acc = jnp.dot(p.astype(BF16), v2, preferred_element_type=F32)
            acc_ref[g, qrows, :] = acc / sm
            lse_ref[g, qrows, :] = jnp.broadcast_to(mx + jnp.log(sm), (bl, HEAD))
            return carry

        def residue_pair(it, carry, dil=dil, g=g):
            sub = dil // mid
            parts = []
            for r in (2 * it, 2 * it + 1):
                rows4 = pl.ds((r % mid) * seg + r // mid, bl, stride=sub)
                parts.append((q34_ref[rows4, :], k4_ref[rows4, :], v4_ref[rows4, :],
                              pl.ds(r, bl, stride=dil)))
            qb = jnp.concatenate([pt[0] for pt in parts], axis=0).astype(BF16)
            k2 = jnp.concatenate([pt[1] for pt in parts], axis=0).astype(BF16)
            v2 = jnp.concatenate([pt[2] for pt in parts], axis=0).astype(BF16)
            own = bias_ref[g, 0, :, bl:2 * bl]
            neg = jnp.full((bl, bl), NEG, F32)
            bias = jnp.concatenate([jnp.concatenate([own, neg], axis=1),
                                    jnp.concatenate([neg, own], axis=1)], axis=0)
            s = lax.dot_general(qb, k2, _NT, preferred_element_type=F32) * scale + bias
            mx = jnp.max(s, axis=-1, keepdims=True)
            p = jnp.exp(s - mx)
            sm = jnp.sum(p, axis=-1, keepdims=True)
            o = jnp.dot(p.astype(BF16), v2, preferred_element_type=F32) / sm
            lse = jnp.broadcast_to(mx + jnp.log(sm), (2 * bl, HEAD))
            for half, pt in enumerate(parts):
                acc_ref[g, pt[3], :] = o[half * bl:(half + 1) * bl, :]
                lse_ref[g, pt[3], :] = lse[half * bl:(half + 1) * bl, :]
            return carry

        if with_prev:
            lax.fori_loop(0, dil * nblk, block, 0, unroll=ATTN_BLOCK_UNROLL)
        else:
            lax.fori_loop(0, dil // 2, residue_pair, 0, unroll=ATTN_BLOCK_UNROLL // 2)

    def merge(ci, carry):
        rows = pl.ds(pl.multiple_of(ci * bl, bl), bl)
        l1, l2, l3 = lse_ref[0, rows, :], lse_ref[1, rows, :], lse_ref[2, rows, :]
        mm = jnp.maximum(jnp.maximum(l1, l2), l3)
        w1, w2, w3 = jnp.exp(l1 - mm), jnp.exp(l2 - mm), jnp.exp(l3 - mm)
        num = w1 * acc_ref[0, rows, :] + w2 * acc_ref[1, rows, :] + w3 * acc_ref[2, rows, :]
        o_ref[rows, :] = (num / (w1 + w2 + w3)).astype(o_ref.dtype)
        return carry

    lax.fori_loop(0, tlen // bl, merge, 0, unroll=2)


def _attn_prompt(qproj, k, v, bias_tiles, *, nseq, tlen):
    m, d3 = qproj.shape
    d = d3 // 3
    nh = d // HEAD
    qspec = lambda g: pl.BlockSpec((tlen, HEAD), lambda n, h: (n, g * nh + h))
    kvspec = pl.BlockSpec((tlen, HEAD), lambda n, h: (n, h))
    blocks = (2 * (5 * _nbytes((tlen, HEAD), F32) + _nbytes((3, 1, SUB_WINDOW, 2 * SUB_WINDOW), F32)
                   + _nbytes((tlen, HEAD), BF16))
              + 9 * _nbytes((tlen, HEAD), F32))
    return pl.pallas_call(
        functools.partial(_attn_prompt_body, tlen=tlen),
        grid=(nseq, nh),
        in_specs=[qspec(0), qspec(1), qspec(2), kvspec, kvspec,
                  pl.BlockSpec((3, 1, SUB_WINDOW, 2 * SUB_WINDOW), lambda n, h: (0, h, 0, 0))],
        out_specs=pl.BlockSpec((tlen, HEAD), lambda n, h: (n, h)),
        out_shape=jax.ShapeDtypeStruct((m, d), BF16),
        scratch_shapes=[pltpu.VMEM((3, tlen, HEAD), F32), pltpu.VMEM((3, tlen, HEAD), F32),
                        pltpu.VMEM((tlen, HEAD), F32), pltpu.VMEM((tlen, HEAD), F32),
                        pltpu.VMEM((tlen, HEAD), F32)],
        compiler_params=pltpu.CompilerParams(
            dimension_semantics=("arbitrary", "arbitrary"),
            vmem_limit_bytes=_vmem_limit(blocks)),
        name="dilated_attention_prompt",
    )(qproj, qproj, qproj, k, v, bias_tiles)


SAMPLE_HEADS_PER_STEP = V7X_SUBLANES


def _sample_key_rows(p_len, ts):
    period = max(DILATIONS)
    assert p_len % period == 0 and ts < period
    n_i = p_len // period
    tail_i0 = (p_len - DILATIONS[1] * SUB_WINDOW) // period
    assert tail_i0 >= 0 and DILATIONS[0] * SUB_WINDOW <= DILATIONS[1] * SUB_WINDOW
    main = (np.arange(n_i)[:, None] * period + np.arange(ts)[None, :]).reshape(-1)
    tail = (np.arange(tail_i0, n_i)[:, None] * period + np.arange(ts, period)[None, :]).reshape(-1)
    rows = np.concatenate([main, tail, p_len + np.arange(ts)])
    nkeys = -(-rows.size // V7X_LANES) * V7X_LANES
    return np.concatenate([rows, np.full(nkeys - rows.size, -1)]), n_i, n_i - tail_i0


def _sample_bias_table(rel_bias, nh, p_len, ts, key_rows):
    t = np.arange(ts)[:, None]
    row = key_rows[None, :]
    dist = p_len + t - row
    out = []
    for g, dil in enumerate(DILATIONS):
        j = dist // dil
        ok = (row >= 0) & (dist >= 0) & (dist % dil == 0) & (j <= SUB_WINDOW)
        buckets = _t5_buckets(np.clip(j, 0, SUB_WINDOW) * dil)
        tab = rel_bias[:, g * nh:(g + 1) * nh].astype(F32)
        out.append(jnp.where(ok[None], _bucket_lookup(tab, buckets), NEG))
    return jnp.concatenate(out, axis=1)


def _attn_sample_body(q1_ref, q2_ref, q3_ref, km_ref, kt_ref, vm_ref, vt_ref, kn_ref, vn_ref,
                      bias_ref, o_ref, kall_ref, vall_ref, *, ts):
    nkeys = kall_ref.shape[0]
    n_main = km_ref.shape[0] * km_ref.shape[1]
    n_tail = kt_ref.shape[0] * kt_ref.shape[1]
    n_used = n_main + n_tail + ts
    scale = HEAD ** -0.5
    kall_ref[n_used:nkeys, :] = jnp.zeros((nkeys - n_used, HEAD), F32)
    vall_ref[n_used:nkeys, :] = jnp.zeros((nkeys - n_used, HEAD), F32)
    for hh in range(SAMPLE_HEADS_PER_STEP):
        cols = slice(hh * HEAD, (hh + 1) * HEAD)
        for src_main, src_tail, src_new, dst in ((km_ref, kt_ref, kn_ref, kall_ref),
                                                 (vm_ref, vt_ref, vn_ref, vall_ref)):
            flat_main = src_main.reshape(n_main * SAMPLE_HEADS_PER_STEP, HEAD)
            flat_tail = src_tail.reshape(n_tail * SAMPLE_HEADS_PER_STEP, HEAD)
            dst[0:n_main, :] = flat_main[pl.ds(hh, n_main, stride=SAMPLE_HEADS_PER_STEP), :]
            dst[n_main:n_main + n_tail, :] = flat_tail[pl.ds(hh, n_tail, stride=SAMPLE_HEADS_PER_STEP), :]
            dst[n_main + n_tail:n_used, :] = src_new[:, cols]
        kb = kall_ref[...].astype(BF16)
        vb = vall_ref[...].astype(BF16)
        bias = bias_ref[hh]
        s = []
        for g, q_ref in enumerate((q1_ref, q2_ref, q3_ref)):
            sg = lax.dot_general(q_ref[:, cols].astype(BF16), kb, _NT, preferred_element_type=F32)
            s.append(sg * scale + bias[g * ts:(g + 1) * ts, :])
        mx = [jnp.max(x, axis=-1, keepdims=True) for x in s]
        mm = jnp.maximum(jnp.maximum(mx[0], mx[1]), mx[2])
        p = jnp.exp(s[0] - mm) + jnp.exp(s[1] - mm) + jnp.exp(s[2] - mm)
        den = jnp.sum(p, axis=-1, keepdims=True)
        o = jnp.dot(p.astype(BF16), vb, preferred_element_type=F32)
        o_ref[:, cols] = (o / den).astype(o_ref.dtype)


def _attn_sample(qproj, k_new, v_new, cache_k, cache_v, bias_table, n_i, n_tail, *, nseq, ts):
    m, d3 = qproj.shape
    d = d3 // 3
    nh = d // HEAD
    hps = SAMPLE_HEADS_PER_STEP
    assert nh % hps == 0 and n_i % n_tail == 0
    p_len = cache_k.shape[1]
    period = p_len // n_i
    nkeys = bias_table.shape[-1]
    ngrp = nh // hps
    wide = hps * HEAD
    qspec = lambda g: pl.BlockSpec((ts, wide), lambda n, h: (n, g * ngrp + h))
    newspec = pl.BlockSpec((ts, wide), lambda n, h: (n, h))
    mainspec = pl.BlockSpec((None, n_i, ts, hps, HEAD), lambda n, h: (n, 0, 0, h, 0))
    tailspec = pl.BlockSpec((None, n_tail, period - ts, hps, HEAD),
                            lambda n, h: (n, n_i // n_tail - 1, 1, h, 0))
    assert period - ts == ts
    cache_bytes = _nbytes((n_i + n_tail, ts, hps, HEAD), F32)
    blocks = (2 * (2 * cache_bytes + _nbytes((hps, 3 * ts, nkeys), F32))
              + 2 * _nbytes((nkeys, HEAD), F32) + 2 * _nbytes((nkeys, HEAD), BF16)
              + 8 * _nbytes((3 * ts, nkeys), F32))
    k5 = cache_k.reshape(nseq, n_i, period, nh, HEAD)
    v5 = cache_v.reshape(nseq, n_i, period, nh, HEAD)
    return pl.pallas_call(
        functools.partial(_attn_sample_body, ts=ts),
        grid=(nseq, ngrp),
        in_specs=[qspec(0), qspec(1), qspec(2), mainspec, tailspec, mainspec, tailspec,
                  newspec, newspec,
                  pl.BlockSpec((hps, 3 * ts, nkeys), lambda n, h: (h, 0, 0))],
        out_specs=pl.BlockSpec((ts, wide), lambda n, h: (n, h)),
        out_shape=jax.ShapeDtypeStruct((m, d), F32),
        scratch_shapes=[pltpu.VMEM((nkeys, HEAD), F32), pltpu.VMEM((nkeys, HEAD), F32)],
        compiler_params=pltpu.CompilerParams(
            dimension_semantics=("arbitrary", "arbitrary"),
            vmem_limit_bytes=_vmem_limit(blocks)),
        name="dilated_attention_sample",
    )(qproj, qproj, qproj, k5, k5, v5, v5, k_new, v_new, bias_table)


def _lower_bounds(lb_raw):
    p = jax.nn.softmax(lb_raw.astype(F32), axis=0)
    c = jnp.cumsum(p, axis=0)
    return c - c[0:1]


def kernel(x_prompt, x_sample, state_hgrn, state_ffn_conv, cache_k_win, cache_v_win, a_w_in, a_lb_raw, a_out_gain, a_w_out, kv_norm, w_kv, b_w_q, b_w_out, rel_bias, ffn_w_up, ffn_conv_w, ffn_conv_b, ffn_w_down, norm_mix, norm_ffn, norm_final):
    nb_p, t_p, d = x_prompt.shape
    nb_s, t_s, _ = x_sample.shape
    depth = ffn_w_up.shape[0]
    n_a = a_w_in.shape[0]
    nh = d // HEAD
    f = ffn_w_down.shape[1]
    p_len = cache_k_win.shape[1]
    assert t_p % (SUB_WINDOW * max(DILATIONS)) == 0 and t_s == V7X_SUBLANES

    xp = x_prompt.reshape(nb_p * t_p, d)
    xs = x_sample.reshape(nb_s * t_s, d)
    lbs = _lower_bounds(a_lb_raw)
    key_rows, n_i, n_tail = _sample_key_rows(p_len, t_s)
    bias_p = _prompt_bias_tiles(rel_bias, nh)
    bias_s = _sample_bias_table(rel_bias, nh, p_len, t_s, key_rows)

    hgrn_p, hgrn_s, conv_p, conv_s = [], [], [], []
    kp = vp = ks = vs = None
    w_kv3 = w_kv.reshape(1, d, 2 * d)
    conv_b3 = ffn_conv_b.reshape(depth, 1, f)
    for l in range(depth):
        gains = [norm_mix[l], kv_norm] if l == n_a else [norm_mix[l]]
        hp, *kvp = _rmsnorm(xp, gains, BF16)
        hs, *kvs = _rmsnorm(xs, gains, BF16)
        if l < n_a:
            pp, ps = _linear(hp, hs, a_w_in, l, name="hgrn2_in_proj")
            op, sp = _gla(pp, lbs[l], a_out_gain[l], None, nseq_total=nb_p, tlen=t_p, out_dtype=BF16)
            os_, ss = _gla(ps, lbs[l], a_out_gain[l], state_hgrn[l], nseq_total=nb_s, tlen=t_s,
                           out_dtype=F32)
            hgrn_p.append(sp)
            hgrn_s.append(ss)
            w_o, l_o = a_w_out, l
        else:
            if l == n_a:
                kp, ks = _linear(kvp[0], kvs[0], w_kv3, 0, col0=0, n=d, name="k_proj")
                vp, vs = _linear(kvp[0], kvs[0], w_kv3, 0, col0=d, n=d, name="v_proj")
            jb = l - n_a
            qp, qs = _linear(hp, hs, b_w_q, jb, name="q_proj")
            op = _attn_prompt(qp, kp, vp, bias_p, nseq=nb_p, tlen=t_p)
            os_ = _attn_sample(qs, ks, vs, cache_k_win, cache_v_win, bias_s, n_i, n_tail,
                               nseq=nb_s, ts=t_s)
            w_o, l_o = b_w_out, jb
        xp, xs = _linear(op, os_, w_o, l_o, res=(xp, xs), name="mixer_out_proj")

        hp, = _rmsnorm(xp, [norm_ffn[l]], BF16)
        hs, = _rmsnorm(xs, [norm_ffn[l]], BF16)
        prev = state_ffn_conv[l]
        ext_s = jnp.pad(prev, ((0, 0), (0, t_s - (CONV_W - 1)), (0, 0))).reshape(nb_s * t_s, f)
        actp, acts, tailp, a_s = _ffn_up(hp, hs, ffn_w_up, ffn_conv_w, conv_b3, ext_s, l,
                                         tlen=t_p, ts=t_s)
        conv_p.append(tailp.reshape(nb_p, 8, f)[:, 8 - (CONV_W - 1):, :])
        conv_s.append(a_s.reshape(nb_s, t_s, f)[:, t_s - (CONV_W - 1):, :])
        xp, xs = _linear(actp, acts, ffn_w_down, l, res=(xp, xs), name="ffn_down_proj")

    yp = _rmsnorm(xp, [norm_final], F32)[0].reshape(nb_p, t_p, d)
    ys = _rmsnorm(xs, [norm_final], F32)[0].reshape(nb_s, t_s, d)
    keep = min(BUCKET_MAX_DIST, t_p)
    k_win = kp.reshape(nb_p, t_p, nh, HEAD)[:, t_p - keep:]
    v_win = vp.reshape(nb_p, t_p, nh, HEAD)[:, t_p - keep:]
    return (yp, ys, jnp.stack(hgrn_p), jnp.stack(hgrn_s), jnp.stack(conv_p), jnp.stack(conv_s),
            k_win, v_win, ks.reshape(nb_s, t_s, nh, HEAD), vs.reshape(nb_s, t_s, nh, HEAD))
```

```python
import functools
import math

import numpy as np
import jax
import jax.numpy as jnp
from jax import lax
from jax.experimental import pallas as pl
from jax.experimental.pallas import tpu as pltpu

F32 = jnp.float32
BF16 = jnp.bfloat16

V7X_VMEM_BYTES = 64 * 1024 * 1024
V7X_LANES = 128
V7X_SUBLANES = 8
COMPILER_SCRATCH_BYTES = 12 * 1024 * 1024

HEAD = 128
GLA_CHUNK = 128
GLA_SEQ_UNROLL = 8
GLA_HEADS_PER_STEP = 2
GLA_CHUNK_UNROLL = 8
SUB_WINDOW = 128
DILATIONS = (1, 4, 16)
N_BUCKETS = 32
BUCKET_MAX_DIST = 2048
CONV_W = 3
EPS = 1e-6
NEG = -1e30
LOG2_E = 1.4426950408889634


def _vmem_limit(block_bytes):
    return int(min(block_bytes + COMPILER_SCRATCH_BYTES, V7X_VMEM_BYTES - (2 << 20)))


def _nbytes(shape, dtype):
    return int(np.prod(shape)) * jnp.dtype(dtype).itemsize


def _rmsnorm_body(x_ref, *refs):
    g_refs, o_refs = refs[:len(refs) // 2], refs[len(refs) // 2:]
    x = x_ref[...]
    ms = jnp.mean(x * x, axis=-1, keepdims=True)
    y = x * lax.rsqrt(ms + EPS)
    for g_ref, o_ref in zip(g_refs, o_refs):
        o_ref[...] = (y * g_ref[...]).astype(o_ref.dtype)


def _rmsnorm(x, gains, out_dtype):
    m, d = x.shape
    tm = min(m, 256)
    assert m % tm == 0
    n = len(gains)
    blocks = 2 * (_nbytes((tm, d), F32) + n * _nbytes((tm, d), out_dtype))
    row_spec = pl.BlockSpec((tm, d), lambda i: (i, 0))
    return pl.pallas_call(
        _rmsnorm_body,
        grid=(m // tm,),
        in_specs=[row_spec] + [pl.BlockSpec((1, d), lambda i: (0, 0))] * n,
        out_specs=[row_spec] * n,
        out_shape=[jax.ShapeDtypeStruct((m, d), out_dtype)] * n,
        compiler_params=pltpu.CompilerParams(vmem_limit_bytes=_vmem_limit(blocks)),
        name="rmsnorm",
    )(x, *[g.reshape(1, d) for g in gains])


def _linear_body(*refs, layer, col0, kc, nchunks, spc, has_res):
    if has_res:
        xp_ref, xs_ref, w_hbm, rp_ref, rs_ref, yp_ref, ys_ref, wb_ref, stg_ref, sem = refs
    else:
        xp_ref, xs_ref, w_hbm, yp_ref, ys_ref, wb_ref, stg_ref, sem = refs
    j = pl.program_id(0)
    i = pl.program_id(1)
    nj = pl.num_programs(0)
    tn = wb_ref.shape[2]
    cur = j % 2

    def chunk_copy(jt, c):
        rows = pl.ds(pl.multiple_of(c * kc, kc), kc)
        cols = pl.ds(pl.multiple_of(col0 + jt * tn, tn), tn)
        return pltpu.make_async_copy(w_hbm.at[layer, rows, cols], stg_ref.at[c % 2], sem.at[c % 2])

    def land(jt, c, wslot):
        chunk_copy(jt, c).wait()
        rows = pl.ds(pl.multiple_of(c * kc, kc), kc)
        wb_ref[wslot, rows, :] = stg_ref[c % 2].astype(BF16)

    @pl.when(jnp.logical_and(j == 0, i == 0))
    def _first_tile():
        chunk_copy(0, 0).start()

        def step(c, carry):
            @pl.when(c + 1 < nchunks)
            def _():
                chunk_copy(0, c + 1).start()
            land(0, c, 0)
            return carry

        lax.fori_loop(0, nchunks, step, 0)

    @pl.when(jnp.logical_and(j > 0, i == 0))
    def _finish_prefetched_tile():
        land(j, nchunks - 1, cur)

    @pl.when(jnp.logical_and(j + 1 < nj, i % spc == 0))
    def _prefetch_next_tile():
        c = i // spc
        chunk_copy(j + 1, c).start()

        @pl.when(c > 0)
        def _():
            land(j + 1, c - 1, 1 - cur)

    @pl.when(i == 0)
    def _sample_rows():
        ys = jnp.dot(xs_ref[...].astype(BF16), wb_ref[cur], preferred_element_type=F32)
        if has_res:
            ys = ys + rs_ref[...]
        ys_ref[...] = ys

    yp = jnp.dot(xp_ref[...], wb_ref[cur], preferred_element_type=F32)
    if has_res:
        yp = yp + rp_ref[...]
    yp_ref[...] = yp


BF16_SUBLANES = 16
MAX_W_CHUNKS = 16


def _linear_tiles(mp, k, n, col0):
    tm, tn_max = (512, 1024) if k <= 4096 else (256, 512)
    tm = min(tm, mp)
    tn = next(t for t in (tn_max, tn_max // 2, tn_max // 4, V7X_LANES)
              if n % t == 0 and col0 % t == 0)
    ni = mp // tm
    nchunks = next(c for c in range(min(MAX_W_CHUNKS, ni), 0, -1)
                   if ni % c == 0 and k % (c * BF16_SUBLANES) == 0)
    return tm, tn, k // nchunks, nchunks, ni // nchunks


def _linear(xp, xs, w, layer, *, col0=0, n=None, res=None, name):
    mp, k = xp.shape
    ms = xs.shape[0]
    n = w.shape[2] if n is None else n
    tm, tn, kc, nchunks, spc = _linear_tiles(mp, k, n, col0)
    assert mp % tm == 0 and n % tn == 0
    in_specs = [pl.BlockSpec((tm, k), lambda j, i: (i, 0)),
                pl.BlockSpec((ms, k), lambda j, i: (0, 0)),
                pl.BlockSpec(memory_space=pl.ANY)]
    args = [xp, xs, w]
    blocks = (2 * (_nbytes((tm, k), BF16) + _nbytes((ms, k), xs.dtype)
                   + _nbytes((tm, tn), F32) + _nbytes((ms, tn), F32))
              + 2 * _nbytes((k, tn), BF16) + 2 * _nbytes((kc, tn), F32))
    if res is not None:
        in_specs += [pl.BlockSpec((tm, tn), lambda j, i: (i, j)),
                     pl.BlockSpec((ms, tn), lambda j, i: (0, j))]
        args += list(res)
        blocks += 2 * (_nbytes((tm, tn), F32) + _nbytes((ms, tn), F32))
    return pl.pallas_call(
        functools.partial(_linear_body, layer=layer, col0=col0, kc=kc, nchunks=nchunks, spc=spc,
                          has_res=res is not None),
        grid=(n // tn, mp // tm),
        in_specs=in_specs,
        out_specs=[pl.BlockSpec((tm, tn), lambda j, i: (i, j)),
                   pl.BlockSpec((ms, tn), lambda j, i: (0, j))],
        out_shape=[jax.ShapeDtypeStruct((mp, n), F32), jax.ShapeDtypeStruct((ms, n), F32)],
        scratch_shapes=[pltpu.VMEM((2, k, tn), BF16), pltpu.VMEM((2, kc, tn), F32),
                        pltpu.SemaphoreType.DMA((2,))],
        compiler_params=pltpu.CompilerParams(
            dimension_semantics=("arbitrary", "arbitrary"),
            vmem_limit_bytes=_vmem_limit(blocks)),
        name=name,
    )(*args)


FFN_STEP_ROWS = 2048
FFN_SUB_ROWS = 256


def _ffn_up_body(hp_ref, hs_ref, w_hbm, cw_ref, cb_ref, exts_ref,
                 actp_ref, acts_ref, tailp_ref, as_ref,
                 wcat_ref, stg_ref, sem, ext_ref, *, layer, f, kc, nchunks, tm, tlen, ts):
    j = pl.program_id(0)
    i = pl.program_id(1)
    nj = pl.num_programs(0)
    tf = cw_ref.shape[1]
    cur = j % 2
    cw = cw_ref[...]
    cb = cb_ref[...]

    def act_fn(a2, a1, a0, b):
        conv = cb + cw[0:1, :] * a2 + cw[1:2, :] * a1 + cw[2:3, :] * a0
        return jax.nn.silu(conv) * b

    def chunk_copy(jt, c, half):
        rows = pl.ds(pl.multiple_of(c * kc, kc), kc)
        cols = pl.ds(pl.multiple_of(half * f + jt * tf, tf), tf)
        return pltpu.make_async_copy(w_hbm.at[layer, rows, cols], stg_ref.at[c % 2, half],
                                     sem.at[c % 2, half])

    def start(jt, c):
        for half in (0, 1):
            chunk_copy(jt, c, half).start()

    def land(jt, c, wslot):
        rows = pl.ds(pl.multiple_of(c * kc, kc), kc)
        for half in (0, 1):
            chunk_copy(jt, c, half).wait()
            wcat_ref[wslot, rows, half * tf:(half + 1) * tf] = stg_ref[c % 2, half].astype(BF16)

    @pl.when(jnp.logical_and(j == 0, i == 0))
    def _first_tile():
        start(0, 0)

        def step(c, carry):
            @pl.when(c + 1 < nchunks)
            def _():
                start(0, c + 1)
            land(0, c, 0)
            return carry

        lax.fori_loop(0, nchunks, step, 0)

    @pl.when(jnp.logical_and(j > 0, i == 0))
    def _finish_prefetched_tile():
        land(j, nchunks - 1, cur)

    @pl.when(j + 1 < nj)
    def _prefetch_next_tile():
        start(j + 1, i)

        @pl.when(i > 0)
        def _():
            land(j + 1, i - 1, 1 - cur)

    @pl.when(i == 0)
    def _sample_rows():
        ab = jnp.dot(hs_ref[...].astype(BF16), wcat_ref[cur], preferred_element_type=F32)
        a, b = ab[:, 0:tf], ab[:, tf:2 * tf]
        ms = a.shape[0]
        t = lax.broadcasted_iota(jnp.int32, a.shape, 0) % ts
        ext = exts_ref[...]
        a1 = jnp.where(t == 0, pltpu.roll(ext, ms - 1, axis=0), pltpu.roll(a, 1, axis=0))
        a2 = jnp.where(t < 2, ext, pltpu.roll(a, 2, axis=0))
        acts_ref[...] = act_fn(a2, a1, a, b).astype(acts_ref.dtype)
        as_ref[...] = a

    @pl.when((i * tm) % tlen == 0)
    def _sequence_start():
        ext_ref[...] = jnp.zeros(ext_ref.shape, F32)

    tail = ext_ref[...]
    for s in range(tm // FFN_SUB_ROWS):
        rows = slice(s * FFN_SUB_ROWS, (s + 1) * FFN_SUB_ROWS)
        hp = hp_ref[rows, :]
        ab = jnp.dot(hp, wcat_ref[cur], preferred_element_type=F32)
        a, b = ab[:, 0:tf], ab[:, tf:2 * tf]
        ext = jnp.concatenate([tail, a], axis=0)
        a1 = ext[7:7 + FFN_SUB_ROWS, :]
        a2 = ext[6:6 + FFN_SUB_ROWS, :]
        actp_ref[rows, :] = act_fn(a2, a1, a, b).astype(actp_ref.dtype)
        tail = a[FFN_SUB_ROWS - 8:FFN_SUB_ROWS, :]
    ext_ref[...] = tail

    @pl.when(((i + 1) * tm) % tlen == 0)
    def _sequence_end():
        n = (i * tm) // tlen
        tailp_ref[pl.ds(pl.multiple_of(n * 8, 8), 8), :] = tail


def _ffn_up(hp, hs, w_up, conv_w, conv_b, ext_s, layer, *, tlen, ts):
    mp, d = hp.shape
    ms = hs.shape[0]
    f = w_up.shape[2] // 2
    nseq = mp // tlen
    tm = min(FFN_STEP_ROWS, tlen)
    tf = 256
    assert mp % tm == 0 and tlen % tm == 0 and f % tf == 0 and tm % FFN_SUB_ROWS == 0
    nf = f // tf
    nchunks = mp // tm
    assert d % (nchunks * BF16_SUBLANES) == 0
    kc = d // nchunks
    blocks = (2 * (_nbytes((tm, d), BF16) + _nbytes((ms, d), hs.dtype)
                   + _nbytes((tm, tf), BF16) + 3 * _nbytes((ms, tf), F32) + _nbytes((nseq * 8, tf), F32))
              + 2 * _nbytes((d, 2 * tf), BF16) + 4 * _nbytes((kc, tf), F32)
              + _nbytes((8, tf), F32) + 5 * _nbytes((tm, tf), F32))
    return pl.pallas_call(
        functools.partial(_ffn_up_body, layer=layer, f=f, kc=kc, nchunks=nchunks, tm=tm, tlen=tlen,
                          ts=ts),
        grid=(nf, mp // tm),
        in_specs=[pl.BlockSpec((tm, d), lambda j, i: (i, 0)),
                  pl.BlockSpec((ms, d), lambda j, i: (0, 0)),
                  pl.BlockSpec(memory_space=pl.ANY),
                  pl.BlockSpec((None, CONV_W, tf), lambda j, i: (layer, 0, j)),
                  pl.BlockSpec((None, 1, tf), lambda j, i: (layer, 0, j)),
                  pl.BlockSpec((ms, tf), lambda j, i: (0, j))],
        out_specs=[pl.BlockSpec((tm, tf), lambda j, i: (i, j)),
                   pl.BlockSpec((ms, tf), lambda j, i: (0, j)),
                   pl.BlockSpec((nseq * 8, tf), lambda j, i: (0, j)),
                   pl.BlockSpec((ms, tf), lambda j, i: (0, j))],
        out_shape=[jax.ShapeDtypeStruct((mp, f), BF16),
                   jax.ShapeDtypeStruct((ms, f), BF16),
                   jax.ShapeDtypeStruct((nseq * 8, f), F32),
                   jax.ShapeDtypeStruct((ms, f), F32)],
        scratch_shapes=[pltpu.VMEM((2, d, 2 * tf), BF16), pltpu.VMEM((2, 2, kc, tf), F32),
                        pltpu.SemaphoreType.DMA((2, 2)), pltpu.VMEM((8, tf), F32)],
        compiler_params=pltpu.CompilerParams(
            dimension_semantics=("arbitrary", "arbitrary"),
            vmem_limit_bytes=_vmem_limit(blocks)),
        name="ffn_up",
    )(hp, hs, w_up, conv_w, conv_b, ext_s)


def _level_table(c):
    t = np.arange(c)[:, None]
    s = np.arange(c)[None, :]
    x = np.maximum(t ^ s, 1)
    lev = np.floor(np.log2(x)).astype(np.int32)
    return np.where(t > s, lev, np.where(t == s, -1, -2)).astype(np.int32)


def _cumsum_table(c):
    return np.tril(np.ones((c, c), np.float32))


def _split3(x):
    x0 = x.astype(BF16)
    r1 = x - x0.astype(F32)
    x1 = r1.astype(BF16)
    x2 = (r1 - x1.astype(F32)).astype(BF16)
    return x0, x1, x2


def _block_boundary_rows(b, blk):
    c, w = b.shape
    half = blk // 2
    if blk >= 2 * V7X_SUBLANES:
        x = b.reshape(c // blk, blk, w)
        r = jnp.broadcast_to(x[:, half - 1:half, :], x.shape)
        return r.reshape(c, w)
    x = b.reshape(c // V7X_SUBLANES, V7X_SUBLANES, w)
    sub = lax.broadcasted_iota(jnp.int32, x.shape, 1)

    def row(i):
        return jnp.broadcast_to(x[:, i:i + 1, :], x.shape)

    if blk == 8:
        r = row(3)
    elif blk == 4:
        r = jnp.where(sub < 4, row(1), row(5))
    else:
        r = jnp.where(sub < 2, row(0), jnp.where(sub < 4, row(2), jnp.where(sub < 6, row(4), row(6))))
    return r.reshape(c, w)


_NT = (((1,), (1,)), ((), ()))
_TN = (((0,), (0,)), ((), ()))


def _gla_body(*refs, nseq, tlen, valid, has_s0, heads):
    if has_s0:
        (q_ref, f_ref, v_ref, g_ref, lb_ref, gain_ref, lev_ref, tri_ref, s0_ref,
         o_ref, so_ref) = refs
    else:
        q_ref, f_ref, v_ref, g_ref, lb_ref, gain_ref, lev_ref, tri_ref, o_ref, so_ref = refs
    for hh in range(heads):
        _gla_head(q_ref, f_ref, v_ref, g_ref, lb_ref, gain_ref, lev_ref, tri_ref,
                  s0_ref if has_s0 else None, o_ref, so_ref, hh, nseq=nseq, tlen=tlen, valid=valid)


def _gla_head(q_ref, f_ref, v_ref, g_ref, lb_ref, gain_ref, lev_ref, tri_ref, s0_ref, o_ref, so_ref,
              hh, *, nseq, tlen, valid):
    has_s0 = s0_ref is not None
    cols = slice(hh * HEAD, (hh + 1) * HEAD)
    c = GLA_CHUNK
    rows_in = min(c, valid)
    nchunk = max(tlen // c, 1)
    lb = lb_ref[:, cols]
    log_lb = jnp.log(lb)
    log1m_lb = jnp.log1p(-lb)
    one_m_lb = 1.0 - lb
    gain = gain_ref[:, cols]
    n_levels = int(math.log2(rows_in))
    assert 1 << n_levels == rows_in

    def pad_rows(x):
        if rows_in == c:
            return x
        return jnp.concatenate([x, jnp.zeros((c - rows_in, x.shape[1]), x.dtype)], axis=0)

    def chunk(n, ci, st):
        row0 = pl.multiple_of(n * tlen + ci * rows_in, rows_in)
        rows = pl.ds(row0, rows_in)
        qr, fz, v = q_ref[rows, cols], f_ref[rows, cols], v_ref[rows, cols]
        e = jnp.exp(-jnp.abs(fz))
        one_p_e = 1.0 + e
        log_sig = jnp.minimum(fz, 0.0) - jnp.log(one_p_e)
        u = log1m_lb + log_sig
        log_f = jnp.maximum(log_lb, u) + jnp.log(1.0 + jnp.exp(-jnp.abs(log_lb - u)))
        k = one_m_lb * (jnp.where(fz > 0.0, e, 1.0) / one_p_e)
        q = jax.nn.silu(qr) * (HEAD ** -0.5)
        vb = pad_rows(v.astype(BF16))

        tri = tri_ref[...]
        b = sum(jnp.dot(tri, pad_rows(part), preferred_element_type=F32) for part in _split3(log_f))
        b = b * LOG2_E
        b_last = b[c - 1:c, :]
        b = b[0:rows_in, :]

        lev = lev_ref[...]
        diag = pad_rows(jnp.sum(q * k, axis=-1, keepdims=True))
        att = jnp.where(lev == -1, diag, 0.0)
        for l in range(n_levels):
            r = _block_boundary_rows(b, 2 << l)
            decay = jnp.exp2(-jnp.abs(b - r))
            qt = pad_rows((q * decay).astype(BF16))
            kt = pad_rows((k * decay).astype(BF16))
            s_l = lax.dot_general(qt, kt, _NT, preferred_element_type=F32)
            att = jnp.where(lev == l, s_l, att)
        o = jnp.dot(att.astype(BF16), vb, preferred_element_type=F32)
        o = o + lax.dot_general(pad_rows((q * jnp.exp2(b)).astype(BF16)), st.astype(BF16), _NT,
                                preferred_element_type=F32)
        kd = pad_rows((k * jnp.exp2(b_last - b)).astype(BF16))
        st_new = st * jnp.exp2(b_last) + lax.dot_general(vb, kd, _TN, preferred_element_type=F32)

        o = o[0:rows_in, :]
        o = o * lax.rsqrt(jnp.mean(o * o, axis=-1, keepdims=True) + EPS) * gain
        o = o * jax.nn.silu(g_ref[rows, cols])
        o_ref[rows, cols] = o.astype(o_ref.dtype)
        return st_new

    def sequence(n, carry):
        if has_s0:
            st0 = s0_ref[n, hh].T
        else:
            st0 = jnp.zeros((HEAD, HEAD), F32)
        st = lax.fori_loop(0, nchunk, lambda ci, st: chunk(n, ci, st), st0,
                           unroll=min(GLA_CHUNK_UNROLL, nchunk))
        so_ref[n, hh] = st.T
        return carry

    lax.fori_loop(0, nseq, sequence, 0, unroll=min(nseq, GLA_SEQ_UNROLL))


def _gla(proj, lb, gain, s0, *, nseq_total, tlen, out_dtype):
    m, d4 = proj.shape
    d = d4 // 4
    nh = d // HEAD
    nseq = 1 if tlen >= GLA_CHUNK else nseq_total
    assert tlen % GLA_CHUNK == 0 or tlen < GLA_CHUNK
    rows = nseq * tlen
    hps = GLA_HEADS_PER_STEP if nh % GLA_HEADS_PER_STEP == 0 else 1
    wide = hps * HEAD
    ngrp = nh // hps
    col = lambda off: pl.BlockSpec((rows, wide), lambda n, h: (n, h + off))
    vec = pl.BlockSpec((1, wide), lambda n, h: (0, h))
    const = pl.BlockSpec((GLA_CHUNK, GLA_CHUNK), lambda n, h: (0, 0))
    in_specs = [col(0), col(ngrp), col(2 * ngrp), col(3 * ngrp), vec, vec, const, const]
    args = [proj, proj, proj, proj, lb.reshape(1, d), gain.reshape(1, d),
            jnp.asarray(_level_table(GLA_CHUNK)), jnp.asarray(_cumsum_table(GLA_CHUNK), BF16)]
    state_spec = pl.BlockSpec((nseq, hps, HEAD, HEAD), lambda n, h: (n, h, 0, 0))
    if s0 is not None:
        in_specs.append(state_spec)
        args.append(s0)
    blocks = 2 * (4 * _nbytes((rows, wide), F32) + _nbytes((rows, wide), out_dtype)
                  + 2 * _nbytes((nseq, hps, HEAD, HEAD), F32)) + (4 << 20)
    return pl.pallas_call(
        functools.partial(_gla_body, nseq=nseq, tlen=tlen, valid=tlen, has_s0=s0 is not None,
                          heads=hps),
        grid=(nseq_total // nseq, ngrp),
        in_specs=in_specs,
        out_specs=[pl.BlockSpec((rows, wide), lambda n, h: (n, h)), state_spec],
        out_shape=[jax.ShapeDtypeStruct((m, d), out_dtype),
                   jax.ShapeDtypeStruct((nseq_total, nh, HEAD, HEAD), F32)],
        compiler_params=pltpu.CompilerParams(
            dimension_semantics=("arbitrary", "arbitrary"),
            vmem_limit_bytes=_vmem_limit(blocks)),
        name="hgrn2_recurrence",
    )(*args)


def _t5_buckets(dist):
    max_exact = N_BUCKETS // 2
    n = np.asarray(dist, dtype=np.int64)
    large = max_exact + (np.log(np.maximum(n, 1) / max_exact)
                         / np.log(BUCKET_MAX_DIST / max_exact)
                         * (N_BUCKETS - max_exact)).astype(np.int64)
    large = np.minimum(large, N_BUCKETS - 1)
    return np.where(n < max_exact, n, large).astype(np.int32)


def _bucket_lookup(tab, buckets):
    flat = jnp.asarray(buckets.reshape(1, -1))
    onehot = (flat == lax.broadcasted_iota(jnp.int32, (N_BUCKETS, flat.shape[1]), 0)).astype(F32)
    out = jnp.einsum("bh,bx->hx", tab, onehot, precision=lax.Precision.HIGHEST)
    return out.reshape((tab.shape[1],) + buckets.shape)


def _prompt_bias_tiles(rel_bias, nh):
    bl = SUB_WINDOW
    off = np.arange(bl)[:, None] + bl - np.arange(2 * bl)[None, :]
    band = (off >= 0) & (off <= SUB_WINDOW)
    tiles = []
    for g, dil in enumerate(DILATIONS):
        buckets = _t5_buckets(np.clip(off, 0, SUB_WINDOW) * dil)
        tab = rel_bias[:, g * nh:(g + 1) * nh].astype(F32)
        tiles.append(jnp.where(band[None], _bucket_lookup(tab, buckets), NEG))
    return jnp.stack(tiles)


ATTN_BLOCK_UNROLL = 16


def _attn_prompt_body(q1_ref, q2_ref, q3_ref, k_ref, v_ref, bias_ref, o_ref,
                      acc_ref, lse_ref, k4_ref, v4_ref, q34_ref, *, tlen):
    bl = SUB_WINDOW
    scale = HEAD ** -0.5
    mid = DILATIONS[1]
    seg = tlen // mid

    for src_ref, dst_ref in ((k_ref, k4_ref), (v_ref, v4_ref), (q3_ref, q34_ref)):
        for r in range(mid):
            dst_ref[r * seg:(r + 1) * seg, :] = src_ref[pl.ds(r, seg, stride=mid), :]

    for g, (dil, q_ref) in enumerate(zip(DILATIONS, (q1_ref, q2_ref, q3_ref))):
        nblk = tlen // (dil * bl)
        with_prev = nblk > 1

        def block(it, carry, dil=dil, q_ref=q_ref, nblk=nblk, g=g, with_prev=with_prev):
            r = it // nblk
            bi = it % nblk
            start = r + dil * bl * bi
            qrows = pl.ds(start, bl, stride=dil) if dil > 1 else pl.ds(start, bl)
            if with_prev:
                nkey = 2 * bl
                bias = bias_ref[g, 0]
                bias_first = jnp.concatenate([bias[:, bl:2 * bl], jnp.full((bl, bl), NEG, F32)], axis=1)
                bias = jnp.where(bi == 0, bias_first, bias)
                back = jnp.where(bi > 0, bl, 0)
            else:
                nkey = bl
                bias = bias_ref[g, 0, :, bl:2 * bl]
                back = 0
            if dil == 1:
                krows = pl.ds(start - back, nkey)
                qb, k2, v2 = q_ref[qrows, :], k_ref[krows, :], v_ref[krows, :]
            elif dil == mid:
                krows = pl.ds(r * seg + bl * bi - back, nkey)
                qb, k2, v2 = q_ref[qrows, :], k4_ref[krows, :], v4_ref[krows, :]
            else:
                sub = dil // mid
                rows4 = pl.ds((r % mid) * seg + r // mid + sub * (bl * bi - back), nkey, stride=sub)
                q4rows = pl.ds((r % mid) * seg + r // mid + sub * bl * bi, bl, stride=sub)
                qb, k2, v2 = q34_ref[q4rows, :], k4_ref[rows4, :], v4_ref[rows4, :]
            qb, k2, v2 = qb.astype(BF16), k2.astype(BF16), v2.astype(BF16)
            s = lax.dot_general(qb, k2, _NT, preferred_element_type=F32) * scale + bias
            mx = jnp.max(s, axis=-1, keepdims=True)
            p = jnp.exp(s - mx)
            sm = jnp.sum(p, axis=-1, keepdims=True)
            acc = jnp.dot(p.astype(BF16), v2, preferred_element_type=F32)
            acc_ref[g, qrows, :] = acc / sm
            lse_ref[g, qrows, :] = jnp.broadcast_to(mx + jnp.log(sm), (bl, HEAD))
            return carry

        def residue_pair(it, carry, dil=dil, g=g):
            sub = dil // mid
            parts = []
            for r in (2 * it, 2 * it + 1):
                rows4 = pl.ds((r % mid) * seg + r // mid, bl, stride=sub)
                parts.append((q34_ref[rows4, :], k4_ref[rows4, :], v4_ref[rows4, :],
                              pl.ds(r, bl, stride=dil)))
            qb = jnp.concatenate([pt[0] for pt in parts], axis=0).astype(BF16)
            k2 = jnp.concatenate([pt[1] for pt in parts], axis=0).astype(BF16)
            v2 = jnp.concatenate([pt[2] for pt in parts], axis=0).astype(BF16)
            own = bias_ref[g, 0, :, bl:2 * bl]
            neg = jnp.full((bl, bl), NEG, F32)
            bias = jnp.concatenate([jnp.concatenate([own, neg], axis=1),
                                    jnp.concatenate([neg, own], axis=1)], axis=0)
            s = lax.dot_general(qb, k2, _NT, preferred_element_type=F32) * scale + bias
            mx = jnp.max(s, axis=-1, keepdims=True)
            p = jnp.exp(s - mx)
            sm = jnp.sum(p, axis=-1, keepdims=True)
            o = jnp.dot(p.astype(BF16), v2, preferred_element_type=F32) / sm
            lse = jnp.broadcast_to(mx + jnp.log(sm), (2 * bl, HEAD))
            for half, pt in enumerate(parts):
                acc_ref[g, pt[3], :] = o[half * bl:(half + 1) * bl, :]
                lse_ref[g, pt[3], :] = lse[half * bl:(half + 1) * bl, :]
            return carry

        if with_prev:
            lax.fori_loop(0, dil * nblk, block, 0, unroll=ATTN_BLOCK_UNROLL)
        else:
            lax.fori_loop(0, dil // 2, residue_pair, 0, unroll=ATTN_BLOCK_UNROLL // 2)

    def merge(ci, carry):
        rows = pl.ds(pl.multiple_of(ci * bl, bl), bl)
        l1, l2, l3 = lse_ref[0, rows, :], lse_ref[1, rows, :], lse_ref[2, rows, :]
        mm = jnp.maximum(jnp.maximum(l1, l2), l3)
        w1, w2, w3 = jnp.exp(l1 - mm), jnp.exp(l2 - mm), jnp.exp(l3 - mm)
        num = w1 * acc_ref[0, rows, :] + w2 * acc_ref[1, rows, :] + w3 * acc_ref[2, rows, :]
        o_ref[rows, :] = (num / (w1 + w2 + w3)).astype(o_ref.dtype)
        return carry

    lax.fori_loop(0, tlen // bl, merge, 0, unroll=2)


def _attn_prompt(qproj, k, v, bias_tiles, *, nseq, tlen):
    m, d3 = qproj.shape
    d = d3 // 3
    nh = d // HEAD
    qspec = lambda g: pl.BlockSpec((tlen, HEAD), lambda n, h: (n, g * nh + h))
    kvspec = pl.BlockSpec((tlen, HEAD), lambda n, h: (n, h))
    blocks = (2 * (5 * _nbytes((tlen, HEAD), F32) + _nbytes((3, 1, SUB_WINDOW, 2 * SUB_WINDOW), F32)
                   + _nbytes((tlen, HEAD), BF16))
              + 9 * _nbytes((tlen, HEAD), F32))
    return pl.pallas_call(
        functools.partial(_attn_prompt_body, tlen=tlen),
        grid=(nseq, nh),
        in_specs=[qspec(0), qspec(1), qspec(2), kvspec, kvspec,
                  pl.BlockSpec((3, 1, SUB_WINDOW, 2 * SUB_WINDOW), lambda n, h: (0, h, 0, 0))],
        out_specs=pl.BlockSpec((tlen, HEAD), lambda n, h: (n, h)),
        out_shape=jax.ShapeDtypeStruct((m, d), BF16),
        scratch_shapes=[pltpu.VMEM((3, tlen, HEAD), F32), pltpu.VMEM((3, tlen, HEAD), F32),
                        pltpu.VMEM((tlen, HEAD), F32), pltpu.VMEM((tlen, HEAD), F32),
                        pltpu.VMEM((tlen, HEAD), F32)],
        compiler_params=pltpu.CompilerParams(
            dimension_semantics=("arbitrary", "arbitrary"),
            vmem_limit_bytes=_vmem_limit(blocks)),
        name="dilated_attention_prompt",
    )(qproj, qproj, qproj, k, v, bias_tiles)


SAMPLE_HEADS_PER_STEP = V7X_SUBLANES


def _sample_key_rows(p_len, ts):
    period = max(DILATIONS)
    assert p_len % period == 0 and ts < period
    n_i = p_len // period
    tail_i0 = (p_len - DILATIONS[1] * SUB_WINDOW) // period
    assert tail_i0 >= 0 and DILATIONS[0] * SUB_WINDOW <= DILATIONS[1] * SUB_WINDOW
    main = (np.arange(n_i)[:, None] * period + np.arange(ts)[None, :]).reshape(-1)
    tail = (np.arange(tail_i0, n_i)[:, None] * period + np.arange(ts, period)[None, :]).reshape(-1)
    rows = np.concatenate([main, tail, p_len + np.arange(ts)])
    nkeys = -(-rows.size // V7X_LANES) * V7X_LANES
    return np.concatenate([rows, np.full(nkeys - rows.size, -1)]), n_i, n_i - tail_i0


def _sample_bias_table(rel_bias, nh, p_len, ts, key_rows):
    t = np.arange(ts)[:, None]
    row = key_rows[None, :]
    dist = p_len + t - row
    out = []
    for g, dil in enumerate(DILATIONS):
        j = dist // dil
        ok = (row >= 0) & (dist >= 0) & (dist % dil == 0) & (j <= SUB_WINDOW)
        buckets = _t5_buckets(np.clip(j, 0, SUB_WINDOW) * dil)
        tab = rel_bias[:, g * nh:(g + 1) * nh].astype(F32)
        out.append(jnp.where(ok[None], _bucket_lookup(tab, buckets), NEG))
    return jnp.concatenate(out, axis=1)


def _attn_sample_body(q1_ref, q2_ref, q3_ref, km_ref, kt_ref, vm_ref, vt_ref, kn_ref, vn_ref,
                      bias_ref, o_ref, kall_ref, vall_ref, *, ts):
    nkeys = kall_ref.shape[0]
    n_main = km_ref.shape[0] * km_ref.shape[1]
    n_tail = kt_ref.shape[0] * kt_ref.shape[1]
    n_used = n_main + n_tail + ts
    scale = HEAD ** -0.5
    kall_ref[n_used:nkeys, :] = jnp.zeros((nkeys - n_used, HEAD), F32)
    vall_ref[n_used:nkeys, :] = jnp.zeros((nkeys - n_used, HEAD), F32)
    for hh in range(SAMPLE_HEADS_PER_STEP):
        cols = slice(hh * HEAD, (hh + 1) * HEAD)
        for src_main, src_tail, src_new, dst in ((km_ref, kt_ref, kn_ref, kall_ref),
                                                 (vm_ref, vt_ref, vn_ref, vall_ref)):
            flat_main = src_main.reshape(n_main * SAMPLE_HEADS_PER_STEP, HEAD)
            flat_tail = src_tail.reshape(n_tail * SAMPLE_HEADS_PER_STEP, HEAD)
            dst[0:n_main, :] = flat_main[pl.ds(hh, n_main, stride=SAMPLE_HEADS_PER_STEP), :]
            dst[n_main:n_main + n_tail, :] = flat_tail[pl.ds(hh, n_tail, stride=SAMPLE_HEADS_PER_STEP), :]
            dst[n_main + n_tail:n_used, :] = src_new[:, cols]
        kb = kall_ref[...].astype(BF16)
        vb = vall_ref[...].astype(BF16)
        bias = bias_ref[hh]
        s = []
        for g, q_ref in enumerate((q1_ref, q2_ref, q3_ref)):
            sg = lax.dot_general(q_ref[:, cols].astype(BF16), kb, _NT, preferred_element_type=F32)
            s.append(sg * scale + bias[g * ts:(g + 1) * ts, :])
        mx = [jnp.max(x, axis=-1, keepdims=True) for x in s]
        mm = jnp.maximum(jnp.maximum(mx[0], mx[1]), mx[2])
        p = jnp.exp(s[0] - mm) + jnp.exp(s[1] - mm) + jnp.exp(s[2] - mm)
        den = jnp.sum(p, axis=-1, keepdims=True)
        o = jnp.dot(p.astype(BF16), vb, preferred_element_type=F32)
        o_ref[:, cols] = (o / den).astype(o_ref.dtype)


def _attn_sample(qproj, k_new, v_new, cache_k, cache_v, bias_table, n_i, n_tail, *, nseq, ts):
    m, d3 = qproj.shape
    d = d3 // 3
    nh = d // HEAD
    hps = SAMPLE_HEADS_PER_STEP
    assert nh % hps == 0 and n_i % n_tail == 0
    p_len = cache_k.shape[1]
    period = p_len // n_i
    nkeys = bias_table.shape[-1]
    ngrp = nh // hps
    wide = hps * HEAD
    qspec = lambda g: pl.BlockSpec((ts, wide), lambda n, h: (n, g * ngrp + h))
    newspec = pl.BlockSpec((ts, wide), lambda n, h: (n, h))
    mainspec = pl.BlockSpec((None, n_i, ts, hps, HEAD), lambda n, h: (n, 0, 0, h, 0))
    tailspec = pl.BlockSpec((None, n_tail, period - ts, hps, HEAD),
                            lambda n, h: (n, n_i // n_tail - 1, 1, h, 0))
    assert period - ts == ts
    cache_bytes = _nbytes((n_i + n_tail, ts, hps, HEAD), F32)
    blocks = (2 * (2 * cache_bytes + _nbytes((hps, 3 * ts, nkeys), F32))
              + 2 * _nbytes((nkeys, HEAD), F32) + 2 * _nbytes((nkeys, HEAD), BF16)
              + 8 * _nbytes((3 * ts, nkeys), F32))
    k5 = cache_k.reshape(nseq, n_i, period, nh, HEAD)
    v5 = cache_v.reshape(nseq, n_i, period, nh, HEAD)
    return pl.pallas_call(
        functools.partial(_attn_sample_body, ts=ts),
        grid=(nseq, ngrp),
        in_specs=[qspec(0), qspec(1), qspec(2), mainspec, tailspec, mainspec, tailspec,
                  newspec, newspec,
                  pl.BlockSpec((hps, 3 * ts, nkeys), lambda n, h: (h, 0, 0))],
        out_specs=pl.BlockSpec((ts, wide), lambda n, h: (n, h)),
        out_shape=jax.ShapeDtypeStruct((m, d), F32),
        scratch_shapes=[pltpu.VMEM((nkeys, HEAD), F32), pltpu.VMEM((nkeys, HEAD), F32)],
        compiler_params=pltpu.CompilerParams(
            dimension_semantics=("arbitrary", "arbitrary"),
            vmem_limit_bytes=_vmem_limit(blocks)),
        name="dilated_attention_sample",
    )(qproj, qproj, qproj, k5, k5, v5, v5, k_new, v_new, bias_table)


def _lower_bounds(lb_raw):
    p = jax.nn.softmax(lb_raw.astype(F32), axis=0)
    c = jnp.cumsum(p, axis=0)
    return c - c[0:1]


def kernel(x_prompt, x_sample, state_hgrn, state_ffn_conv, cache_k_win, cache_v_win, a_w_in, a_lb_raw, a_out_gain, a_w_out, kv_norm, w_kv, b_w_q, b_w_out, rel_bias, ffn_w_up, ffn_conv_w, ffn_conv_b, ffn_w_down, norm_mix, norm_ffn, norm_final):
    nb_p, t_p, d = x_prompt.shape
    nb_s, t_s, _ = x_sample.shape
    depth = ffn_w_up.shape[0]
    n_a = a_w_in.shape[0]
    nh = d // HEAD
    f = ffn_w_down.shape[1]
    p_len = cache_k_win.shape[1]
    assert t_p % (SUB_WINDOW * max(DILATIONS)) == 0 and t_s == V7X_SUBLANES

    xp = x_prompt.reshape(nb_p * t_p, d)
    xs = x_sample.reshape(nb_s * t_s, d)
    lbs = _lower_bounds(a_lb_raw)
    key_rows, n_i, n_tail = _sample_key_rows(p_len, t_s)
    bias_p = _prompt_bias_tiles(rel_bias, nh)
    bias_s = _sample_bias_table(rel_bias, nh, p_len, t_s, key_rows)

    hgrn_p, hgrn_s, conv_p, conv_s = [], [], [], []
    kp = vp = ks = vs = None
    w_kv3 = w_kv.reshape(1, d, 2 * d)
    conv_b3 = ffn_conv_b.reshape(depth, 1, f)
    for l in range(depth):
        gains = [norm_mix[l], kv_norm] if l == n_a else [norm_mix[l]]
        hp, *kvp = _rmsnorm(xp, gains, BF16)
        hs, *kvs = _rmsnorm(xs, gains, BF16)
        if l < n_a:
            pp, ps = _linear(hp, hs, a_w_in, l, name="hgrn2_in_proj")
            op, sp = _gla(pp, lbs[l], a_out_gain[l], None, nseq_total=nb_p, tlen=t_p, out_dtype=BF16)
            os_, ss = _gla(ps, lbs[l], a_out_gain[l], state_hgrn[l], nseq_total=nb_s, tlen=t_s,
                           out_dtype=F32)
            hgrn_p.append(sp)
            hgrn_s.append(ss)
            w_o, l_o = a_w_out, l
        else:
            if l == n_a:
                kp, ks = _linear(kvp[0], kvs[0], w_kv3, 0, col0=0, n=d, name="k_proj")
                vp, vs = _linear(kvp[0], kvs[0], w_kv3, 0, col0=d, n=d, name="v_proj")
            jb = l - n_a
            qp, qs = _linear(hp, hs, b_w_q, jb, name="q_proj")
            op = _attn_prompt(qp, kp, vp, bias_p, nseq=nb_p, tlen=t_p)
            os_ = _attn_sample(qs, ks, vs, cache_k_win, cache_v_win, bias_s, n_i, n_tail,
                               nseq=nb_s, ts=t_s)
            w_o, l_o = b_w_out, jb
        xp, xs = _linear(op, os_, w_o, l_o, res=(xp, xs), name="mixer_out_proj")

        hp, = _rmsnorm(xp, [norm_ffn[l]], BF16)
        hs, = _rmsnorm(xs, [norm_ffn[l]], BF16)
        prev = state_ffn_conv[l]
        ext_s = jnp.pad(prev, ((0, 0), (0, t_s - (CONV_W - 1)), (0, 0))).reshape(nb_s * t_s, f)
        actp, acts, tailp, a_s = _ffn_up(hp, hs, ffn_w_up, ffn_conv_w, conv_b3, ext_s, l,
                                         tlen=t_p, ts=t_s)
        conv_p.append(tailp.reshape(nb_p, 8, f)[:, 8 - (CONV_W - 1):, :])
        conv_s.append(a_s.reshape(nb_s, t_s, f)[:, t_s - (CONV_W - 1):, :])
        xp, xs = _linear(actp, acts, ffn_w_down, l, res=(xp, xs), name="ffn_down_proj")

    yp = _rmsnorm(xp, [norm_final], F32)[0].reshape(nb_p, t_p, d)
    ys = _rmsnorm(xs, [norm_final], F32)[0].reshape(nb_s, t_s, d)
    keep = min(BUCKET_MAX_DIST, t_p)
    k_win = kp.reshape(nb_p, t_p, nh, HEAD)[:, t_p - keep:]
    v_win = vp.reshape(nb_p, t_p, nh, HEAD)[:, t_p - keep:]
    return (yp, ys, jnp.stack(hgrn_p), jnp.stack(hgrn_s), jnp.stack(conv_p), jnp.stack(conv_s),
            k_win, v_win, ks.reshape(nb_s, t_s, nh, HEAD), vs.reshape(nb_s, t_s, nh, HEAD))
```

```python
import functools
import math

import numpy as np
import jax
import jax.numpy as jnp
from jax import lax
from jax.experimental import pallas as pl
from jax.experimental.pallas import tpu as pltpu

F32 = jnp.float32
BF16 = jnp.bfloat16

V7X_VMEM_BYTES = 64 * 1024 * 1024
V7X_LANES = 128
V7X_SUBLANES = 8
COMPILER_SCRATCH_BYTES = 12 * 1024 * 1024

HEAD = 128
GLA_CHUNK = 128
GLA_SEQ_UNROLL = 8
GLA_HEADS_PER_STEP = 2
GLA_CHUNK_UNROLL = 16
SUB_WINDOW = 128
DILATIONS = (1, 4, 16)
N_BUCKETS = 32
BUCKET_MAX_DIST = 2048
CONV_W = 3
EPS = 1e-6
NEG = -1e30
LOG2_E = 1.4426950408889634


def _vmem_limit(block_bytes):
    return int(min(block_bytes + COMPILER_SCRATCH_BYTES, V7X_VMEM_BYTES - (2 << 20)))


def _nbytes(shape, dtype):
    return int(np.prod(shape)) * jnp.dtype(dtype).itemsize


def _rmsnorm_body(x_ref, *refs):
    g_refs, o_refs = refs[:len(refs) // 2], refs[len(refs) // 2:]
    x = x_ref[...]
    ms = jnp.mean(x * x, axis=-1, keepdims=True)
    y = x * lax.rsqrt(ms + EPS)
    for g_ref, o_ref in zip(g_refs, o_refs):
        o_ref[...] = (y * g_ref[...]).astype(o_ref.dtype)


def _rmsnorm(x, gains, out_dtype):
    m, d = x.shape
    tm = min(m, 256)
    assert m % tm == 0
    n = len(gains)
    blocks = 2 * (_nbytes((tm, d), F32) + n * _nbytes((tm, d), out_dtype))
    row_spec = pl.BlockSpec((tm, d), lambda i: (i, 0))
    return pl.pallas_call(
        _rmsnorm_body,
        grid=(m // tm,),
        in_specs=[row_spec] + [pl.BlockSpec((1, d), lambda i: (0, 0))] * n,
        out_specs=[row_spec] * n,
        out_shape=[jax.ShapeDtypeStruct((m, d), out_dtype)] * n,
        compiler_params=pltpu.CompilerParams(vmem_limit_bytes=_vmem_limit(blocks)),
        name="rmsnorm",
    )(x, *[g.reshape(1, d) for g in gains])


def _linear_body(*refs, layer, col0, kc, nchunks, spc, has_res):
    if has_res:
        xp_ref, xs_ref, w_hbm, rp_ref, rs_ref, yp_ref, ys_ref, wb_ref, stg_ref, sem = refs
    else:
        xp_ref, xs_ref, w_hbm, yp_ref, ys_ref, wb_ref, stg_ref, sem = refs
    j = pl.program_id(0)
    i = pl.program_id(1)
    nj = pl.num_programs(0)
    tn = wb_ref.shape[2]
    cur = j % 2

    def chunk_copy(jt, c):
        rows = pl.ds(pl.multiple_of(c * kc, kc), kc)
        cols = pl.ds(pl.multiple_of(col0 + jt * tn, tn), tn)
        return pltpu.make_async_copy(w_hbm.at[layer, rows, cols], stg_ref.at[c % 2], sem.at[c % 2])

    def land(jt, c, wslot):
        chunk_copy(jt, c).wait()
        rows = pl.ds(pl.multiple_of(c * kc, kc), kc)
        wb_ref[wslot, rows, :] = stg_ref[c % 2].astype(BF16)

    @pl.when(jnp.logical_and(j == 0, i == 0))
    def _first_tile():
        chunk_copy(0, 0).start()

        def step(c, carry):
            @pl.when(c + 1 < nchunks)
            def _():
                chunk_copy(0, c + 1).start()
            land(0, c, 0)
            return carry

        lax.fori_loop(0, nchunks, step, 0)

    @pl.when(jnp.logical_and(j > 0, i == 0))
    def _finish_prefetched_tile():
        land(j, nchunks - 1, cur)

    @pl.when(jnp.logical_and(j + 1 < nj, i % spc == 0))
    def _prefetch_next_tile():
        c = i // spc
        chunk_copy(j + 1, c).start()

        @pl.when(c > 0)
        def _():
            land(j + 1, c - 1, 1 - cur)

    @pl.when(i == 0)
    def _sample_rows():
        ys = jnp.dot(xs_ref[...].astype(BF16), wb_ref[cur], preferred_element_type=F32)
        if has_res:
            ys = ys + rs_ref[...]
        ys_ref[...] = ys

    yp = jnp.dot(xp_ref[...], wb_ref[cur], preferred_element_type=F32)
    if has_res:
        yp = yp + rp_ref[...]
    yp_ref[...] = yp


BF16_SUBLANES = 16
MAX_W_CHUNKS = 16


def _linear_tiles(mp, k, n, col0):
    tm, tn_max = (512, 1024) if k <= 4096 else (256, 512)
    tm = min(tm, mp)
    tn = next(t for t in (tn_max, tn_max // 2, tn_max // 4, V7X_LANES)
              if n % t == 0 and col0 % t == 0)
    ni = mp // tm
    nchunks = next(c for c in range(min(MAX_W_CHUNKS, ni), 0, -1)
                   if ni % c == 0 and k % (c * BF16_SUBLANES) == 0)
    return tm, tn, k // nchunks, nchunks, ni // nchunks


def _linear(xp, xs, w, layer, *, col0=0, n=None, res=None, name):
    mp, k = xp.shape
    ms = xs.shape[0]
    n = w.shape[2] if n is None else n
    tm, tn, kc, nchunks, spc = _linear_tiles(mp, k, n, col0)
    assert mp % tm == 0 and n % tn == 0
    in_specs = [pl.BlockSpec((tm, k), lambda j, i: (i, 0)),
                pl.BlockSpec((ms, k), lambda j, i: (0, 0)),
                pl.BlockSpec(memory_space=pl.ANY)]
    args = [xp, xs, w]
    blocks = (2 * (_nbytes((tm, k), BF16) + _nbytes((ms, k), xs.dtype)
                   + _nbytes((tm, tn), F32) + _nbytes((ms, tn), F32))
              + 2 * _nbytes((k, tn), BF16) + 2 * _nbytes((kc, tn), F32))
    if res is not None:
        in_specs += [pl.BlockSpec((tm, tn), lambda j, i: (i, j)),
                     pl.BlockSpec((ms, tn), lambda j, i: (0, j))]
        args += list(res)
        blocks += 2 * (_nbytes((tm, tn), F32) + _nbytes((ms, tn), F32))
    return pl.pallas_call(
        functools.partial(_linear_body, layer=layer, col0=col0, kc=kc, nchunks=nchunks, spc=spc,
                          has_res=res is not None),
        grid=(n // tn, mp // tm),
        in_specs=in_specs,
        out_specs=[pl.BlockSpec((tm, tn), lambda j, i: (i, j)),
                   pl.BlockSpec((ms, tn), lambda j, i: (0, j))],
        out_shape=[jax.ShapeDtypeStruct((mp, n), F32), jax.ShapeDtypeStruct((ms, n), F32)],
        scratch_shapes=[pltpu.VMEM((2, k, tn), BF16), pltpu.VMEM((2, kc, tn), F32),
                        pltpu.SemaphoreType.DMA((2,))],
        compiler_params=pltpu.CompilerParams(
            dimension_semantics=("arbitrary", "arbitrary"),
            vmem_limit_bytes=_vmem_limit(blocks)),
        name=name,
    )(*args)


FFN_STEP_ROWS = 2048
FFN_SUB_ROWS = 256


def _ffn_up_body(hp_ref, hs_ref, w_hbm, cw_ref, cb_ref, exts_ref,
                 actp_ref, acts_ref, tailp_ref, as_ref,
                 wcat_ref, stg_ref, sem, ext_ref, *, layer, f, kc, nchunks, tm, tlen, ts):
    j = pl.program_id(0)
    i = pl.program_id(1)
    nj = pl.num_programs(0)
    tf = cw_ref.shape[1]
    cur = j % 2
    cw = cw_ref[...]
    cb = cb_ref[...]

    def act_fn(a2, a1, a0, b):
        conv = cb + cw[0:1, :] * a2 + cw[1:2, :] * a1 + cw[2:3, :] * a0
        return jax.nn.silu(conv) * b

    def chunk_copy(jt, c, half):
        rows = pl.ds(pl.multiple_of(c * kc, kc), kc)
        cols = pl.ds(pl.multiple_of(half * f + jt * tf, tf), tf)
        return pltpu.make_async_copy(w_hbm.at[layer, rows, cols], stg_ref.at[c % 2, half],
                                     sem.at[c % 2, half])

    def start(jt, c):
        for half in (0, 1):
            chunk_copy(jt, c, half).start()

    def land(jt, c, wslot):
        rows = pl.ds(pl.multiple_of(c * kc, kc), kc)
        for half in (0, 1):
            chunk_copy(jt, c, half).wait()
            wcat_ref[wslot, rows, half * tf:(half + 1) * tf] = stg_ref[c % 2, half].astype(BF16)

    @pl.when(jnp.logical_and(j == 0, i == 0))
    def _first_tile():
        start(0, 0)

        def step(c, carry):
            @pl.when(c + 1 < nchunks)
            def _():
                start(0, c + 1)
            land(0, c, 0)
            return carry

        lax.fori_loop(0, nchunks, step, 0)

    @pl.when(jnp.logical_and(j > 0, i == 0))
    def _finish_prefetched_tile():
        land(j, nchunks - 1, cur)

    @pl.when(j + 1 < nj)
    def _prefetch_next_tile():
        start(j + 1, i)

        @pl.when(i > 0)
        def _():
            land(j + 1, i - 1, 1 - cur)

    @pl.when(i == 0)
    def _sample_rows():
        ab = jnp.dot(hs_ref[...].astype(BF16), wcat_ref[cur], preferred_element_type=F32)
        a, b = ab[:, 0:tf], ab[:, tf:2 * tf]
        ms = a.shape[0]
        t = lax.broadcasted_iota(jnp.int32, a.shape, 0) % ts
        ext = exts_ref[...]
        a1 = jnp.where(t == 0, pltpu.roll(ext, ms - 1, axis=0), pltpu.roll(a, 1, axis=0))
        a2 = jnp.where(t < 2, ext, pltpu.roll(a, 2, axis=0))
        acts_ref[...] = act_fn(a2, a1, a, b).astype(acts_ref.dtype)
        as_ref[...] = a

    @pl.when((i * tm) % tlen == 0)
    def _sequence_start():
        ext_ref[...] = jnp.zeros(ext_ref.shape, F32)

    tail = ext_ref[...]
    for s in range(tm // FFN_SUB_ROWS):
        rows = slice(s * FFN_SUB_ROWS, (s + 1) * FFN_SUB_ROWS)
        hp = hp_ref[rows, :]
        ab = jnp.dot(hp, wcat_ref[cur], preferred_element_type=F32)
        a, b = ab[:, 0:tf], ab[:, tf:2 * tf]
        ext = jnp.concatenate([tail, a], axis=0)
        a1 = ext[7:7 + FFN_SUB_ROWS, :]
        a2 = ext[6:6 + FFN_SUB_ROWS, :]
        actp_ref[rows, :] = act_fn(a2, a1, a, b).astype(actp_ref.dtype)
        tail = a[FFN_SUB_ROWS - 8:FFN_SUB_ROWS, :]
    ext_ref[...] = tail

    @pl.when(((i + 1) * tm) % tlen == 0)
    def _sequence_end():
        n = (i * tm) // tlen
        tailp_ref[pl.ds(pl.multiple_of(n * 8, 8), 8), :] = tail


def _ffn_up(hp, hs, w_up, conv_w, conv_b, ext_s, layer, *, tlen, ts):
    mp, d = hp.shape
    ms = hs.shape[0]
    f = w_up.shape[2] // 2
    nseq = mp // tlen
    tm = min(FFN_STEP_ROWS, tlen)
    tf = 256
    assert mp % tm == 0 and tlen % tm == 0 and f % tf == 0 and tm % FFN_SUB_ROWS == 0
    nf = f // tf
    nchunks = mp // tm
    assert d % (nchunks * BF16_SUBLANES) == 0
    kc = d // nchunks
    blocks = (2 * (_nbytes((tm, d), BF16) + _nbytes((ms, d), hs.dtype)
                   + _nbytes((tm, tf), BF16) + 3 * _nbytes((ms, tf), F32) + _nbytes((nseq * 8, tf), F32))
              + 2 * _nbytes((d, 2 * tf), BF16) + 4 * _nbytes((kc, tf), F32)
              + _nbytes((8, tf), F32) + 5 * _nbytes((tm, tf), F32))
    return pl.pallas_call(
        functools.partial(_ffn_up_body, layer=layer, f=f, kc=kc, nchunks=nchunks, tm=tm, tlen=tlen,
                          ts=ts),
        grid=(nf, mp // tm),
        in_specs=[pl.BlockSpec((tm, d), lambda j, i: (i, 0)),
                  pl.BlockSpec((ms, d), lambda j, i: (0, 0)),
                  pl.BlockSpec(memory_space=pl.ANY),
                  pl.BlockSpec((None, CONV_W, tf), lambda j, i: (layer, 0, j)),
                  pl.BlockSpec((None, 1, tf), lambda j, i: (layer, 0, j)),
                  pl.BlockSpec((ms, tf), lambda j, i: (0, j))],
        out_specs=[pl.BlockSpec((tm, tf), lambda j, i: (i, j)),
                   pl.BlockSpec((ms, tf), lambda j, i: (0, j)),
                   pl.BlockSpec((nseq * 8, tf), lambda j, i: (0, j)),
                   pl.BlockSpec((ms, tf), lambda j, i: (0, j))],
        out_shape=[jax.ShapeDtypeStruct((mp, f), BF16),
                   jax.ShapeDtypeStruct((ms, f), BF16),
                   jax.ShapeDtypeStruct((nseq * 8, f), F32),
                   jax.ShapeDtypeStruct((ms, f), F32)],
        scratch_shapes=[pltpu.VMEM((2, d, 2 * tf), BF16), pltpu.VMEM((2, 2, kc, tf), F32),
                        pltpu.SemaphoreType.DMA((2, 2)), pltpu.VMEM((8, tf), F32)],
        compiler_params=pltpu.CompilerParams(
            dimension_semantics=("arbitrary", "arbitrary"),
            vmem_limit_bytes=_vmem_limit(blocks)),
        name="ffn_up",
    )(hp, hs, w_up, conv_w, conv_b, ext_s)


def _level_table(c):
    t = np.arange(c)[:, None]
    s = np.arange(c)[None, :]
    x = np.maximum(t ^ s, 1)
    lev = np.floor(np.log2(x)).astype(np.int32)
    return np.where(t > s, lev, np.where(t == s, -1, -2)).astype(np.int32)


def _cumsum_table(c):
    return np.tril(np.ones((c, c), np.float32))


def _split3(x):
    x0 = x.astype(BF16)
    r1 = x - x0.astype(F32)
    x1 = r1.astype(BF16)
    x2 = (r1 - x1.astype(F32)).astype(BF16)
    return x0, x1, x2


def _block_boundary_rows(b, blk):
    c, w = b.shape
    half = blk // 2
    if blk >= 2 * V7X_SUBLANES:
        x = b.reshape(c // blk, blk, w)
        r = jnp.broadcast_to(x[:, half - 1:half, :], x.shape)
        return r.reshape(c, w)
    x = b.reshape(c // V7X_SUBLANES, V7X_SUBLANES, w)
    sub = lax.broadcasted_iota(jnp.int32, x.shape, 1)

    def row(i):
        return jnp.broadcast_to(x[:, i:i + 1, :], x.shape)

    if blk == 8:
        r = row(3)
    elif blk == 4:
        r = jnp.where(sub < 4, row(1), row(5))
    else:
        r = jnp.where(sub < 2, row(0), jnp.where(sub < 4, row(2), jnp.where(sub < 6, row(4), row(6))))
    return r.reshape(c, w)


_NT = (((1,), (1,)), ((), ()))
_TN = (((0,), (0,)), ((), ()))


def _gla_body(*refs, nseq, tlen, valid, has_s0, heads):
    if has_s0:
        (q_ref, f_ref, v_ref, g_ref, lb_ref, gain_ref, lev_ref, tri_ref, s0_ref,
         o_ref, so_ref) = refs
    else:
        q_ref, f_ref, v_ref, g_ref, lb_ref, gain_ref, lev_ref, tri_ref, o_ref, so_ref = refs
    for hh in range(heads):
        _gla_head(q_ref, f_ref, v_ref, g_ref, lb_ref, gain_ref, lev_ref, tri_ref,
                  s0_ref if has_s0 else None, o_ref, so_ref, hh, nseq=nseq, tlen=tlen, valid=valid)


def _gla_head(q_ref, f_ref, v_ref, g_ref, lb_ref, gain_ref, lev_ref, tri_ref, s0_ref, o_ref, so_ref,
              hh, *, nseq, tlen, valid):
    has_s0 = s0_ref is not None
    cols = slice(hh * HEAD, (hh + 1) * HEAD)
    c = GLA_CHUNK
    rows_in = min(c, valid)
    nchunk = max(tlen // c, 1)
    lb = lb_ref[:, cols]
    log_lb = jnp.log(lb)
    log1m_lb = jnp.log1p(-lb)
    one_m_lb = 1.0 - lb
    gain = gain_ref[:, cols]
    n_levels = int(math.log2(rows_in))
    assert 1 << n_levels == rows_in

    def pad_rows(x):
        if rows_in == c:
            return x
        return jnp.concatenate([x, jnp.zeros((c - rows_in, x.shape[1]), x.dtype)], axis=0)

    def chunk(n, ci, st):
        row0 = pl.multiple_of(n * tlen + ci * rows_in, rows_in)
        rows = pl.ds(row0, rows_in)
        qr, fz, v = q_ref[rows, cols], f_ref[rows, cols], v_ref[rows, cols]
        e = jnp.exp(-jnp.abs(fz))
        one_p_e = 1.0 + e
        log_sig = jnp.minimum(fz, 0.0) - jnp.log(one_p_e)
        u = log1m_lb + log_sig
        log_f = jnp.maximum(log_lb, u) + jnp.log(1.0 + jnp.exp(-jnp.abs(log_lb - u)))
        k = one_m_lb * (jnp.where(fz > 0.0, e, 1.0) / one_p_e)
        q = jax.nn.silu(qr) * (HEAD ** -0.5)
        vb = pad_rows(v.astype(BF16))

        tri = tri_ref[...]
        b = sum(jnp.dot(tri, pad_rows(part), preferred_element_type=F32) for part in _split3(log_f))
        b = b * LOG2_E
        b_last = b[c - 1:c, :]
        b = b[0:rows_in, :]

        lev = lev_ref[...]
        diag = pad_rows(jnp.sum(q * k, axis=-1, keepdims=True))
        att = jnp.where(lev == -1, diag, 0.0)
        for l in range(n_levels):
            r = _block_boundary_rows(b, 2 << l)
            decay = jnp.exp2(-jnp.abs(b - r))
            qt = pad_rows((q * decay).astype(BF16))
            kt = pad_rows((k * decay).astype(BF16))
            s_l = lax.dot_general(qt, kt, _NT, preferred_element_type=F32)
            att = jnp.where(lev == l, s_l, att)
        o = jnp.dot(att.astype(BF16), vb, preferred_element_type=F32)
        o = o + lax.dot_general(pad_rows((q * jnp.exp2(b)).astype(BF16)), st.astype(BF16), _NT,
                                preferred_element_type=F32)
        kd = pad_rows((k * jnp.exp2(b_last - b)).astype(BF16))
        st_new = st * jnp.exp2(b_last) + lax.dot_general(vb, kd, _TN, preferred_element_type=F32)

        o = o[0:rows_in, :]
        o = o * lax.rsqrt(jnp.mean(o * o, axis=-1, keepdims=True) + EPS) * gain
        o = o * jax.nn.silu(g_ref[rows, cols])
        o_ref[rows, cols] = o.astype(o_ref.dtype)
        return st_new

    def sequence(n, carry):
        if has_s0:
            st0 = s0_ref[n, hh].T
        else:
            st0 = jnp.zeros((HEAD, HEAD), F32)
        st = lax.fori_loop(0, nchunk, lambda ci, st: chunk(n, ci, st), st0,
                           unroll=min(GLA_CHUNK_UNROLL, nchunk))
        so_ref[n, hh] = st.T
        return carry

    lax.fori_loop(0, nseq, sequence, 0, unroll=min(nseq, GLA_SEQ_UNROLL))


def _gla(proj, lb, gain, s0, *, nseq_total, tlen, out_dtype):
    m, d4 = proj.shape
    d = d4 // 4
    nh = d // HEAD
    nseq = 1 if tlen >= GLA_CHUNK else nseq_total
    assert tlen % GLA_CHUNK == 0 or tlen < GLA_CHUNK
    rows = nseq * tlen
    hps = GLA_HEADS_PER_STEP if nh % GLA_HEADS_PER_STEP == 0 else 1
    wide = hps * HEAD
    ngrp = nh // hps
    col = lambda off: pl.BlockSpec((rows, wide), lambda n, h: (n, h + off))
    vec = pl.BlockSpec((1, wide), lambda n, h: (0, h))
    const = pl.BlockSpec((GLA_CHUNK, GLA_CHUNK), lambda n, h: (0, 0))
    in_specs = [col(0), col(ngrp), col(2 * ngrp), col(3 * ngrp), vec, vec, const, const]
    args = [proj, proj, proj, proj, lb.reshape(1, d), gain.reshape(1, d),
            jnp.asarray(_level_table(GLA_CHUNK)), jnp.asarray(_cumsum_table(GLA_CHUNK), BF16)]
    state_spec = pl.BlockSpec((nseq, hps, HEAD, HEAD), lambda n, h: (n, h, 0, 0))
    if s0 is not None:
        in_specs.append(state_spec)
        args.append(s0)
    blocks = 2 * (4 * _nbytes((rows, wide), F32) + _nbytes((rows, wide), out_dtype)
                  + 2 * _nbytes((nseq, hps, HEAD, HEAD), F32)) + (4 << 20)
    return pl.pallas_call(
        functools.partial(_gla_body, nseq=nseq, tlen=tlen, valid=tlen, has_s0=s0 is not None,
                          heads=hps),
        grid=(nseq_total // nseq, ngrp),
        in_specs=in_specs,
        out_specs=[pl.BlockSpec((rows, wide), lambda n, h: (n, h)), state_spec],
        out_shape=[jax.ShapeDtypeStruct((m, d), out_dtype),
                   jax.ShapeDtypeStruct((nseq_total, nh, HEAD, HEAD), F32)],
        compiler_params=pltpu.CompilerParams(
            dimension_semantics=("arbitrary", "arbitrary"),
            vmem_limit_bytes=_vmem_limit(blocks)),
        name="hgrn2_recurrence",
    )(*args)


def _t5_buckets(dist):
    max_exact = N_BUCKETS // 2
    n = np.asarray(dist, dtype=np.int64)
    large = max_exact + (np.log(np.maximum(n, 1) / max_exact)
                         / np.log(BUCKET_MAX_DIST / max_exact)
                         * (N_BUCKETS - max_exact)).astype(np.int64)
    large = np.minimum(large, N_BUCKETS - 1)
    return np.where(n < max_exact, n, large).astype(np.int32)


def _bucket_lookup(tab, buckets):
    flat = jnp.asarray(buckets.reshape(1, -1))
    onehot = (flat == lax.broadcasted_iota(jnp.int32, (N_BUCKETS, flat.shape[1]), 0)).astype(F32)
    out = jnp.einsum("bh,bx->hx", tab, onehot, precision=lax.Precision.HIGHEST)
    return out.reshape((tab.shape[1],) + buckets.shape)


def _prompt_bias_tiles(rel_bias, nh):
    bl = SUB_WINDOW
    off = np.arange(bl)[:, None] + bl - np.arange(2 * bl)[None, :]
    band = (off >= 0) & (off <= SUB_WINDOW)
    tiles = []
    for g, dil in enumerate(DILATIONS):
        buckets = _t5_buckets(np.clip(off, 0, SUB_WINDOW) * dil)
        tab = rel_bias[:, g * nh:(g + 1) * nh].astype(F32)
        tiles.append(jnp.where(band[None], _bucket_lookup(tab, buckets), NEG))
    return jnp.stack(tiles)


ATTN_BLOCK_UNROLL = 16


def _attn_prompt_body(q1_ref, q2_ref, q3_ref, k_ref, v_ref, bias_ref, o_ref,
                      acc_ref, lse_ref, k4_ref, v4_ref, q34_ref, *, tlen):
    bl = SUB_WINDOW
    scale = HEAD ** -0.5
    mid = DILATIONS[1]
    seg = tlen // mid

    for src_ref, dst_ref in ((k_ref, k4_ref), (v_ref, v4_ref), (q3_ref, q34_ref)):
        for r in range(mid):
            dst_ref[r * seg:(r + 1) * seg, :] = src_ref[pl.ds(r, seg, stride=mid), :]

    for g, (dil, q_ref) in enumerate(zip(DILATIONS, (q1_ref, q2_ref, q3_ref))):
        nblk = tlen // (dil * bl)
        with_prev = nblk > 1

        def block(it, carry, dil=dil, q_ref=q_ref, nblk=nblk, g=g, with_prev=with_prev):
            r = it // nblk
            bi = it % nblk
            start = r + dil * bl * bi
            qrows = pl.ds(start, bl, stride=dil) if dil > 1 else pl.ds(start, bl)
            if with_prev:
                nkey = 2 * bl
                bias = bias_ref[g, 0]
                bias_first = jnp.concatenate([bias[:, bl:2 * bl], jnp.full((bl, bl), NEG, F32)], axis=1)
                bias = jnp.where(bi == 0, bias_first, bias)
                back = jnp.where(bi > 0, bl, 0)
            else:
                nkey = bl
                bias = bias_ref[g, 0, :, bl:2 * bl]
                back = 0
            if dil == 1:
                krows = pl.ds(start - back, nkey)
                qb, k2, v2 = q_ref[qrows, :], k_ref[krows, :], v_ref[krows, :]
            elif dil == mid:
                krows = pl.ds(r * seg + bl * bi - back, nkey)
                qb, k2, v2 = q_ref[qrows, :], k4_ref[krows, :], v4_ref[krows, :]
            else:
                sub = dil // mid
                rows4 = pl.ds((r % mid) * seg + r // mid + sub * (bl * bi - back), nkey, stride=sub)
                q4rows = pl.ds((r % mid) * seg + r // mid + sub * bl * bi, bl, stride=sub)
                qb, k2, v2 = q34_ref[q4rows, :], k4_ref[rows4, :], v4_ref[rows4, :]
            qb, k2, v2 = qb.astype(BF16), k2.astype(BF16), v2.astype(BF16)
            s = lax.dot_general(qb, k2, _NT, preferred_element_type=F32) * scale + bias
            mx = jnp.max(s, axis=-1, keepdims=True)
            p = jnp.exp(s - mx)
            sm = jnp.sum(p, axis=-1, keepdims=True)
            acc = jnp.dot(p.astype(BF16), v2, preferred_element_type=F32)
            acc_ref[g, qrows, :] = acc / sm
            lse_ref[g, qrows, :] = jnp.broadcast_to(mx + jnp.log(sm), (bl, HEAD))
            return carry

        def residue_pair(it, carry, dil=dil, g=g):
            sub = dil // mid
            parts = []
            for r in (2 * it, 2 * it + 1):
                rows4 = pl.ds((r % mid) * seg + r // mid, bl, stride=sub)
                parts.append((q34_ref[rows4, :], k4_ref[rows4, :], v4_ref[rows4, :],
                              pl.ds(r, bl, stride=dil)))
            qb = jnp.concatenate([pt[0] for pt in parts], axis=0).astype(BF16)
            k2 = jnp.concatenate([pt[1] for pt in parts], axis=0).astype(BF16)
            v2 = jnp.concatenate([pt[2] for pt in parts], axis=0).astype(BF16)
            own = bias_ref[g, 0, :, bl:2 * bl]
            neg = jnp.full((bl, bl), NEG, F32)
            bias = jnp.concatenate([jnp.concatenate([own, neg], axis=1),
                                    jnp.concatenate([neg, own], axis=1)], axis=0)
            s = lax.dot_general(qb, k2, _NT, preferred_element_type=F32) * scale + bias
            mx = jnp.max(s, axis=-1, keepdims=True)
            p = jnp.exp(s - mx)
            sm = jnp.sum(p, axis=-1, keepdims=True)
            o = jnp.dot(p.astype(BF16), v2, preferred_element_type=F32) / sm
            lse = jnp.broadcast_to(mx + jnp.log(sm), (2 * bl, HEAD))
            for half, pt in enumerate(parts):
                acc_ref[g, pt[3], :] = o[half * bl:(half + 1) * bl, :]
                lse_ref[g, pt[3], :] = lse[half * bl:(half + 1) * bl, :]
            return carry

        if with_prev:
            lax.fori_loop(0, dil * nblk, block, 0, unroll=ATTN_BLOCK_UNROLL)
        else:
            lax.fori_loop(0, dil // 2, residue_pair, 0, unroll=ATTN_BLOCK_UNROLL // 2)

    def merge(ci, carry):
        rows = pl.ds(pl.multiple_of(ci * bl, bl), bl)
        l1, l2, l3 = lse_ref[0, rows, :], lse_ref[1, rows, :], lse_ref[2, rows, :]
        mm = jnp.maximum(jnp.maximum(l1, l2), l3)
        w1, w2, w3 = jnp.exp(l1 - mm), jnp.exp(l2 - mm), jnp.exp(l3 - mm)
        num = w1 * acc_ref[0, rows, :] + w2 * acc_ref[1, rows, :] + w3 * acc_ref[2, rows, :]
        o_ref[rows, :] = (num / (w1 + w2 + w3)).astype(o_ref.dtype)
        return carry

    lax.fori_loop(0, tlen // bl, merge, 0, unroll=2)


def _attn_prompt(qproj, k, v, bias_tiles, *, nseq, tlen):
    m, d3 = qproj.shape
    d = d3 // 3
    nh = d // HEAD
    qspec = lambda g: pl.BlockSpec((tlen, HEAD), lambda n, h: (n, g * nh + h))
    kvspec = pl.BlockSpec((tlen, HEAD), lambda n, h: (n, h))
    blocks = (2 * (5 * _nbytes((tlen, HEAD), F32) + _nbytes((3, 1, SUB_WINDOW, 2 * SUB_WINDOW), F32)
                   + _nbytes((tlen, HEAD), BF16))
              + 9 * _nbytes((tlen, HEAD), F32))
    return pl.pallas_call(
        functools.partial(_attn_prompt_body, tlen=tlen),
        grid=(nseq, nh),
        in_specs=[qspec(0), qspec(1), qspec(2), kvspec, kvspec,
                  pl.BlockSpec((3, 1, SUB_WINDOW, 2 * SUB_WINDOW), lambda n, h: (0, h, 0, 0))],
        out_specs=pl.BlockSpec((tlen, HEAD), lambda n, h: (n, h)),
        out_shape=jax.ShapeDtypeStruct((m, d), BF16),
        scratch_shapes=[pltpu.VMEM((3, tlen, HEAD), F32), pltpu.VMEM((3, tlen, HEAD), F32),
                        pltpu.VMEM((tlen, HEAD), F32), pltpu.VMEM((tlen, HEAD), F32),
                        pltpu.VMEM((tlen, HEAD), F32)],
        compiler_params=pltpu.CompilerParams(
            dimension_semantics=("arbitrary", "arbitrary"),
            vmem_limit_bytes=_vmem_limit(blocks)),
        name="dilated_attention_prompt",
    )(qproj, qproj, qproj, k, v, bias_tiles)


SAMPLE_HEADS_PER_STEP = V7X_SUBLANES


def _sample_key_rows(p_len, ts):
    period = max(DILATIONS)
    assert p_len % period == 0 and ts < period
    n_i = p_len // period
    tail_i0 = (p_len - DILATIONS[1] * SUB_WINDOW) // period
    assert tail_i0 >= 0 and DILATIONS[0] * SUB_WINDOW <= DILATIONS[1] * SUB_WINDOW
    main = (np.arange(n_i)[:, None] * period + np.arange(ts)[None, :]).reshape(-1)
    tail = (np.arange(tail_i0, n_i)[:, None] * period + np.arange(ts, period)[None, :]).reshape(-1)
    rows = np.concatenate([main, tail, p_len + np.arange(ts)])
    nkeys = -(-rows.size // V7X_LANES) * V7X_LANES
    return np.concatenate([rows, np.full(nkeys - rows.size, -1)]), n_i, n_i - tail_i0


def _sample_bias_table(rel_bias, nh, p_len, ts, key_rows):
    t = np.arange(ts)[:, None]
    row = key_rows[None, :]
    dist = p_len + t - row
    out = []
    for g, dil in enumerate(DILATIONS):
        j = dist // dil
        ok = (row >= 0) & (dist >= 0) & (dist % dil == 0) & (j <= SUB_WINDOW)
        buckets = _t5_buckets(np.clip(j, 0, SUB_WINDOW) * dil)
        tab = rel_bias[:, g * nh:(g + 1) * nh].astype(F32)
        out.append(jnp.where(ok[None], _bucket_lookup(tab, buckets), NEG))
    return jnp.concatenate(out, axis=1)


def _attn_sample_body(q1_ref, q2_ref, q3_ref, km_ref, kt_ref, vm_ref, vt_ref, kn_ref, vn_ref,
                      bias_ref, o_ref, kall_ref, vall_ref, *, ts):
    nkeys = kall_ref.shape[0]
    n_main = km_ref.shape[0] * km_ref.shape[1]
    n_tail = kt_ref.shape[0] * kt_ref.shape[1]
    n_used = n_main + n_tail + ts
    scale = HEAD ** -0.5
    kall_ref[n_used:nkeys, :] = jnp.zeros((nkeys - n_used, HEAD), F32)
    vall_ref[n_used:nkeys, :] = jnp.zeros((nkeys - n_used, HEAD), F32)
    for hh in range(SAMPLE_HEADS_PER_STEP):
        cols = slice(hh * HEAD, (hh + 1) * HEAD)
        for src_main, src_tail, src_new, dst in ((km_ref, kt_ref, kn_ref, kall_ref),
                                                 (vm_ref, vt_ref, vn_ref, vall_ref)):
            flat_main = src_main.reshape(n_main * SAMPLE_HEADS_PER_STEP, HEAD)
            flat_tail = src_tail.reshape(n_tail * SAMPLE_HEADS_PER_STEP, HEAD)
            dst[0:n_main, :] = flat_main[pl.ds(hh, n_main, stride=SAMPLE_HEADS_PER_STEP), :]
            dst[n_main:n_main + n_tail, :] = flat_tail[pl.ds(hh, n_tail, stride=SAMPLE_HEADS_PER_STEP), :]
            dst[n_main + n_tail:n_used, :] = src_new[:, cols]
        kb = kall_ref[...].astype(BF16)
        vb = vall_ref[...].astype(BF16)
        bias = bias_ref[hh]
        s = []
        for g, q_ref in enumerate((q1_ref, q2_ref, q3_ref)):
            sg = lax.dot_general(q_ref[:, cols].astype(BF16), kb, _NT, preferred_element_type=F32)
            s.append(sg * scale + bias[g * ts:(g + 1) * ts, :])
        mx = [jnp.max(x, axis=-1, keepdims=True) for x in s]
        mm = jnp.maximum(jnp.maximum(mx[0], mx[1]), mx[2])
        p = jnp.exp(s[0] - mm) + jnp.exp(s[1] - mm) + jnp.exp(s[2] - mm)
        den = jnp.sum(p, axis=-1, keepdims=True)
        o = jnp.dot(p.astype(BF16), vb, preferred_element_type=F32)
        o_ref[:, cols] = (o / den).astype(o_ref.dtype)


def _attn_sample(qproj, k_new, v_new, cache_k, cache_v, bias_table, n_i, n_tail, *, nseq, ts):
    m, d3 = qproj.shape
    d = d3 // 3
    nh = d // HEAD
    hps = SAMPLE_HEADS_PER_STEP
    assert nh % hps == 0 and n_i % n_tail == 0
    p_len = cache_k.shape[1]
    period = p_len // n_i
    nkeys = bias_table.shape[-1]
    ngrp = nh // hps
    wide = hps * HEAD
    qspec = lambda g: pl.BlockSpec((ts, wide), lambda n, h: (n, g * ngrp + h))
    newspec = pl.BlockSpec((ts, wide), lambda n, h: (n, h))
    mainspec = pl.BlockSpec((None, n_i, ts, hps, HEAD), lambda n, h: (n, 0, 0, h, 0))
    tailspec = pl.BlockSpec((None, n_tail, period - ts, hps, HEAD),
                            lambda n, h: (n, n_i // n_tail - 1, 1, h, 0))
    assert period - ts == ts
    cache_bytes = _nbytes((n_i + n_tail, ts, hps, HEAD), F32)
    blocks = (2 * (2 * cache_bytes + _nbytes((hps, 3 * ts, nkeys), F32))
              + 2 * _nbytes((nkeys, HEAD), F32) + 2 * _nbytes((nkeys, HEAD), BF16)
              + 8 * _nbytes((3 * ts, nkeys), F32))
    k5 = cache_k.reshape(nseq, n_i, period, nh, HEAD)
    v5 = cache_v.reshape(nseq, n_i, period, nh, HEAD)
    return pl.pallas_call(
        functools.partial(_attn_sample_body, ts=ts),
        grid=(nseq, ngrp),
        in_specs=[qspec(0), qspec(1), qspec(2), mainspec, tailspec, mainspec, tailspec,
                  newspec, newspec,
                  pl.BlockSpec((hps, 3 * ts, nkeys), lambda n, h: (h, 0, 0))],
        out_specs=pl.BlockSpec((ts, wide), lambda n, h: (n, h)),
        out_shape=jax.ShapeDtypeStruct((m, d), F32),
        scratch_shapes=[pltpu.VMEM((nkeys, HEAD), F32), pltpu.VMEM((nkeys, HEAD), F32)],
        compiler_params=pltpu.CompilerParams(
            dimension_semantics=("arbitrary", "arbitrary"),
            vmem_limit_bytes=_vmem_limit(blocks)),
        name="dilated_attention_sample",
    )(qproj, qproj, qproj, k5, k5, v5, v5, k_new, v_new, bias_table)


def _lower_bounds(lb_raw):
    p = jax.nn.softmax(lb_raw.astype(F32), axis=0)
    c = jnp.cumsum(p, axis=0)
    return c - c[0:1]


def kernel(x_prompt, x_sample, state_hgrn, state_ffn_conv, cache_k_win, cache_v_win, a_w_in, a_lb_raw, a_out_gain, a_w_out, kv_norm, w_kv, b_w_q, b_w_out, rel_bias, ffn_w_up, ffn_conv_w, ffn_conv_b, ffn_w_down, norm_mix, norm_ffn, norm_final):
    nb_p, t_p, d = x_prompt.shape
    nb_s, t_s, _ = x_sample.shape
    depth = ffn_w_up.shape[0]
    n_a = a_w_in.shape[0]
    nh = d // HEAD
    f = ffn_w_down.shape[1]
    p_len = cache_k_win.shape[1]
    assert t_p % (SUB_WINDOW * max(DILATIONS)) == 0 and t_s == V7X_SUBLANES

    xp = x_prompt.reshape(nb_p * t_p, d)
    xs = x_sample.reshape(nb_s * t_s, d)
    lbs = _lower_bounds(a_lb_raw)
    key_rows, n_i, n_tail = _sample_key_rows(p_len, t_s)
    bias_p = _prompt_bias_tiles(rel_bias, nh)
    bias_s = _sample_bias_table(rel_bias, nh, p_len, t_s, key_rows)

    hgrn_p, hgrn_s, conv_p, conv_s = [], [], [], []
    kp = vp = ks = vs = None
    w_kv3 = w_kv.reshape(1, d, 2 * d)
    conv_b3 = ffn_conv_b.reshape(depth, 1, f)
    for l in range(depth):
        gains = [norm_mix[l], kv_norm] if l == n_a else [norm_mix[l]]
        hp, *kvp = _rmsnorm(xp, gains, BF16)
        hs, *kvs = _rmsnorm(xs, gains, BF16)
        if l < n_a:
            pp, ps = _linear(hp, hs, a_w_in, l, name="hgrn2_in_proj")
            op, sp = _gla(pp, lbs[l], a_out_gain[l], None, nseq_total=nb_p, tlen=t_p, out_dtype=BF16)
            os_, ss = _gla(ps, lbs[l], a_out_gain[l], state_hgrn[l], nseq_total=nb_s, tlen=t_s,
                           out_dtype=F32)
            hgrn_p.append(sp)
            hgrn_s.append(ss)
            w_o, l_o = a_w_out, l
        else:
            if l == n_a:
                kp, ks = _linear(kvp[0], kvs[0], w_kv3, 0, col0=0, n=d, name="k_proj")
                vp, vs = _linear(kvp[0], kvs[0], w_kv3, 0, col0=d, n=d, name="v_proj")
            jb = l - n_a
            qp, qs = _linear(hp, hs, b_w_q, jb, name="q_proj")
            op = _attn_prompt(qp, kp, vp, bias_p, nseq=nb_p, tlen=t_p)
            os_ = _attn_sample(qs, ks, vs, cache_k_win, cache_v_win, bias_s, n_i, n_tail,
                               nseq=nb_s, ts=t_s)
            w_o, l_o = b_w_out, jb
        xp, xs = _linear(op, os_, w_o, l_o, res=(xp, xs), name="mixer_out_proj")

        hp, = _rmsnorm(xp, [norm_ffn[l]], BF16)
        hs, = _rmsnorm(xs, [norm_ffn[l]], BF16)
        prev = state_ffn_conv[l]
        ext_s = jnp.pad(prev, ((0, 0), (0, t_s - (CONV_W - 1)), (0, 0))).reshape(nb_s * t_s, f)
        actp, acts, tailp, a_s = _ffn_up(hp, hs, ffn_w_up, ffn_conv_w, conv_b3, ext_s, l,
                                         tlen=t_p, ts=t_s)
        conv_p.append(tailp.reshape(nb_p, 8, f)[:, 8 - (CONV_W - 1):, :])
        conv_s.append(a_s.reshape(nb_s, t_s, f)[:, t_s - (CONV_W - 1):, :])
        xp, xs = _linear(actp, acts, ffn_w_down, l, res=(xp, xs), name="ffn_down_proj")

    yp = _rmsnorm(xp, [norm_final], F32)[0].reshape(nb_p, t_p, d)
    ys = _rmsnorm(xs, [norm_final], F32)[0].reshape(nb_s, t_s, d)
    keep = min(BUCKET_MAX_DIST, t_p)
    k_win = kp.reshape(nb_p, t_p, nh, HEAD)[:, t_p - keep:]
    v_win = vp.reshape(nb_p, t_p, nh, HEAD)[:, t_p - keep:]
    return (yp, ys, jnp.stack(hgrn_p), jnp.stack(hgrn_s), jnp.stack(conv_p), jnp.stack(conv_s),
            k_win, v_win, ks.reshape(nb_s, t_s, nh, HEAD), vs.reshape(nb_s, t_s, nh, HEAD))
```
